```python
import jax, jax.numpy as jnp
from jax import lax
import numpy as np

D_MODEL = 2048
BATCH = 4
SEQ = 2048
DEPTH = 2

ATTN_HEADS = 8
HEAD_DIM = 128
ATTN_WIDTH = ATTN_HEADS * HEAD_DIM
MOBA_BLOCK = 256
MOBA_TOPK = 3
Q_CHUNK = 16
SGU_GROUPS = 8
SGU_GROUP_DIM = 128
SGU_WIDTH = SGU_GROUPS * SGU_GROUP_DIM
SGU_CHUNK = 128
D_FF = 5632
N_EXPERTS = 8
TOP_K = 2
D_FF_EXPERT = 5632
N_DENSE = (DEPTH + 1) // 2
N_MOE = DEPTH // 2
NORM_EPS = 1e-6
IN_SPLITS = (ATTN_WIDTH, ATTN_WIDTH, ATTN_WIDTH, SGU_WIDTH, SGU_WIDTH, D_MODEL, D_MODEL)
IN_WIDTH = sum(IN_SPLITS)

kernel_name = "hybrid_moba_sgu_moe_block"


def rmsnorm(x, g):
    xf = x.astype(jnp.float32)
    y = xf * lax.rsqrt(jnp.mean(xf * xf, axis=-1, keepdims=True) + NORM_EPS) * g.astype(jnp.float32)
    return y.astype(x.dtype)


def layernorm(x, g):
    xf = x.astype(jnp.float32)
    mu = jnp.mean(xf, axis=-1, keepdims=True)
    xc = xf - mu
    y = xc * lax.rsqrt(jnp.mean(xc * xc, axis=-1, keepdims=True) + NORM_EPS) * g.astype(jnp.float32)
    return y.astype(x.dtype)


def moba_attention(q, k, v):
    B, S, H, Dh = q.shape
    nb = -(-S // MOBA_BLOCK)
    sp = nb * MOBA_BLOCK
    pad = ((0, 0), (0, sp - S), (0, 0), (0, 0))
    q, k, v = [jnp.pad(t, pad).transpose(0, 2, 1, 3) for t in (q, k, v)]
    scale = Dh ** -0.5
    kb = k.reshape(B, H, nb, MOBA_BLOCK, Dh)
    vb = v.reshape(B, H, nb, MOBA_BLOCK, Dh)
    k_mean = jnp.mean(kb.astype(jnp.float32), axis=3)
    gate = jnp.einsum('bhsd,bhnd->bhsn', q.astype(jnp.float32), k_mean)
    q_blk = jnp.arange(sp) // MOBA_BLOCK
    past = jnp.arange(nb)[None, :] < q_blk[:, None]
    gate = jnp.where(past, gate, -jnp.inf)
    k_sel = min(MOBA_TOPK, nb)
    _, sel = lax.top_k(gate, k_sel)
    sel_valid = sel < q_blk[:, None]
    b_ix = jnp.arange(B)[:, None, None, None]
    h_ix = jnp.arange(H)[None, :, None, None]

    def attend_chunk(c):
        start = c * Q_CHUNK
        qc = lax.dynamic_slice_in_dim(q, start, Q_CHUNK, axis=2)
        sc = lax.dynamic_slice_in_dim(sel, start, Q_CHUNK, axis=2)
        vc = lax.dynamic_slice_in_dim(sel_valid, start, Q_CHUNK, axis=2)
        kg = kb[b_ix, h_ix, sc]
        vg = vb[b_ix, h_ix, sc]
        s_sel = jnp.einsum('bhqd,bhqkjd->bhqkj', qc, kg).astype(jnp.float32) * scale
        s_sel = jnp.where(vc[..., None], s_sel, -jnp.inf).reshape(B, H, Q_CHUNK, k_sel * MOBA_BLOCK)
        own = start // MOBA_BLOCK
        ko = lax.dynamic_slice_in_dim(k, own * MOBA_BLOCK, MOBA_BLOCK, axis=2)
        vo = lax.dynamic_slice_in_dim(v, own * MOBA_BLOCK, MOBA_BLOCK, axis=2)
        s_own = jnp.einsum('bhqd,bhjd->bhqj', qc, ko).astype(jnp.float32) * scale
        q_pos = start + jnp.arange(Q_CHUNK)
        k_pos = own * MOBA_BLOCK + jnp.arange(MOBA_BLOCK)
        s_own = jnp.where(k_pos[None, :] <= q_pos[:, None], s_own, -jnp.inf)
        p = jax.nn.softmax(jnp.concatenate([s_sel, s_own], axis=-1), axis=-1)
        p_sel = p[..., :k_sel * MOBA_BLOCK].reshape(B, H, Q_CHUNK, k_sel, MOBA_BLOCK).astype(v.dtype)
        p_own = p[..., k_sel * MOBA_BLOCK:].astype(v.dtype)
        return (jnp.einsum('bhqkj,bhqkjd->bhqd', p_sel, vg)
                + jnp.einsum('bhqj,bhjd->bhqd', p_own, vo))

    outs = lax.map(attend_chunk, jnp.arange(sp // Q_CHUNK))
    out = outs.transpose(1, 0, 3, 2, 4).reshape(B, sp, H, Dh)
    return out[:, :S]


def spatial_gating(u, vg, g_sgu, w_s, b_s):
    B, S, _ = u.shape
    u = jax.nn.gelu(u)
    vn = layernorm(jax.nn.gelu(vg), g_sgu)
    nc = S // SGU_CHUNK
    vr = vn.reshape(B, nc, SGU_CHUNK, SGU_GROUPS, SGU_GROUP_DIM)
    mask = jnp.tril(jnp.ones((SGU_CHUNK, SGU_CHUNK), dtype=bool))
    w = jnp.where(mask, w_s, jnp.zeros_like(w_s))
    mixed = jnp.einsum('gts,bcsgd->bctgd', w, vr) + b_s.T[None, None, :, :, None]
    return u * mixed.reshape(B, S, SGU_WIDTH)


def token_mixer(h, w_in, g_sgu, w_s, b_s, w_pa, w_pb, w_o):
    B, S, _ = h.shape
    z = h @ w_in
    offs = np.cumsum(IN_SPLITS)[:-1].tolist()
    q, k, v, u, vg, ga, gb = jnp.split(z, offs, axis=-1)
    shp = (B, S, ATTN_HEADS, HEAD_DIM)
    attn = moba_attention(q.reshape(shp), k.reshape(shp), v.reshape(shp)).reshape(B, S, ATTN_WIDTH)
    sgu = spatial_gating(u, vg, g_sgu, w_s, b_s)
    merged = jax.nn.sigmoid(ga) * (attn @ w_pa) + jax.nn.sigmoid(gb) * (sgu @ w_pb)
    return merged @ w_o


def swiglu(t, wg, wu, wd):
    return (jax.nn.silu(t @ wg) * (t @ wu)) @ wd


def moe_swiglu(h, w_router, b_router, wg, wu, wd):
    B, S, D = h.shape
    t = h.reshape(B * S, D)
    logits = (t @ w_router).astype(jnp.float32) + b_router.astype(jnp.float32)
    top_val, top_idx = lax.top_k(logits, TOP_K)
    top_w = jax.nn.softmax(top_val, axis=-1)
    combine = jnp.sum(jax.nn.one_hot(top_idx, N_EXPERTS, dtype=jnp.float32) * top_w[..., None], axis=1)
    combine = combine.astype(t.dtype)
    out = jnp.zeros_like(t)
    for e in range(N_EXPERTS):
        out = out + combine[:, e:e + 1] * swiglu(t, wg[e], wu[e], wd[e])
    return out.reshape(B, S, D)


def setup_inputs(seed: int = 0) -> dict:
    key = jax.random.key(seed)
    ks = jax.random.split(key, 24)
    f32 = jnp.float32
    nrm = lambda k, shape, s: jax.random.normal(k, shape, f32) * s
    return {
        "x": nrm(ks[0], (BATCH, SEQ, D_MODEL), 1.0),
        "mix_norm_g": 1.0 + nrm(ks[1], (DEPTH, D_MODEL), 0.02),
        "w_in": nrm(ks[2], (DEPTH, D_MODEL, IN_WIDTH), D_MODEL ** -0.5),
        "sgu_norm_g": 1.0 + nrm(ks[3], (DEPTH, SGU_WIDTH), 0.02),
        "w_s": nrm(ks[4], (DEPTH, SGU_GROUPS, SGU_CHUNK, SGU_CHUNK), SGU_CHUNK ** -0.5),
        "b_s": 1.0 + nrm(ks[5], (DEPTH, SGU_GROUPS, SGU_CHUNK), 0.01),
        "w_pa": nrm(ks[6], (DEPTH, ATTN_WIDTH, D_MODEL), ATTN_WIDTH ** -0.5),
        "w_pb": nrm(ks[7], (DEPTH, SGU_WIDTH, D_MODEL), SGU_WIDTH ** -0.5),
        "w_o": nrm(ks[8], (DEPTH, D_MODEL, D_MODEL), D_MODEL ** -0.5),
        "ffn_norm_g": 1.0 + nrm(ks[9], (DEPTH, D_MODEL), 0.02),
        "dense_w_gate": nrm(ks[10], (N_DENSE, D_MODEL, D_FF), D_MODEL ** -0.5),
        "dense_w_up": nrm(ks[11], (N_DENSE, D_MODEL, D_FF), D_MODEL ** -0.5),
        "dense_w_down": nrm(ks[12], (N_DENSE, D_FF, D_MODEL), D_FF ** -0.5),
        "router_w": nrm(ks[13], (N_MOE, D_MODEL, N_EXPERTS), D_MODEL ** -0.5),
        "router_b": nrm(ks[14], (N_MOE, N_EXPERTS), 0.01),
        "expert_w_gate": nrm(ks[15], (N_MOE, N_EXPERTS, D_MODEL, D_FF_EXPERT), D_MODEL ** -0.5),
        "expert_w_up": nrm(ks[16], (N_MOE, N_EXPERTS, D_MODEL, D_FF_EXPERT), D_MODEL ** -0.5),
        "expert_w_down": nrm(ks[17], (N_MOE, N_EXPERTS, D_FF_EXPERT, D_MODEL), D_FF_EXPERT ** -0.5),
        "final_norm_g": 1.0 + nrm(ks[18], (D_MODEL,), 0.02),
    }


def reference(x, mix_norm_g, w_in, sgu_norm_g, w_s, b_s, w_pa, w_pb, w_o, ffn_norm_g,
              dense_w_gate, dense_w_up, dense_w_down, router_w, router_b,
              expert_w_gate, expert_w_up, expert_w_down, final_norm_g):
    for i in range(DEPTH):
        h = rmsnorm(x, mix_norm_g[i])
        x = x + token_mixer(h, w_in[i], sgu_norm_g[i], w_s[i], b_s[i], w_pa[i], w_pb[i], w_o[i])
        h = rmsnorm(x, ffn_norm_g[i])
        j = i // 2
        if i % 2 == 0:
            x = x + swiglu(h, dense_w_gate[j], dense_w_up[j], dense_w_down[j])
        else:
            x = x + moe_swiglu(h, router_w[j], router_b[j], expert_w_gate[j], expert_w_up[j], expert_w_down[j])
    return rmsnorm(x, final_norm_g)
```

```python
import functools

import jax
import jax.numpy as jnp
from jax import lax
from jax.experimental import pallas as pl
from jax.experimental.pallas import tpu as pltpu

F32 = jnp.float32
BF16 = jnp.bfloat16

ATTN_HEADS = 8
HEAD_DIM = 128
MOBA_BLOCK = 256
MOBA_TOPK = 3
SGU_GROUPS = 8
SGU_GROUP_DIM = 128
SGU_CHUNK = 128
TOP_K = 2
NORM_EPS = 1e-6

V7X_VMEM_BYTES = 64 * 1024 * 1024
VMEM_LIMIT_BYTES = V7X_VMEM_BYTES - 8 * 1024 * 1024

ROW_TILE = 512
CAST_ROWS = 256


def _cparams(sem):
    return pltpu.CompilerParams(dimension_semantics=sem, vmem_limit_bytes=VMEM_LIMIT_BYTES)


def _rms(x, g):
    ms = jnp.mean(x * x, axis=-1, keepdims=True)
    return x * lax.rsqrt(ms + NORM_EPS) * g


def _rmsnorm_kernel(x_ref, g_ref, o_ref):
    o_ref[...] = _rms(x_ref[...], g_ref[...]).astype(o_ref.dtype)


def rmsnorm(x, g, out_dtype, tm=ROW_TILE):
    n, d = x.shape
    return pl.pallas_call(
        _rmsnorm_kernel,
        out_shape=jax.ShapeDtypeStruct((n, d), out_dtype),
        grid=(n // tm,),
        in_specs=[pl.BlockSpec((tm, d), lambda i: (i, 0)),
                  pl.BlockSpec((1, d), lambda i: (0, 0))],
        out_specs=pl.BlockSpec((tm, d), lambda i: (i, 0)),
        compiler_params=_cparams(("parallel",)),
        name="rmsnorm",
    )(x, g.reshape(1, d))


def _ws_kernel(texp_ref, *refs, n_lhs, w_lhs, n_extra, epilogue):
    n_w = len(w_lhs)
    lhs_refs = refs[:n_lhs]
    w_refs = refs[n_lhs:n_lhs + n_w]
    extra_refs = refs[n_lhs + n_w:n_lhs + n_w + n_extra]
    out_ref = refs[n_lhs + n_w + n_extra]
    wb_refs = refs[n_lhs + n_w + n_extra + 1:]

    r = pl.program_id(1)
    new_group = jnp.logical_or(r == 0, texp_ref[r] != texp_ref[jnp.maximum(r - 1, 0)])

    @pl.when(new_group)
    def _():
        for w_ref, wb_ref in zip(w_refs, wb_refs):
            k = w_ref.shape[0]

            def cast_rows(c, carry, w_ref=w_ref, wb_ref=wb_ref):
                rows = pl.ds(pl.multiple_of(c * CAST_ROWS, CAST_ROWS), CAST_ROWS)
                wb_ref[rows, :] = w_ref[rows, :].astype(BF16)
                return carry

            lax.fori_loop(0, k // CAST_ROWS, cast_rows, 0)

    prods = [jnp.dot(lhs_refs[w_lhs[i]][...], wb_refs[i][...], preferred_element_type=F32)
             for i in range(n_w)]
    out_ref[...] = epilogue(prods, [e[...] for e in extra_refs]).astype(out_ref.dtype)


def ws_matmul(lhs, ws, w_lhs, extras, extra_col_off, epilogue, out_dtype, tn, texp=None,
              w_base=0, tm=ROW_TILE, name="ws_matmul"):
    m = lhs[0].shape[0]
    n_total = ws[0].shape[2]
    n_rows = m // tm
    if texp is None:
        texp = jnp.zeros((n_rows,), jnp.int32)
    texp = texp + w_base
    in_specs = []
    for a in lhs:
        in_specs.append(pl.BlockSpec((tm, a.shape[1]), lambda n, r, t: (r, 0)))
    for w in ws:
        in_specs.append(pl.BlockSpec((None, w.shape[1], tn), lambda n, r, t: (t[r], 0, n)))
    for off in extra_col_off:
        in_specs.append(pl.BlockSpec((tm, tn), lambda n, r, t, off=off: (r, off + n)))
    kern = functools.partial(_ws_kernel, n_lhs=len(lhs), w_lhs=tuple(w_lhs),
                             n_extra=len(extras), epilogue=epilogue)
    return pl.pallas_call(
        kern,
        out_shape=jax.ShapeDtypeStruct((m, n_total), out_dtype),
        grid_spec=pltpu.PrefetchScalarGridSpec(
            num_scalar_prefetch=1,
            grid=(n_total // tn, n_rows),
            in_specs=in_specs,
            out_specs=pl.BlockSpec((tm, tn), lambda n, r, t: (r, n)),
            scratch_shapes=[pltpu.VMEM((w.shape[1], tn), BF16) for w in ws],
        ),
        compiler_params=_cparams(("arbitrary", "arbitrary")),
        name=name,
    )(texp, *lhs, *ws, *extras)


def _ep_plain(prods, extras):
    return prods[0]


def _ep_residual(prods, extras):
    return extras[0] + prods[0]


def _ep_swiglu(prods, extras):
    g, u = prods
    return (g * jax.nn.sigmoid(g)) * u


def _ep_merge(prods, extras):
    ga, gb = extras
    return (jax.nn.sigmoid(ga.astype(F32)) * prods[0]
            + jax.nn.sigmoid(gb.astype(F32)) * prods[1])


def _attn_kernel(q_ref, k_ref, v_ref, o_ref, *, nb, blk, topk, scale):
    seq = k_ref.shape[0]
    neg_inf = jnp.float32(-jnp.inf)
    avg = jnp.where(lax.broadcasted_iota(jnp.int32, (nb, seq), 1) // blk
                    == lax.broadcasted_iota(jnp.int32, (nb, seq), 0), 1.0 / blk, 0.0).astype(BF16)
    k_mean = jnp.dot(avg, k_ref[...], preferred_element_type=F32)
    blk_id = lax.broadcasted_iota(jnp.int32, (blk, nb), 1)

    for n in range(nb):
        qn = q_ref[n * blk:(n + 1) * blk, :]
        nk = (n + 1) * blk
        s = lax.dot_general(qn, k_ref[0:nk, :], (((1,), (1,)), ((), ())),
                            preferred_element_type=F32) * scale
        k_pos = lax.broadcasted_iota(jnp.int32, (blk, nk), 1)
        q_pos = lax.broadcasted_iota(jnp.int32, (blk, nk), 0) + n * blk
        mask = k_pos <= q_pos
        if n > topk:
            gate = lax.dot_general(qn.astype(F32), k_mean, (((1,), (1,)), ((), ())),
                                   precision=lax.Precision.HIGHEST,
                                   preferred_element_type=F32)
            beaten = jnp.zeros((blk, nb), F32)
            for j in range(n):
                gj = gate[:, j:j + 1]
                wins = jnp.logical_or(gj > gate, jnp.logical_and(gj == gate, j < blk_id))
                beaten = beaten + wins.astype(F32)
            sel = jnp.where(beaten < topk, 1.0, 0.0)
            picked = jnp.concatenate(
                [jnp.broadcast_to(sel[:, j:j + 1], (blk, blk)) for j in range(n)]
                + [jnp.ones((blk, blk), F32)], axis=1)
            mask = jnp.logical_and(mask, picked > 0.5)
        s = jnp.where(mask, s, neg_inf)
        m = jnp.max(s, axis=-1, keepdims=True)
        p = jnp.exp(s - m)
        l = jnp.sum(p, axis=-1, keepdims=True)
        o = jnp.dot(p.astype(BF16), v_ref[0:nk, :], preferred_element_type=F32)
        o_ref[n * blk:(n + 1) * blk, :] = (o / l).astype(o_ref.dtype)


def moba_attention(z, batch, seq):
    nb = seq // MOBA_BLOCK
    kern = functools.partial(_attn_kernel, nb=nb, blk=MOBA_BLOCK, topk=MOBA_TOPK,
                             scale=HEAD_DIM ** -0.5)
    blockspec = lambda off: pl.BlockSpec((None, seq, HEAD_DIM), lambda b, h: (b, 0, off + h))
    return pl.pallas_call(
        kern,
        out_shape=jax.ShapeDtypeStruct((batch, seq, ATTN_HEADS * HEAD_DIM), BF16),
        grid=(batch, ATTN_HEADS),
        in_specs=[blockspec(0), blockspec(ATTN_HEADS), blockspec(2 * ATTN_HEADS)],
        out_specs=pl.BlockSpec((None, seq, HEAD_DIM), lambda b, h: (b, 0, h)),
        compiler_params=_cparams(("parallel", "parallel")),
        name="moba_attention",
    )(z, z, z)


def _gelu_tanh(x):
    return 0.5 * x * (1.0 + jnp.tanh(0.7978845608028654 * (x + 0.044715 * (x * x * x))))


def _sgu_kernel(u_ref, v_ref, g_ref, w_ref, bt_ref, o_ref, *, chunk, groups, gd):
    tm = u_ref.shape[0]
    u = _gelu_tanh(u_ref[...].astype(F32))
    v = _gelu_tanh(v_ref[...].astype(F32))
    mu = jnp.mean(v, axis=-1, keepdims=True)
    vc = v - mu
    var = jnp.mean(vc * vc, axis=-1, keepdims=True)
    vn = (vc * lax.rsqrt(var + NORM_EPS) * g_ref[...]).astype(BF16)
    t_i = lax.broadcasted_iota(jnp.int32, (chunk, chunk), 0)
    s_i = lax.broadcasted_iota(jnp.int32, (chunk, chunk), 1)
    lower = s_i <= t_i
    for g in range(groups):
        w = jnp.where(lower, w_ref[g], 0.0).astype(BF16)
        bias = bt_ref[:, g:g + 1]
        cols = slice(g * gd, (g + 1) * gd)
        for c in range(tm // chunk):
            rows = slice(c * chunk, (c + 1) * chunk)
            mixed = jnp.dot(w, vn[rows, cols], preferred_element_type=F32) + bias
            o_ref[rows, cols] = (u[rows, cols] * mixed).astype(o_ref.dtype)


def spatial_gating(z, g_sgu, w_s, b_s, col_u, col_v, tm=256):
    n = z.shape[0]
    width = SGU_GROUPS * SGU_GROUP_DIM
    kern = functools.partial(_sgu_kernel, chunk=SGU_CHUNK, groups=SGU_GROUPS, gd=SGU_GROUP_DIM)
    return pl.pallas_call(
        kern,
        out_shape=jax.ShapeDtypeStruct((n, width), BF16),
        grid=(n // tm,),
        in_specs=[pl.BlockSpec((tm, width), lambda i: (i, col_u)),
                  pl.BlockSpec((tm, width), lambda i: (i, col_v)),
                  pl.BlockSpec((1, width), lambda i: (0, 0)),
                  pl.BlockSpec(w_s.shape, lambda i: (0, 0, 0)),
                  pl.BlockSpec((SGU_CHUNK, SGU_GROUPS), lambda i: (0, 0))],
        out_specs=pl.BlockSpec((tm, width), lambda i: (i, 0)),
        compiler_params=_cparams(("parallel",)),
        name="spatial_gating",
    )(z, z, g_sgu.reshape(1, width), w_s, b_s.T)


def _router_kernel(x_ref, g_ref, rwt_ref, rb_ref, eidx_ref, wts_ref, rank_ref, cnt_ref):
    i = pl.program_id(0)
    ne = rwt_ref.shape[0]
    tm = x_ref.shape[0]

    @pl.when(i == 0)
    def _():
        cnt_ref[...] = jnp.zeros_like(cnt_ref)

    h = _rms(x_ref[...], g_ref[...])
    logits = lax.dot_general(rwt_ref[...], h, (((1,), (1,)), ((), ())),
                             precision=lax.Precision.HIGHEST,
                             preferred_element_type=F32) + rb_ref[...]
    row = lax.broadcasted_iota(jnp.int32, (ne, tm), 0).astype(F32)
    neg_inf = jnp.float32(-jnp.inf)
    m0 = jnp.max(logits, axis=0, keepdims=True)
    i0 = jnp.min(jnp.where(logits == m0, row, float(ne)), axis=0, keepdims=True)
    rest = jnp.where(row == i0, neg_inf, logits)
    m1 = jnp.max(rest, axis=0, keepdims=True)
    i1 = jnp.min(jnp.where(jnp.logical_and(rest == m1, row != i0), row, float(ne)),
                 axis=0, keepdims=True)
    e1 = jnp.exp(m1 - m0)
    denom = 1.0 + e1
    eidx_ref[0:1, :] = i0.astype(jnp.int32)
    eidx_ref[1:2, :] = i1.astype(jnp.int32)
    wts_ref[0:1, :] = 1.0 / denom
    wts_ref[1:2, :] = e1 / denom

    hit0 = row == i0
    hit1 = row == i1
    chosen = jnp.logical_or(hit0, hit1).astype(BF16)
    earlier = (lax.broadcasted_iota(jnp.int32, (tm, tm), 0)
               < lax.broadcasted_iota(jnp.int32, (tm, tm), 1)).astype(BF16)
    before = jnp.dot(chosen, earlier, preferred_element_type=F32) + cnt_ref[:, 0:1]
    rank_ref[0:1, :] = jnp.sum(jnp.where(hit0, before, 0.0), axis=0, keepdims=True).astype(jnp.int32)
    rank_ref[1:2, :] = jnp.sum(jnp.where(hit1, before, 0.0), axis=0, keepdims=True).astype(jnp.int32)
    cnt_ref[...] = cnt_ref[...] + jnp.sum(chosen.astype(F32), axis=1, keepdims=True)


def moe_router(x, g, router_w, router_b, tm=ROW_TILE):
    n, d = x.shape
    ne = router_w.shape[1]
    out_shapes = (jax.ShapeDtypeStruct((TOP_K, n), jnp.int32),
                  jax.ShapeDtypeStruct((TOP_K, n), F32),
                  jax.ShapeDtypeStruct((TOP_K, n), jnp.int32),
                  jax.ShapeDtypeStruct((ne, 128), F32))
    tok_spec = pl.BlockSpec((TOP_K, tm), lambda i: (0, i))
    return pl.pallas_call(
        _router_kernel,
        out_shape=out_shapes,
        grid=(n // tm,),
        in_specs=[pl.BlockSpec((tm, d), lambda i: (i, 0)),
                  pl.BlockSpec((1, d), lambda i: (0, 0)),
                  pl.BlockSpec((ne, d), lambda i: (0, 0)),
                  pl.BlockSpec((ne, 1), lambda i: (0, 0))],
        out_specs=(tok_spec, tok_spec, tok_spec, pl.BlockSpec((ne, 128), lambda i: (0, 0))),
        compiler_params=_cparams(("arbitrary",)),
        name="moe_router",
    )(x, g.reshape(1, d), router_w.T, router_b.reshape(ne, 1))


def _row_copy(src_hbm, row, buf, slot, j, sem):
    return pltpu.make_async_copy(src_hbm.at[pl.ds(row, 1), :],
                                 buf.at[slot, pl.ds(j, 1), :], sem.at[slot])


def _dispatch_kernel(src_ref, x_hbm, g_ref, o_ref, buf, sem, *, tg):
    i = pl.program_id(0)
    nsteps = pl.num_programs(0)

    def fetch(step, slot):
        def body(j, carry):
            _row_copy(x_hbm, src_ref[step * tg + j], buf, slot, j, sem).start()
            return carry
        lax.fori_loop(0, tg, body, 0)

    @pl.when(i == 0)
    def _():
        fetch(0, 0)

    @pl.when(i + 1 < nsteps)
    def _():
        fetch(i + 1, jnp.bitwise_and(i + 1, 1))

    slot = jnp.bitwise_and(i, 1)

    def wait_row(j, carry):
        _row_copy(x_hbm, 0, buf, slot, j, sem).wait()
        return carry
    lax.fori_loop(0, tg, wait_row, 0)
    o_ref[...] = _rms(buf[slot], g_ref[...]).astype(o_ref.dtype)


def moe_dispatch(x, g, src, tg=256):
    n, d = x.shape
    n_slots = src.shape[0]
    kern = functools.partial(_dispatch_kernel, tg=tg)
    return pl.pallas_call(
        kern,
        out_shape=jax.ShapeDtypeStruct((n_slots, d), BF16),
        grid_spec=pltpu.PrefetchScalarGridSpec(
            num_scalar_prefetch=1,
            grid=(n_slots // tg,),
            in_specs=[pl.BlockSpec(memory_space=pl.ANY),
                      pl.BlockSpec((1, d), lambda i, s: (0, 0))],
            out_specs=pl.BlockSpec((tg, d), lambda i, s: (i, 0)),
            scratch_shapes=[pltpu.VMEM((2, tg, d), F32), pltpu.SemaphoreType.DMA((2,))],
        ),
        compiler_params=_cparams(("arbitrary",)),
        name="moe_dispatch",
    )(src, x, g.reshape(1, d))


def _combine_kernel(slot_ref, y_hbm, x_ref, w_ref, g_ref, o_ref, buf, sem, *, tc, final_norm):
    i = pl.program_id(0)
    nsteps = pl.num_programs(0)
    n_tok = nsteps * tc
    rows = TOP_K * tc

    def fetch(step, slot):
        for k in range(TOP_K):
            def body(j, carry, k=k):
                _row_copy(y_hbm, slot_ref[k * n_tok + step * tc + j], buf, slot, k * tc + j, sem).start()
                return carry
            lax.fori_loop(0, tc, body, 0)

    @pl.when(i == 0)
    def _():
        fetch(0, 0)

    @pl.when(i + 1 < nsteps)
    def _():
        fetch(i + 1, jnp.bitwise_and(i + 1, 1))

    slot = jnp.bitwise_and(i, 1)

    def wait_row(j, carry):
        _row_copy(y_hbm, 0, buf, slot, j, sem).wait()
        return carry
    lax.fori_loop(0, rows, wait_row, 0)
    acc = x_ref[...]
    for k in range(TOP_K):
        acc = acc + w_ref[:, k:k + 1] * buf[slot, k * tc:(k + 1) * tc, :]
    if final_norm:
        acc = _rms(acc, g_ref[...])
    o_ref[...] = acc


def moe_combine(y, x, slot, wts, g, final_norm, tc=128):
    n, d = x.shape
    kern = functools.partial(_combine_kernel, tc=tc, final_norm=final_norm)
    return pl.pallas_call(
        kern,
        out_shape=jax.ShapeDtypeStruct((n, d), F32),
        grid_spec=pltpu.PrefetchScalarGridSpec(
            num_scalar_prefetch=1,
            grid=(n // tc,),
            in_specs=[pl.BlockSpec(memory_space=pl.ANY),
                      pl.BlockSpec((tc, d), lambda i, s: (i, 0)),
                      pl.BlockSpec((tc, TOP_K), lambda i, s: (i, 0)),
                      pl.BlockSpec((1, d), lambda i, s: (0, 0))],
            out_specs=pl.BlockSpec((tc, d), lambda i, s: (i, 0)),
            scratch_shapes=[pltpu.VMEM((2, TOP_K * tc, d), F32), pltpu.SemaphoreType.DMA((2,))],
        ),
        compiler_params=_cparams(("arbitrary",)),
        name="moe_combine",
    )(slot.reshape(-1), y, x, wts.T, g.reshape(1, d))


def token_mixer(x, batch, seq, layer, norm_g, w_in, g_sgu, w_s, b_s, w_pa, w_pb, w_o):
    n, d = x.shape
    aw = ATTN_HEADS * HEAD_DIM
    sw = SGU_GROUPS * SGU_GROUP_DIM
    tn = 1024
    h = rmsnorm(x, norm_g, BF16)
    z = ws_matmul([h], [w_in], [0], [], [], _ep_plain, BF16, tn, w_base=layer, name="in_proj")
    attn = moba_attention(z.reshape(batch, seq, z.shape[1]), batch, seq).reshape(n, aw)
    sgu = spatial_gating(z, g_sgu, w_s, b_s, (3 * aw) // sw, (3 * aw + sw) // sw)
    gate_col = (3 * aw + 2 * sw) // tn
    merged = ws_matmul([attn, sgu], [w_pa, w_pb], [0, 1], [z, z],
                       [gate_col, gate_col + d // tn], _ep_merge, BF16, tn, w_base=layer,
                       name="branch_merge")
    return ws_matmul([merged], [w_o], [0], [x], [0], _ep_residual, F32, tn, w_base=layer,
                     name="out_proj")


def dense_ffn(x, norm_g, j, wg, wu, wd):
    h = rmsnorm(x, norm_g, BF16)
    act = ws_matmul([h], [wg, wu], [0, 0], [], [], _ep_swiglu, BF16, 512, w_base=j, name="ffn_up")
    return ws_matmul([act], [wd], [0], [x], [0], _ep_residual, F32, 512, w_base=j, name="ffn_down")


def moe_ffn(x, norm_g, j, router_w, router_b, wg, wu, wd, final_g, final_norm):
    n, d = x.shape
    ne = router_w.shape[1]
    tm = ROW_TILE
    eidx, wts, rank, cnt = moe_router(x, norm_g, router_w, router_b)
    counts = cnt[:, 0].astype(jnp.int32)
    padded = ((counts + tm - 1) // tm) * tm
    ends = jnp.cumsum(padded)
    starts = ends - padded
    slot = starts[eidx] + rank
    n_tiles = (TOP_K * n) // tm + ne
    tok = jnp.tile(jnp.arange(n, dtype=jnp.int32), TOP_K)
    src = jnp.zeros((n_tiles * tm,), jnp.int32).at[slot.reshape(-1)].set(tok)
    tile_row = jnp.arange(n_tiles, dtype=jnp.int32) * tm
    texp = jnp.minimum(jnp.sum(tile_row[:, None] >= ends[None, :], axis=1), ne - 1).astype(jnp.int32)

    hs = moe_dispatch(x, norm_g, src)
    act = ws_matmul([hs], [wg, wu], [0, 0], [], [], _ep_swiglu, BF16, 512, texp=texp,
                    w_base=j * ne, name="moe_up")
    y = ws_matmul([act], [wd], [0], [], [], _ep_plain, F32, 512, texp=texp, w_base=j * ne,
                  name="moe_down")
    return moe_combine(y, x, slot, wts, final_g, final_norm)


def kernel(x, mix_norm_g, w_in, sgu_norm_g, w_s, b_s, w_pa, w_pb, w_o, ffn_norm_g,
           dense_w_gate, dense_w_up, dense_w_down, router_w, router_b,
           expert_w_gate, expert_w_up, expert_w_down, final_norm_g):
    batch, seq, d = x.shape
    depth = mix_norm_g.shape[0]
    xf = x.reshape(batch * seq, d)
    merge_experts = lambda w: w.reshape((w.shape[0] * w.shape[1],) + w.shape[2:])
    ewg, ewu, ewd = (merge_experts(w) for w in (expert_w_gate, expert_w_up, expert_w_down))
    normed = False
    for i in range(depth):
        xf = token_mixer(xf, batch, seq, i, mix_norm_g[i], w_in, sgu_norm_g[i], w_s[i], b_s[i],
                         w_pa, w_pb, w_o)
        j = i // 2
        last = i == depth - 1
        if i % 2 == 0:
            xf = dense_ffn(xf, ffn_norm_g[i], j, dense_w_gate, dense_w_up, dense_w_down)
        else:
            xf = moe_ffn(xf, ffn_norm_g[i], j, router_w[j], router_b[j], ewg, ewu, ewd,
                         final_norm_g, last)
            normed = last
    if not normed:
        xf = rmsnorm(xf, final_norm_g, F32)
    return xf.reshape(batch, seq, d)
```

```python
import functools

import jax
import jax.numpy as jnp
from jax import lax
from jax.experimental import pallas as pl
from jax.experimental.pallas import tpu as pltpu

F32 = jnp.float32
BF16 = jnp.bfloat16

ATTN_HEADS = 8
HEAD_DIM = 128
MOBA_BLOCK = 256
MOBA_TOPK = 3
SGU_GROUPS = 8
SGU_GROUP_DIM = 128
SGU_CHUNK = 128
TOP_K = 2
NORM_EPS = 1e-6

V7X_VMEM_BYTES = 64 * 1024 * 1024
VMEM_LIMIT_BYTES = V7X_VMEM_BYTES - 8 * 1024 * 1024

ROW_TILE = 512
CAST_ROWS = 256
MOE_SUB_ROWS = 256


def _cparams(sem):
    return pltpu.CompilerParams(dimension_semantics=sem, vmem_limit_bytes=VMEM_LIMIT_BYTES)


def _rms(x, g):
    ms = jnp.mean(x * x, axis=-1, keepdims=True)
    return x * lax.rsqrt(ms + NORM_EPS) * g


def _rmsnorm_kernel(x_ref, g_ref, o_ref):
    o_ref[...] = _rms(x_ref[...], g_ref[...]).astype(o_ref.dtype)


def rmsnorm(x, g, out_dtype, tm=ROW_TILE):
    n, d = x.shape
    return pl.pallas_call(
        _rmsnorm_kernel,
        out_shape=jax.ShapeDtypeStruct((n, d), out_dtype),
        grid=(n // tm,),
        in_specs=[pl.BlockSpec((tm, d), lambda i: (i, 0)),
                  pl.BlockSpec((1, d), lambda i: (0, 0))],
        out_specs=pl.BlockSpec((tm, d), lambda i: (i, 0)),
        compiler_params=_cparams(("parallel",)),
        name="rmsnorm",
    )(x, g.reshape(1, d))


def _ws_kernel(texp_ref, tvalid_ref, *refs, n_lhs, w_lhs, n_extra, epilogue, sub_rows):
    n_w = len(w_lhs)
    lhs_refs = refs[:n_lhs]
    w_refs = refs[n_lhs:n_lhs + n_w]
    extra_refs = refs[n_lhs + n_w:n_lhs + n_w + n_extra]
    out_ref = refs[n_lhs + n_w + n_extra]
    wb_refs = refs[n_lhs + n_w + n_extra + 1:]
    tm = out_ref.shape[0]

    r = pl.program_id(1)
    new_group = jnp.logical_or(r == 0, texp_ref[r] != texp_ref[jnp.maximum(r - 1, 0)])

    @pl.when(new_group)
    def _():
        for w_ref, wb_ref in zip(w_refs, wb_refs):
            k = w_ref.shape[0]

            def cast_rows(c, carry, w_ref=w_ref, wb_ref=wb_ref):
                rows = pl.ds(pl.multiple_of(c * CAST_ROWS, CAST_ROWS), CAST_ROWS)
                wb_ref[rows, :] = w_ref[rows, :].astype(BF16)
                return carry

            lax.fori_loop(0, k // CAST_ROWS, cast_rows, 0)

    def compute(rows):
        prods = [jnp.dot(lhs_refs[w_lhs[i]][rows, :], wb_refs[i][...], preferred_element_type=F32)
                 for i in range(n_w)]
        out_ref[rows, :] = epilogue(prods, [e[rows, :] for e in extra_refs]).astype(out_ref.dtype)

    if sub_rows is None:
        compute(slice(None))
    else:
        for c in range(tm // sub_rows):
            rows = slice(c * sub_rows, (c + 1) * sub_rows)
            live = tvalid_ref[r] > c * sub_rows
            pl.when(live)(functools.partial(compute, rows))

            @pl.when(jnp.logical_not(live))
            def _(rows=rows):
                out_ref[rows, :] = jnp.zeros((sub_rows, out_ref.shape[1]), out_ref.dtype)


def ws_matmul(lhs, ws, w_lhs, extras, extra_col_off, epilogue, out_dtype, tn, texp=None,
              tvalid=None, sub_rows=None, w_base=0, tm=ROW_TILE, name="ws_matmul"):
    m = lhs[0].shape[0]
    n_total = ws[0].shape[2]
    n_rows = m // tm
    if texp is None:
        texp = jnp.zeros((n_rows,), jnp.int32)
    if tvalid is None:
        tvalid = jnp.full((n_rows,), tm, jnp.int32)
    texp = texp + w_base
    in_specs = []
    for a in lhs:
        in_specs.append(pl.BlockSpec((tm, a.shape[1]), lambda n, r, t, v: (r, 0)))
    for w in ws:
        in_specs.append(pl.BlockSpec((None, w.shape[1], tn), lambda n, r, t, v: (t[r], 0, n)))
    for off in extra_col_off:
        in_specs.append(pl.BlockSpec((tm, tn), lambda n, r, t, v, off=off: (r, off + n)))
    kern = functools.partial(_ws_kernel, n_lhs=len(lhs), w_lhs=tuple(w_lhs),
                             n_extra=len(extras), epilogue=epilogue, sub_rows=sub_rows)
    return pl.pallas_call(
        kern,
        out_shape=jax.ShapeDtypeStruct((m, n_total), out_dtype),
        grid_spec=pltpu.PrefetchScalarGridSpec(
            num_scalar_prefetch=2,
            grid=(n_total // tn, n_rows),
            in_specs=in_specs,
            out_specs=pl.BlockSpec((tm, tn), lambda n, r, t, v: (r, n)),
            scratch_shapes=[pltpu.VMEM((w.shape[1], tn), BF16) for w in ws],
        ),
        compiler_params=_cparams(("arbitrary", "arbitrary")),
        name=name,
    )(texp, tvalid, *lhs, *ws, *extras)


def _ep_plain(prods, extras):
    return prods[0]


def _ep_residual(prods, extras):
    return extras[0] + prods[0]


def _ep_swiglu(prods, extras):
    g, u = prods
    return (g * jax.nn.sigmoid(g)) * u


def _ep_merge(prods, extras):
    ga, gb = extras
    return (jax.nn.sigmoid(ga.astype(F32)) * prods[0]
            + jax.nn.sigmoid(gb.astype(F32)) * prods[1])


def _attn_kernel(q_ref, k_ref, v_ref, o_ref, *, nb, blk, topk, scale):
    seq = k_ref.shape[0]
    neg_inf = jnp.float32(-jnp.inf)
    avg = jnp.where(lax.broadcasted_iota(jnp.int32, (nb, seq), 1) // blk
                    == lax.broadcasted_iota(jnp.int32, (nb, seq), 0), 1.0 / blk, 0.0).astype(BF16)
    k_mean = jnp.dot(avg, k_ref[...], preferred_element_type=F32)
    blk_id = lax.broadcasted_iota(jnp.int32, (blk, nb), 1)

    for n in range(nb):
        qn = q_ref[n * blk:(n + 1) * blk, :]
        nk = (n + 1) * blk
        s = lax.dot_general(qn, k_ref[0:nk, :], (((1,), (1,)), ((), ())),
                            preferred_element_type=F32) * scale
        k_pos = lax.broadcasted_iota(jnp.int32, (blk, nk), 1)
        q_pos = lax.broadcasted_iota(jnp.int32, (blk, nk), 0) + n * blk
        mask = k_pos <= q_pos
        if n > topk:
            gate = lax.dot_general(qn.astype(F32), k_mean, (((1,), (1,)), ((), ())),
                                   precision=lax.Precision.HIGHEST,
                                   preferred_element_type=F32)
            beaten = jnp.zeros((blk, nb), F32)
            for j in range(n):
                gj = gate[:, j:j + 1]
                wins = jnp.logical_or(gj > gate, jnp.logical_and(gj == gate, j < blk_id))
                beaten = beaten + wins.astype(F32)
            sel = jnp.where(beaten < topk, 1.0, 0.0)
            picked = jnp.concatenate(
                [jnp.broadcast_to(sel[:, j:j + 1], (blk, blk)) for j in range(n)]
                + [jnp.ones((blk, blk), F32)], axis=1)
            mask = jnp.logical_and(mask, picked > 0.5)
        s = jnp.where(mask, s, neg_inf)
        m = jnp.max(s, axis=-1, keepdims=True)
        p = jnp.exp(s - m)
        l = jnp.sum(p, axis=-1, keepdims=True)
        o = jnp.dot(p.astype(BF16), v_ref[0:nk, :], preferred_element_type=F32)
        o_ref[n * blk:(n + 1) * blk, :] = (o / l).astype(o_ref.dtype)


def moba_attention(z, batch, seq):
    nb = seq // MOBA_BLOCK
    kern = functools.partial(_attn_kernel, nb=nb, blk=MOBA_BLOCK, topk=MOBA_TOPK,
                             scale=HEAD_DIM ** -0.5)
    blockspec = lambda off: pl.BlockSpec((None, seq, HEAD_DIM), lambda b, h: (b, 0, off + h))
    return pl.pallas_call(
        kern,
        out_shape=jax.ShapeDtypeStruct((batch, seq, ATTN_HEADS * HEAD_DIM), BF16),
        grid=(batch, ATTN_HEADS),
        in_specs=[blockspec(0), blockspec(ATTN_HEADS), blockspec(2 * ATTN_HEADS)],
        out_specs=pl.BlockSpec((None, seq, HEAD_DIM), lambda b, h: (b, 0, h)),
        compiler_params=_cparams(("parallel", "parallel")),
        name="moba_attention",
    )(z, z, z)


def _gelu_tanh(x):
    return 0.5 * x * (1.0 + jnp.tanh(0.7978845608028654 * (x + 0.044715 * (x * x * x))))


def _sgu_kernel(u_ref, v_ref, g_ref, w_ref, bt_ref, o_ref, *, chunk, groups, gd):
    tm = u_ref.shape[0]
    u = _gelu_tanh(u_ref[...].astype(F32))
    v = _gelu_tanh(v_ref[...].astype(F32))
    mu = jnp.mean(v, axis=-1, keepdims=True)
    vc = v - mu
    var = jnp.mean(vc * vc, axis=-1, keepdims=True)
    vn = (vc * lax.rsqrt(var + NORM_EPS) * g_ref[...]).astype(BF16)
    t_i = lax.broadcasted_iota(jnp.int32, (chunk, chunk), 0)
    s_i = lax.broadcasted_iota(jnp.int32, (chunk, chunk), 1)
    lower = s_i <= t_i
    for g in range(groups):
        w = jnp.where(lower, w_ref[g], 0.0).astype(BF16)
        bias = bt_ref[:, g:g + 1]
        cols = slice(g * gd, (g + 1) * gd)
        for c in range(tm // chunk):
            rows = slice(c * chunk, (c + 1) * chunk)
            mixed = jnp.dot(w, vn[rows, cols], preferred_element_type=F32) + bias
            o_ref[rows, cols] = (u[rows, cols] * mixed).astype(o_ref.dtype)


def spatial_gating(z, g_sgu, w_s, b_s, col_u, col_v, tm=256):
    n = z.shape[0]
    width = SGU_GROUPS * SGU_GROUP_DIM
    kern = functools.partial(_sgu_kernel, chunk=SGU_CHUNK, groups=SGU_GROUPS, gd=SGU_GROUP_DIM)
    return pl.pallas_call(
        kern,
        out_shape=jax.ShapeDtypeStruct((n, width), BF16),
        grid=(n // tm,),
        in_specs=[pl.BlockSpec((tm, width), lambda i: (i, col_u)),
                  pl.BlockSpec((tm, width), lambda i: (i, col_v)),
                  pl.BlockSpec((1, width), lambda i: (0, 0)),
                  pl.BlockSpec(w_s.shape, lambda i: (0, 0, 0)),
                  pl.BlockSpec((SGU_CHUNK, SGU_GROUPS), lambda i: (0, 0))],
        out_specs=pl.BlockSpec((tm, width), lambda i: (i, 0)),
        compiler_params=_cparams(("parallel",)),
        name="spatial_gating",
    )(z, z, g_sgu.reshape(1, width), w_s, b_s.T)


def _router_kernel(x_ref, g_ref, rwt_ref, rb_ref, eidx_ref, wts_ref, rank_ref, cnt_ref):
    i = pl.program_id(0)
    ne = rwt_ref.shape[0]
    tm = x_ref.shape[0]

    @pl.when(i == 0)
    def _():
        cnt_ref[...] = jnp.zeros_like(cnt_ref)

    h = _rms(x_ref[...], g_ref[...])
    logits = lax.dot_general(rwt_ref[...], h, (((1,), (1,)), ((), ())),
                             precision=lax.Precision.HIGHEST,
                             preferred_element_type=F32) + rb_ref[...]
    row = lax.broadcasted_iota(jnp.int32, (ne, tm), 0).astype(F32)
    neg_inf = jnp.float32(-jnp.inf)
    m0 = jnp.max(logits, axis=0, keepdims=True)
    i0 = jnp.min(jnp.where(logits == m0, row, float(ne)), axis=0, keepdims=True)
    rest = jnp.where(row == i0, neg_inf, logits)
    m1 = jnp.max(rest, axis=0, keepdims=True)
    i1 = jnp.min(jnp.where(jnp.logical_and(rest == m1, row != i0), row, float(ne)),
                 axis=0, keepdims=True)
    e1 = jnp.exp(m1 - m0)
    denom = 1.0 + e1
    eidx_ref[0:1, :] = i0.astype(jnp.int32)
    eidx_ref[1:2, :] = i1.astype(jnp.int32)
    wts_ref[0:1, :] = 1.0 / denom
    wts_ref[1:2, :] = e1 / denom

    hit0 = row == i0
    hit1 = row == i1
    chosen = jnp.logical_or(hit0, hit1).astype(BF16)
    earlier = (lax.broadcasted_iota(jnp.int32, (tm, tm), 0)
               < lax.broadcasted_iota(jnp.int32, (tm, tm), 1)).astype(BF16)
    before = jnp.dot(chosen, earlier, preferred_element_type=F32) + cnt_ref[:, 0:1]
    rank_ref[0:1, :] = jnp.sum(jnp.where(hit0, before, 0.0), axis=0, keepdims=True).astype(jnp.int32)
    rank_ref[1:2, :] = jnp.sum(jnp.where(hit1, before, 0.0), axis=0, keepdims=True).astype(jnp.int32)
    cnt_ref[...] = cnt_ref[...] + jnp.sum(chosen.astype(F32), axis=1, keepdims=True)


def moe_router(x, g, router_w, router_b, tm=ROW_TILE):
    n, d = x.shape
    ne = router_w.shape[1]
    out_shapes = (jax.ShapeDtypeStruct((TOP_K, n), jnp.int32),
                  jax.ShapeDtypeStruct((TOP_K, n), F32),
                  jax.ShapeDtypeStruct((TOP_K, n), jnp.int32),
                  jax.ShapeDtypeStruct((ne, 128), F32))
    tok_spec = pl.BlockSpec((TOP_K, tm), lambda i: (0, i))
    return pl.pallas_call(
        _router_kernel,
        out_shape=out_shapes,
        grid=(n // tm,),
        in_specs=[pl.BlockSpec((tm, d), lambda i: (i, 0)),
                  pl.BlockSpec((1, d), lambda i: (0, 0)),
                  pl.BlockSpec((ne, d), lambda i: (0, 0)),
                  pl.BlockSpec((ne, 1), lambda i: (0, 0))],
        out_specs=(tok_spec, tok_spec, tok_spec, pl.BlockSpec((ne, 128), lambda i: (0, 0))),
        compiler_params=_cparams(("arbitrary",)),
        name="moe_router",
    )(x, g.reshape(1, d), router_w.T, router_b.reshape(ne, 1))


ISSUE_UNROLL = 8


def _start_row_gather(src_hbm, idx_ref, idx_base, n_rows, buf, slot, row_base, sem):
    def body(j, carry):
        row = idx_ref[idx_base + j]
        pltpu.make_async_copy(src_hbm.at[pl.ds(row, 1), :],
                              buf.at[slot, pl.ds(row_base + j, 1), :], sem.at[slot]).start()
        return carry
    lax.fori_loop(0, n_rows, body, 0, unroll=ISSUE_UNROLL)


def _wait_row_gather(src_hbm, buf, slot, sem):
    n_rows = buf.shape[1]
    pltpu.make_async_copy(src_hbm.at[pl.ds(0, n_rows), :], buf.at[slot], sem.at[slot]).wait()


def _dispatch_kernel(src_ref, x_hbm, g_ref, o_ref, buf, sem, *, tg):
    i = pl.program_id(0)
    nsteps = pl.num_programs(0)

    @pl.when(i == 0)
    def _():
        _start_row_gather(x_hbm, src_ref, 0, tg, buf, 0, 0, sem)

    @pl.when(i + 1 < nsteps)
    def _():
        _start_row_gather(x_hbm, src_ref, (i + 1) * tg, tg, buf, jnp.bitwise_and(i + 1, 1), 0, sem)

    slot = jnp.bitwise_and(i, 1)
    _wait_row_gather(x_hbm, buf, slot, sem)
    o_ref[...] = _rms(buf[slot], g_ref[...]).astype(o_ref.dtype)


def moe_dispatch(x, g, src, tg=256):
    n, d = x.shape
    n_slots = src.shape[0]
    kern = functools.partial(_dispatch_kernel, tg=tg)
    return pl.pallas_call(
        kern,
        out_shape=jax.ShapeDtypeStruct((n_slots, d), BF16),
        grid_spec=pltpu.PrefetchScalarGridSpec(
            num_scalar_prefetch=1,
            grid=(n_slots // tg,),
            in_specs=[pl.BlockSpec(memory_space=pl.ANY),
                      pl.BlockSpec((1, d), lambda i, s: (0, 0))],
            out_specs=pl.BlockSpec((tg, d), lambda i, s: (i, 0)),
            scratch_shapes=[pltpu.VMEM((2, tg, d), F32), pltpu.SemaphoreType.DMA((2,))],
        ),
        compiler_params=_cparams(("arbitrary",)),
        name="moe_dispatch",
    )(src, x, g.reshape(1, d))


def _combine_kernel(slot_ref, y_hbm, x_ref, w_ref, g_ref, o_ref, buf, sem, *, tc, final_norm):
    i = pl.program_id(0)
    nsteps = pl.num_programs(0)
    n_tok = nsteps * tc

    def fetch(step, slot):
        for k in range(TOP_K):
            _start_row_gather(y_hbm, slot_ref, k * n_tok + step * tc, tc, buf, slot, k * tc, sem)

    @pl.when(i == 0)
    def _():
        fetch(0, 0)

    @pl.when(i + 1 < nsteps)
    def _():
        fetch(i + 1, jnp.bitwise_and(i + 1, 1))

    slot = jnp.bitwise_and(i, 1)
    _wait_row_gather(y_hbm, buf, slot, sem)
    acc = x_ref[...]
    for k in range(TOP_K):
        acc = acc + w_ref[:, k:k + 1] * buf[slot, k * tc:(k + 1) * tc, :]
    if final_norm:
        acc = _rms(acc, g_ref[...])
    o_ref[...] = acc


def moe_combine(y, x, slot, wts, g, final_norm, tc=128):
    n, d = x.shape
    kern = functools.partial(_combine_kernel, tc=tc, final_norm=final_norm)
    return pl.pallas_call(
        kern,
        out_shape=jax.ShapeDtypeStruct((n, d), F32),
        grid_spec=pltpu.PrefetchScalarGridSpec(
            num_scalar_prefetch=1,
            grid=(n // tc,),
            in_specs=[pl.BlockSpec(memory_space=pl.ANY),
                      pl.BlockSpec((tc, d), lambda i, s: (i, 0)),
                      pl.BlockSpec((tc, TOP_K), lambda i, s: (i, 0)),
                      pl.BlockSpec((1, d), lambda i, s: (0, 0))],
            out_specs=pl.BlockSpec((tc, d), lambda i, s: (i, 0)),
            scratch_shapes=[pltpu.VMEM((2, TOP_K * tc, d), F32), pltpu.SemaphoreType.DMA((2,))],
        ),
        compiler_params=_cparams(("arbitrary",)),
        name="moe_combine",
    )(slot.reshape(-1), y, x, wts.T, g.reshape(1, d))


def token_mixer(x, batch, seq, layer, norm_g, w_in, g_sgu, w_s, b_s, w_pa, w_pb, w_o):
    n, d = x.shape
    aw = ATTN_HEADS * HEAD_DIM
    sw = SGU_GROUPS * SGU_GROUP_DIM
    tn = 1024
    h = rmsnorm(x, norm_g, BF16)
    z = ws_matmul([h], [w_in], [0], [], [], _ep_plain, BF16, tn, w_base=layer, name="in_proj")
    attn = moba_attention(z.reshape(batch, seq, z.shape[1]), batch, seq).reshape(n, aw)
    sgu = spatial_gating(z, g_sgu, w_s, b_s, (3 * aw) // sw, (3 * aw + sw) // sw)
    gate_col = (3 * aw + 2 * sw) // tn
    merged = ws_matmul([attn, sgu], [w_pa, w_pb], [0, 1], [z, z],
                       [gate_col, gate_col + d // tn], _ep_merge, BF16, tn, w_base=layer,
                       name="branch_merge")
    return ws_matmul([merged], [w_o], [0], [x], [0], _ep_residual, F32, tn, w_base=layer,
                     name="out_proj")


def dense_ffn(x, norm_g, j, wg, wu, wd):
    h = rmsnorm(x, norm_g, BF16)
    act = ws_matmul([h], [wg, wu], [0, 0], [], [], _ep_swiglu, BF16, 512, w_base=j, name="ffn_up")
    return ws_matmul([act], [wd], [0], [x], [0], _ep_residual, F32, 512, w_base=j, name="ffn_down")


def moe_ffn(x, norm_g, j, router_w, router_b, wg, wu, wd, final_g, final_norm):
    n, d = x.shape
    ne = router_w.shape[1]
    tm = ROW_TILE
    eidx, wts, rank, cnt = moe_router(x, norm_g, router_w, router_b)
    counts = cnt[:, 0].astype(jnp.int32)
    padded = ((counts + tm - 1) // tm) * tm
    ends = jnp.cumsum(padded)
    starts = ends - padded
    group_start = jnp.sum(jnp.where(eidx[..., None] == jnp.arange(ne), starts, 0), axis=-1)
    slot = group_start + rank
    n_tiles = (TOP_K * n) // tm + ne
    tok = jnp.tile(jnp.arange(n, dtype=jnp.int32), TOP_K)
    src = jnp.zeros((n_tiles * tm,), jnp.int32).at[slot.reshape(-1)].set(tok)
    tile_row = jnp.arange(n_tiles, dtype=jnp.int32) * tm
    last_used = jnp.max(jnp.where(counts > 0, jnp.arange(ne, dtype=jnp.int32), 0))
    texp = jnp.minimum(jnp.sum(tile_row[:, None] >= ends[None, :], axis=1), last_used).astype(jnp.int32)
    group_end = jnp.sum(jnp.where(texp[:, None] == jnp.arange(ne), starts + counts, 0), axis=-1)
    tvalid = jnp.clip(group_end - tile_row, 0, tm).astype(jnp.int32)

    hs = moe_dispatch(x, norm_g, src)
    act = ws_matmul([hs], [wg, wu], [0, 0], [], [], _ep_swiglu, BF16, 512, texp=texp,
                    tvalid=tvalid, sub_rows=MOE_SUB_ROWS, w_base=j * ne, name="moe_up")
    y = ws_matmul([act], [wd], [0], [], [], _ep_plain, F32, 512, texp=texp, tvalid=tvalid,
                  sub_rows=MOE_SUB_ROWS, w_base=j * ne, name="moe_down")
    return moe_combine(y, x, slot, wts, final_g, final_norm)


def kernel(x, mix_norm_g, w_in, sgu_norm_g, w_s, b_s, w_pa, w_pb, w_o, ffn_norm_g,
           dense_w_gate, dense_w_up, dense_w_down, router_w, router_b,
           expert_w_gate, expert_w_up, expert_w_down, final_norm_g):
    batch, seq, d = x.shape
    depth = mix_norm_g.shape[0]
    xf = x.reshape(batch * seq, d)
    merge_experts = lambda w: w.reshape((w.shape[0] * w.shape[1],) + w.shape[2:])
    ewg, ewu, ewd = (merge_experts(w) for w in (expert_w_gate, expert_w_up, expert_w_down))
    normed = False
    for i in range(depth):
        xf = token_mixer(xf, batch, seq, i, mix_norm_g[i], w_in, sgu_norm_g[i], w_s[i], b_s[i],
                         w_pa, w_pb, w_o)
        j = i // 2
        last = i == depth - 1
        if i % 2 == 0:
            xf = dense_ffn(xf, ffn_norm_g[i], j, dense_w_gate, dense_w_up, dense_w_down)
        else:
            xf = moe_ffn(xf, ffn_norm_g[i], j, router_w[j], router_b[j], ewg, ewu, ewd,
                         final_norm_g, last)
            normed = last
    if not normed:
        xf = rmsnorm(xf, final_norm_g, F32)
    return xf.reshape(batch, seq, d)
```

```python
import functools

import jax
import jax.numpy as jnp
from jax import lax
from jax.experimental import pallas as pl
from jax.experimental.pallas import tpu as pltpu

F32 = jnp.float32
BF16 = jnp.bfloat16

ATTN_HEADS = 8
HEAD_DIM = 128
MOBA_BLOCK = 256
MOBA_TOPK = 3
SGU_GROUPS = 8
SGU_GROUP_DIM = 128
SGU_CHUNK = 128
TOP_K = 2
NORM_EPS = 1e-6

V7X_VMEM_BYTES = 64 * 1024 * 1024
VMEM_LIMIT_BYTES = V7X_VMEM_BYTES - 8 * 1024 * 1024

ROW_TILE = 512
DENSE_ROW_TILE = 1024
CAST_ROWS = 256
MOE_SUB_ROWS = ROW_TILE


def _cparams(sem):
    return pltpu.CompilerParams(dimension_semantics=sem, vmem_limit_bytes=VMEM_LIMIT_BYTES)


def _rms(x, g):
    ms = jnp.mean(x * x, axis=-1, keepdims=True)
    return x * lax.rsqrt(ms + NORM_EPS) * g


def _rmsnorm_kernel(x_ref, g_ref, o_ref):
    o_ref[...] = _rms(x_ref[...], g_ref[...]).astype(o_ref.dtype)


def rmsnorm(x, g, out_dtype, tm=ROW_TILE):
    n, d = x.shape
    return pl.pallas_call(
        _rmsnorm_kernel,
        out_shape=jax.ShapeDtypeStruct((n, d), out_dtype),
        grid=(n // tm,),
        in_specs=[pl.BlockSpec((tm, d), lambda i: (i, 0)),
                  pl.BlockSpec((1, d), lambda i: (0, 0))],
        out_specs=pl.BlockSpec((tm, d), lambda i: (i, 0)),
        compiler_params=_cparams(("parallel",)),
        name="rmsnorm",
    )(x, g.reshape(1, d))


def _ws_kernel(texp_ref, tnext_ref, tvalid_ref, *refs, n_lhs, w_lhs, n_extra, epilogue,
               sub_rows, tn, scaled):
    n_w = len(w_lhs)
    lhs_refs = refs[:n_lhs]
    w_hbm = refs[n_lhs:n_lhs + n_w]
    extra_refs = refs[n_lhs + n_w:n_lhs + n_w + n_extra]
    n_in = n_lhs + n_w + n_extra + int(scaled)
    scale_ref = refs[n_in - 1] if scaled else None
    out_ref = refs[n_in]
    scratch = refs[n_in + 1:]
    wf_refs = scratch[:n_w]
    wb_refs = scratch[n_w:2 * n_w]
    sem, groups_seen = scratch[2 * n_w:]
    tm = out_ref.shape[0]

    n = pl.program_id(0)
    r = pl.program_id(1)
    n_col_tiles = pl.num_programs(0)

    def weight_copy(i, index, col_tile, slot):
        cols = pl.ds(pl.multiple_of(col_tile * tn, tn), tn)
        return pltpu.make_async_copy(w_hbm[i].at[index, :, cols], wf_refs[i].at[slot], sem.at[i, slot])

    def start_weights(index, col_tile, slot):
        for i in range(n_w):
            weight_copy(i, index, col_tile, slot).start()

    very_first = jnp.logical_and(n == 0, r == 0)

    @pl.when(very_first)
    def _():
        groups_seen[0] = 0
        start_weights(texp_ref[0], 0, 0)

    new_group = jnp.logical_or(r == 0, texp_ref[r] != texp_ref[jnp.maximum(r - 1, 0)])

    @pl.when(new_group)
    def _():
        seen = groups_seen[0]
        slot = jnp.bitwise_and(seen, 1)
        groups_seen[0] = seen + 1
        for i in range(n_w):
            weight_copy(i, texp_ref[r], n, slot).wait()
        following = tnext_ref[r]

        @pl.when(following >= 0)
        def _():
            start_weights(following, n, 1 - slot)

        @pl.when(jnp.logical_and(following < 0, n + 1 < n_col_tiles))
        def _():
            start_weights(texp_ref[0], n + 1, 1 - slot)

        for wf_ref, wb_ref in zip(wf_refs, wb_refs):
            k = wb_ref.shape[0]

            def cast_rows(c, carry, wf_ref=wf_ref, wb_ref=wb_ref):
                rows = pl.ds(pl.multiple_of(c * CAST_ROWS, CAST_ROWS), CAST_ROWS)
                w = wf_ref[slot, rows, :]
                if scaled:
                    w = w * scale_ref[...]
                wb_ref[rows, :] = w.astype(BF16)
                return carry

            lax.fori_loop(0, k // CAST_ROWS, cast_rows, 0)

    def compute(rows):
        prods = [jnp.dot(lhs_refs[w_lhs[i]][rows, :], wb_refs[i][...], preferred_element_type=F32)
                 for i in range(n_w)]
        out_ref[rows, :] = epilogue(prods, [e[rows, :] for e in extra_refs]).astype(out_ref.dtype)

    if sub_rows is None:
        compute(slice(None))
    else:
        for c in range(tm // sub_rows):
            rows = slice(c * sub_rows, (c + 1) * sub_rows)
            live = tvalid_ref[r] > c * sub_rows
            pl.when(live)(functools.partial(compute, rows))

            @pl.when(jnp.logical_not(live))
            def _(rows=rows):
                out_ref[rows, :] = jnp.zeros((sub_rows, out_ref.shape[1]), out_ref.dtype)


def ws_matmul(lhs, ws, w_lhs, extras, extra_col_off, epilogue, out_dtype, tn, texp=None,
              tvalid=None, sub_rows=None, w_base=0, col_scale=None, tm=ROW_TILE,
              name="ws_matmul"):
    m = lhs[0].shape[0]
    n_total = ws[0].shape[2]
    n_rows = m // tm
    if texp is None:
        texp = jnp.zeros((n_rows,), jnp.int32)
    if tvalid is None:
        tvalid = jnp.full((n_rows,), tm, jnp.int32)
    no_next = jnp.iinfo(jnp.int32).max
    later = jnp.min(jnp.where(texp[None, :] > texp[:, None], texp[None, :], no_next), axis=1)
    tnext = jnp.where(later == no_next, -1, later + w_base).astype(jnp.int32)
    texp = texp + w_base
    in_specs = []
    for a in lhs:
        in_specs.append(pl.BlockSpec((tm, a.shape[1]), lambda n, r, *_: (r, 0)))
    for w in ws:
        in_specs.append(pl.BlockSpec(memory_space=pl.ANY))
    for off in extra_col_off:
        in_specs.append(pl.BlockSpec((tm, tn), lambda n, r, *_, off=off: (r, off + n)))
    scale_args = []
    if col_scale is not None:
        in_specs.append(pl.BlockSpec((1, tn), lambda n, r, *_: (0, n)))
        scale_args.append(col_scale.reshape(1, n_total))
    kern = functools.partial(_ws_kernel, n_lhs=len(lhs), w_lhs=tuple(w_lhs),
                             n_extra=len(extras), epilogue=epilogue, sub_rows=sub_rows, tn=tn,
                             scaled=col_scale is not None)
    scratch = ([pltpu.VMEM((2, w.shape[1], tn), F32) for w in ws]
               + [pltpu.VMEM((w.shape[1], tn), BF16) for w in ws]
               + [pltpu.SemaphoreType.DMA((len(ws), 2)), pltpu.SMEM((1,), jnp.int32)])
    return pl.pallas_call(
        kern,
        out_shape=jax.ShapeDtypeStruct((m, n_total), out_dtype),
        grid_spec=pltpu.PrefetchScalarGridSpec(
            num_scalar_prefetch=3,
            grid=(n_total // tn, n_rows),
            in_specs=in_specs,
            out_specs=pl.BlockSpec((tm, tn), lambda n, r, *_: (r, n)),
            scratch_shapes=scratch,
        ),
        compiler_params=_cparams(("arbitrary", "arbitrary")),
        name=name,
    )(texp, tnext, tvalid, *lhs, *ws, *extras, *scale_args)


def _ep_plain(prods, extras):
    return prods[0]


def _ep_residual(prods, extras):
    return extras[0] + prods[0]


def _ep_swiglu(prods, extras):
    g, u = prods
    return (g * jax.nn.sigmoid(g)) * u


def _ep_merge(prods, extras):
    ga, gb = extras
    return (jax.nn.sigmoid(ga.astype(F32)) * prods[0]
            + jax.nn.sigmoid(gb.astype(F32)) * prods[1])


def _attn_kernel(q_ref, k_ref, v_ref, o_ref, *, nb, blk, topk):
    seq = k_ref.shape[0]
    neg_inf = jnp.float32(-jnp.inf)
    avg = jnp.where(lax.broadcasted_iota(jnp.int32, (nb, seq), 1) // blk
                    == lax.broadcasted_iota(jnp.int32, (nb, seq), 0), 1.0 / blk, 0.0).astype(BF16)
    k_mean = jnp.dot(avg, k_ref[...], preferred_element_type=F32)
    k_mean_hi = k_mean.astype(BF16)
    k_mean_lo = (k_mean - k_mean_hi.astype(F32)).astype(BF16)
    blk_id = lax.broadcasted_iota(jnp.int32, (blk, nb), 1)
    causal = (lax.broadcasted_iota(jnp.int32, (blk, blk), 1)
              <= lax.broadcasted_iota(jnp.int32, (blk, blk), 0))
    nt_dims = (((1,), (1,)), ((), ()))

    for n in range(nb):
        qn = q_ref[n * blk:(n + 1) * blk, :]
        nk = (n + 1) * blk
        s = lax.dot_general(qn, k_ref[0:nk, :], nt_dims, preferred_element_type=F32)
        past = [s[:, j * blk:(j + 1) * blk] for j in range(n)]
        if n > topk:
            gate = (lax.dot_general(qn, k_mean_hi, nt_dims, preferred_element_type=F32)
                    + lax.dot_general(qn, k_mean_lo, nt_dims, preferred_element_type=F32))
            beaten = jnp.zeros((blk, nb), F32)
            for j in range(n):
                gj = gate[:, j:j + 1]
                wins = jnp.logical_or(gj > gate, jnp.logical_and(gj == gate, j < blk_id))
                beaten = beaten + wins.astype(F32)
            bias = jnp.where(beaten < topk, 0.0, neg_inf)
            past = [past[j] + bias[:, j:j + 1] for j in range(n)]
        own = jnp.where(causal, s[:, n * blk:nk], neg_inf)
        s = jnp.concatenate(past + [own], axis=1)
        m = jnp.max(s, axis=-1, keepdims=True)
        p = jnp.exp(s - m)
        l = jnp.sum(p, axis=-1, keepdims=True)
        o = jnp.dot(p.astype(BF16), v_ref[0:nk, :], preferred_element_type=F32)
        o_ref[n * blk:(n + 1) * blk, :] = (o / l).astype(o_ref.dtype)


def moba_attention(z, batch, seq):
    nb = seq // MOBA_BLOCK
    kern = functools.partial(_attn_kernel, nb=nb, blk=MOBA_BLOCK, topk=MOBA_TOPK)
    blockspec = lambda off: pl.BlockSpec((None, seq, HEAD_DIM), lambda b, h: (b, 0, off + h))
    return pl.pallas_call(
        kern,
        out_shape=jax.ShapeDtypeStruct((batch, seq, ATTN_HEADS * HEAD_DIM), BF16),
        grid=(batch, ATTN_HEADS),
        in_specs=[blockspec(0), blockspec(ATTN_HEADS), blockspec(2 * ATTN_HEADS)],
        out_specs=pl.BlockSpec((None, seq, HEAD_DIM), lambda b, h: (b, 0, h)),
        compiler_params=_cparams(("parallel", "parallel")),
        name="moba_attention",
    )(z, z, z)


def _gelu_tanh(x):
    return 0.5 * x * (1.0 + jnp.tanh(0.7978845608028654 * (x + 0.044715 * (x * x * x))))


def _sgu_kernel(u_ref, v_ref, g_ref, w_ref, bt_ref, o_ref, *, chunk, groups, gd):
    tm = u_ref.shape[0]
    u = _gelu_tanh(u_ref[...].astype(F32))
    v = _gelu_tanh(v_ref[...].astype(F32))
    mu = jnp.mean(v, axis=-1, keepdims=True)
    vc = v - mu
    var = jnp.mean(vc * vc, axis=-1, keepdims=True)
    vn = (vc * lax.rsqrt(var + NORM_EPS) * g_ref[...]).astype(BF16)
    t_i = lax.broadcasted_iota(jnp.int32, (chunk, chunk), 0)
    s_i = lax.broadcasted_iota(jnp.int32, (chunk, chunk), 1)
    lower = s_i <= t_i
    for g in range(groups):
        w = jnp.where(lower, w_ref[g], 0.0).astype(BF16)
        bias = bt_ref[:, g:g + 1]
        cols = slice(g * gd, (g + 1) * gd)
        for c in range(tm // chunk):
            rows = slice(c * chunk, (c + 1) * chunk)
            mixed = jnp.dot(w, vn[rows, cols], preferred_element_type=F32) + bias
            o_ref[rows, cols] = (u[rows, cols] * mixed).astype(o_ref.dtype)


def spatial_gating(z, g_sgu, w_s, b_s, col_u, col_v, tm=256):
    n = z.shape[0]
    width = SGU_GROUPS * SGU_GROUP_DIM
    kern = functools.partial(_sgu_kernel, chunk=SGU_CHUNK, groups=SGU_GROUPS, gd=SGU_GROUP_DIM)
    return pl.pallas_call(
        kern,
        out_shape=jax.ShapeDtypeStruct((n, width), BF16),
        grid=(n // tm,),
        in_specs=[pl.BlockSpec((tm, width), lambda i: (i, col_u)),
                  pl.BlockSpec((tm, width), lambda i: (i, col_v)),
                  pl.BlockSpec((1, width), lambda i: (0, 0)),
                  pl.BlockSpec(w_s.shape, lambda i: (0, 0, 0)),
                  pl.BlockSpec((SGU_CHUNK, SGU_GROUPS), lambda i: (0, 0))],
        out_specs=pl.BlockSpec((tm, width), lambda i: (i, 0)),
        compiler_params=_cparams(("parallel",)),
        name="spatial_gating",
    )(z, z, g_sgu.reshape(1, width), w_s, b_s.T)


def _router_kernel(x_ref, g_ref, rwt_ref, rb_ref, eidx_ref, wts_ref, rank_ref, cnt_ref):
    i = pl.program_id(0)
    ne = rwt_ref.shape[0]
    tm = x_ref.shape[0]

    @pl.when(i == 0)
    def _():
        cnt_ref[...] = jnp.zeros_like(cnt_ref)

    h = _rms(x_ref[...], g_ref[...])
    logits = lax.dot_general(rwt_ref[...], h, (((1,), (1,)), ((), ())),
                             precision=lax.Precision.HIGHEST,
                             preferred_element_type=F32) + rb_ref[...]
    row = lax.broadcasted_iota(jnp.int32, (ne, tm), 0).astype(F32)
    neg_inf = jnp.float32(-jnp.inf)
    m0 = jnp.max(logits, axis=0, keepdims=True)
    i0 = jnp.min(jnp.where(logits == m0, row, float(ne)), axis=0, keepdims=True)
    rest = jnp.where(row == i0, neg_inf, logits)
    m1 = jnp.max(rest, axis=0, keepdims=True)
    i1 = jnp.min(jnp.where(jnp.logical_and(rest == m1, row != i0), row, float(ne)),
                 axis=0, keepdims=True)
    e1 = jnp.exp(m1 - m0)
    denom = 1.0 + e1
    eidx_ref[0:1, :] = i0.astype(jnp.int32)
    eidx_ref[1:2, :] = i1.astype(jnp.int32)
    wts_ref[0:1, :] = 1.0 / denom
    wts_ref[1:2, :] = e1 / denom

    hit0 = row == i0
    hit1 = row == i1
    chosen = jnp.logical_or(hit0, hit1).astype(BF16)
    earlier = (lax.broadcasted_iota(jnp.int32, (tm, tm), 0)
               < lax.broadcasted_iota(jnp.int32, (tm, tm), 1)).astype(BF16)
    before = jnp.dot(chosen, earlier, preferred_element_type=F32) + cnt_ref[:, 0:1]
    rank_ref[0:1, :] = jnp.sum(jnp.where(hit0, before, 0.0), axis=0, keepdims=True).astype(jnp.int32)
    rank_ref[1:2, :] = jnp.sum(jnp.where(hit1, before, 0.0), axis=0, keepdims=True).astype(jnp.int32)
    cnt_ref[...] = cnt_ref[...] + jnp.sum(chosen.astype(F32), axis=1, keepdims=True)


def moe_router(x, g, router_w, router_b, tm=ROW_TILE):
    n, d = x.shape
    ne = router_w.shape[1]
    out_shapes = (jax.ShapeDtypeStruct((TOP_K, n), jnp.int32),
                  jax.ShapeDtypeStruct((TOP_K, n), F32),
                  jax.ShapeDtypeStruct((TOP_K, n), jnp.int32),
                  jax.ShapeDtypeStruct((ne, 128), F32))
    tok_spec = pl.BlockSpec((TOP_K, tm), lambda i: (0, i))
    return pl.pallas_call(
        _router_kernel,
        out_shape=out_shapes,
        grid=(n // tm,),
        in_specs=[pl.BlockSpec((tm, d), lambda i: (i, 0)),
                  pl.BlockSpec((1, d), lambda i: (0, 0)),
                  pl.BlockSpec((ne, d), lambda i: (0, 0)),
                  pl.BlockSpec((ne, 1), lambda i: (0, 0))],
        out_specs=(tok_spec, tok_spec, tok_spec, pl.BlockSpec((ne, 128), lambda i: (0, 0))),
        compiler_params=_cparams(("arbitrary",)),
        name="moe_router",
    )(x, g.reshape(1, d), router_w.T, router_b.reshape(ne, 1))


ISSUE_UNROLL = 8


def _start_row_gather(src_hbm, idx_ref, idx_base, n_rows, buf, slot, row_base, sem):
    def body(j, carry):
        row = idx_ref[idx_base + j]
        pltpu.make_async_copy(src_hbm.at[pl.ds(row, 1), :],
                              buf.at[slot, pl.ds(row_base + j, 1), :], sem.at[slot]).start()
        return carry
    lax.fori_loop(0, n_rows, body, 0, unroll=ISSUE_UNROLL)


def _wait_row_gather(src_hbm, buf, slot, sem):
    n_rows = buf.shape[1]
    pltpu.make_async_copy(src_hbm.at[pl.ds(0, n_rows), :], buf.at[slot], sem.at[slot]).wait()


def _dispatch_kernel(src_ref, x_hbm, g_ref, o_ref, buf, sem, *, tg):
    i = pl.program_id(0)
    nsteps = pl.num_programs(0)

    @pl.when(i == 0)
    def _():
        _start_row_gather(x_hbm, src_ref, 0, tg, buf, 0, 0, sem)

    @pl.when(i + 1 < nsteps)
    def _():
        _start_row_gather(x_hbm, src_ref, (i + 1) * tg, tg, buf, jnp.bitwise_and(i + 1, 1), 0, sem)

    slot = jnp.bitwise_and(i, 1)
    _wait_row_gather(x_hbm, buf, slot, sem)
    o_ref[...] = _rms(buf[slot], g_ref[...]).astype(o_ref.dtype)


def moe_dispatch(x, g, src, tg=256):
    n, d = x.shape
    n_slots = src.shape[0]
    kern = functools.partial(_dispatch_kernel, tg=tg)
    return pl.pallas_call(
        kern,
        out_shape=jax.ShapeDtypeStruct((n_slots, d), BF16),
        grid_spec=pltpu.PrefetchScalarGridSpec(
            num_scalar_prefetch=1,
            grid=(n_slots // tg,),
            in_specs=[pl.BlockSpec(memory_space=pl.ANY),
                      pl.BlockSpec((1, d), lambda i, s: (0, 0))],
            out_specs=pl.BlockSpec((tg, d), lambda i, s: (i, 0)),
            scratch_shapes=[pltpu.VMEM((2, tg, d), F32), pltpu.SemaphoreType.DMA((2,))],
        ),
        compiler_params=_cparams(("arbitrary",)),
        name="moe_dispatch",
    )(src, x, g.reshape(1, d))


def _combine_kernel(slot_ref, y_hbm, x_ref, w_ref, g_ref, o_ref, buf, sem, *, tc, final_norm):
    i = pl.program_id(0)
    nsteps = pl.num_programs(0)
    n_tok = nsteps * tc

    def fetch(step, slot):
        for k in range(TOP_K):
            _start_row_gather(y_hbm, slot_ref, k * n_tok + step * tc, tc, buf, slot, k * tc, sem)

    @pl.when(i == 0)
    def _():
        fetch(0, 0)

    @pl.when(i + 1 < nsteps)
    def _():
        fetch(i + 1, jnp.bitwise_and(i + 1, 1))

    slot = jnp.bitwise_and(i, 1)
    _wait_row_gather(y_hbm, buf, slot, sem)
    acc = x_ref[...]
    for k in range(TOP_K):
        acc = acc + w_ref[:, k:k + 1] * buf[slot, k * tc:(k + 1) * tc, :]
    if final_norm:
        acc = _rms(acc, g_ref[...])
    o_ref[...] = acc


def moe_combine(y, x, slot, wts, g, final_norm, tc=128):
    n, d = x.shape
    kern = functools.partial(_combine_kernel, tc=tc, final_norm=final_norm)
    return pl.pallas_call(
        kern,
        out_shape=jax.ShapeDtypeStruct((n, d), F32),
        grid_spec=pltpu.PrefetchScalarGridSpec(
            num_scalar_prefetch=1,
            grid=(n // tc,),
            in_specs=[pl.BlockSpec(memory_space=pl.ANY),
                      pl.BlockSpec((tc, d), lambda i, s: (i, 0)),
                      pl.BlockSpec((tc, TOP_K), lambda i, s: (i, 0)),
                      pl.BlockSpec((1, d), lambda i, s: (0, 0))],
            out_specs=pl.BlockSpec((tc, d), lambda i, s: (i, 0)),
            scratch_shapes=[pltpu.VMEM((2, TOP_K * tc, d), F32), pltpu.SemaphoreType.DMA((2,))],
        ),
        compiler_params=_cparams(("arbitrary",)),
        name="moe_combine",
    )(slot.reshape(-1), y, x, wts.T, g.reshape(1, d))


def token_mixer(x, batch, seq, layer, norm_g, w_in, g_sgu, w_s, b_s, w_pa, w_pb, w_o):
    n, d = x.shape
    aw = ATTN_HEADS * HEAD_DIM
    sw = SGU_GROUPS * SGU_GROUP_DIM
    tn = 1024
    h = rmsnorm(x, norm_g, BF16)
    q_scale = jnp.where(jnp.arange(w_in.shape[2]) < aw, HEAD_DIM ** -0.5, 1.0).astype(F32)
    z = ws_matmul([h], [w_in], [0], [], [], _ep_plain, BF16, tn, w_base=layer,
                  col_scale=q_scale, tm=DENSE_ROW_TILE, name="in_proj")
    attn = moba_attention(z.reshape(batch, seq, z.shape[1]), batch, seq).reshape(n, aw)
    sgu = spatial_gating(z, g_sgu, w_s, b_s, (3 * aw) // sw, (3 * aw + sw) // sw)
    gate_col = (3 * aw + 2 * sw) // tn
    merged = ws_matmul([attn, sgu], [w_pa, w_pb], [0, 1], [z, z],
                       [gate_col, gate_col + d // tn], _ep_merge, BF16, tn, w_base=layer,
                       tm=DENSE_ROW_TILE, name="branch_merge")
    return ws_matmul([merged], [w_o], [0], [x], [0], _ep_residual, F32, tn, w_base=layer,
                     tm=DENSE_ROW_TILE, name="out_proj")


def dense_ffn(x, norm_g, j, wg, wu, wd):
    h = rmsnorm(x, norm_g, BF16)
    act = ws_matmul([h], [wg, wu], [0, 0], [], [], _ep_swiglu, BF16, 512, w_base=j,
                    tm=DENSE_ROW_TILE, name="ffn_up")
    return ws_matmul([act], [wd], [0], [x], [0], _ep_residual, F32, 512, w_base=j, name="ffn_down")


def moe_ffn(x, norm_g, j, router_w, router_b, wg, wu, wd, final_g, final_norm):
    n, d = x.shape
    ne = router_w.shape[1]
    tm = ROW_TILE
    eidx, wts, rank, cnt = moe_router(x, norm_g, router_w, router_b)
    counts = cnt[:, 0].astype(jnp.int32)
    padded = ((counts + tm - 1) // tm) * tm
    ends = jnp.cumsum(padded)
    starts = ends - padded
    group_start = jnp.sum(jnp.where(eidx[..., None] == jnp.arange(ne), starts, 0), axis=-1)
    slot = group_start + rank
    n_tiles = (TOP_K * n) // tm + ne
    tok = jnp.tile(jnp.arange(n, dtype=jnp.int32), TOP_K)
    src = (jnp.arange(n_tiles * tm, dtype=jnp.int32) % n).at[slot.reshape(-1)].set(tok)
    tile_row = jnp.arange(n_tiles, dtype=jnp.int32) * tm
    last_used = jnp.max(jnp.where(counts > 0, jnp.arange(ne, dtype=jnp.int32), 0))
    texp = jnp.minimum(jnp.sum(tile_row[:, None] >= ends[None, :], axis=1), last_used).astype(jnp.int32)
    group_end = jnp.sum(jnp.where(texp[:, None] == jnp.arange(ne), starts + counts, 0), axis=-1)
    tvalid = jnp.clip(group_end - tile_row, 0, tm).astype(jnp.int32)

    hs = moe_dispatch(x, norm_g, src)
    act = ws_matmul([hs], [wg, wu], [0, 0], [], [], _ep_swiglu, BF16, 512, texp=texp,
                    tvalid=tvalid, sub_rows=MOE_SUB_ROWS, w_base=j * ne, name="moe_up")
    y = ws_matmul([act], [wd], [0], [], [], _ep_plain, F32, 512, texp=texp, tvalid=tvalid,
                  sub_rows=MOE_SUB_ROWS, w_base=j * ne, name="moe_down")
    return moe_combine(y, x, slot, wts, final_g, final_norm)


def kernel(x, mix_norm_g, w_in, sgu_norm_g, w_s, b_s, w_pa, w_pb, w_o, ffn_norm_g,
           dense_w_gate, dense_w_up, dense_w_down, router_w, router_b,
           expert_w_gate, expert_w_up, expert_w_down, final_norm_g):
    batch, seq, d = x.shape
    depth = mix_norm_g.shape[0]
    xf = x.reshape(batch * seq, d)
    merge_experts = lambda w: w.reshape((w.shape[0] * w.shape[1],) + w.shape[2:])
    ewg, ewu, ewd = (merge_experts(w) for w in (expert_w_gate, expert_w_up, expert_w_down))
    normed = False
    for i in range(depth):
        xf = token_mixer(xf, batch, seq, i, mix_norm_g[i], w_in, sgu_norm_g[i], w_s[i], b_s[i],
                         w_pa, w_pb, w_o)
        j = i // 2
        last = i == depth - 1
        if i % 2 == 0:
            xf = dense_ffn(xf, ffn_norm_g[i], j, dense_w_gate, dense_w_up, dense_w_down)
        else:
            xf = moe_ffn(xf, ffn_norm_g[i], j, router_w[j], router_b[j], ewg, ewu, ewd,
                         final_norm_g, last)
            normed = last
    if not normed:
        xf = rmsnorm(xf, final_norm_g, F32)
    return xf.reshape(batch, seq, d)
```

```python
import functools

import jax
import jax.numpy as jnp
from jax import lax
from jax.experimental import pallas as pl
from jax.experimental.pallas import tpu as pltpu

F32 = jnp.float32
BF16 = jnp.bfloat16

ATTN_HEADS = 8
HEAD_DIM = 128
MOBA_BLOCK = 256
MOBA_TOPK = 3
SGU_GROUPS = 8
SGU_GROUP_DIM = 128
SGU_CHUNK = 128
TOP_K = 2
NORM_EPS = 1e-6

V7X_VMEM_BYTES = 64 * 1024 * 1024
VMEM_LIMIT_BYTES = V7X_VMEM_BYTES - 8 * 1024 * 1024

ROW_TILE = 512
DENSE_ROW_TILE = 1024
CAST_ROWS = 256
MOE_SUB_ROWS = 128


def _cparams(sem):
    return pltpu.CompilerParams(dimension_semantics=sem, vmem_limit_bytes=VMEM_LIMIT_BYTES)


def _rms(x, g):
    ms = jnp.mean(x * x, axis=-1, keepdims=True)
    return x * lax.rsqrt(ms + NORM_EPS) * g


def _rmsnorm_kernel(x_ref, g_ref, o_ref):
    o_ref[...] = _rms(x_ref[...], g_ref[...]).astype(o_ref.dtype)


def rmsnorm(x, g, out_dtype, tm=ROW_TILE):
    n, d = x.shape
    return pl.pallas_call(
        _rmsnorm_kernel,
        out_shape=jax.ShapeDtypeStruct((n, d), out_dtype),
        grid=(n // tm,),
        in_specs=[pl.BlockSpec((tm, d), lambda i: (i, 0)),
                  pl.BlockSpec((1, d), lambda i: (0, 0))],
        out_specs=pl.BlockSpec((tm, d), lambda i: (i, 0)),
        compiler_params=_cparams(("parallel",)),
        name="rmsnorm",
    )(x, g.reshape(1, d))


def _ws_kernel(texp_ref, tnext_ref, tvalid_ref, *refs, n_lhs, w_lhs, n_extra, epilogue,
               sub_rows, tn, scaled):
    n_w = len(w_lhs)
    lhs_refs = refs[:n_lhs]
    w_hbm = refs[n_lhs:n_lhs + n_w]
    extra_refs = refs[n_lhs + n_w:n_lhs + n_w + n_extra]
    n_in = n_lhs + n_w + n_extra + int(scaled)
    scale_ref = refs[n_in - 1] if scaled else None
    out_ref = refs[n_in]
    scratch = refs[n_in + 1:]
    wf_refs = scratch[:n_w]
    wb_refs = scratch[n_w:2 * n_w]
    sem, groups_seen = scratch[2 * n_w:]
    tm = out_ref.shape[0]

    n = pl.program_id(0)
    r = pl.program_id(1)
    n_col_tiles = pl.num_programs(0)

    def weight_copy(i, index, col_tile, slot):
        cols = pl.ds(pl.multiple_of(col_tile * tn, tn), tn)
        return pltpu.make_async_copy(w_hbm[i].at[index, :, cols], wf_refs[i].at[slot], sem.at[i, slot])

    def start_weights(index, col_tile, slot):
        for i in range(n_w):
            weight_copy(i, index, col_tile, slot).start()

    very_first = jnp.logical_and(n == 0, r == 0)

    @pl.when(very_first)
    def _():
        groups_seen[0] = 0
        start_weights(texp_ref[0], 0, 0)

    new_group = jnp.logical_or(r == 0, texp_ref[r] != texp_ref[jnp.maximum(r - 1, 0)])

    @pl.when(new_group)
    def _():
        seen = groups_seen[0]
        slot = jnp.bitwise_and(seen, 1)
        groups_seen[0] = seen + 1
        for i in range(n_w):
            weight_copy(i, texp_ref[r], n, slot).wait()
        following = tnext_ref[r]

        @pl.when(following >= 0)
        def _():
            start_weights(following, n, 1 - slot)

        @pl.when(jnp.logical_and(following < 0, n + 1 < n_col_tiles))
        def _():
            start_weights(texp_ref[0], n + 1, 1 - slot)

        for wf_ref, wb_ref in zip(wf_refs, wb_refs):
            k = wb_ref.shape[0]

            def cast_rows(c, carry, wf_ref=wf_ref, wb_ref=wb_ref):
                rows = pl.ds(pl.multiple_of(c * CAST_ROWS, CAST_ROWS), CAST_ROWS)
                w = wf_ref[slot, rows, :]
                if scaled:
                    w = w * scale_ref[...]
                wb_ref[rows, :] = w.astype(BF16)
                return carry

            lax.fori_loop(0, k // CAST_ROWS, cast_rows, 0)

    def compute(n_live):
        rows = slice(0, n_live)
        if n_live > 0:
            prods = [jnp.dot(lhs_refs[w_lhs[i]][rows, :], wb_refs[i][...],
                             preferred_element_type=F32) for i in range(n_w)]
            out_ref[rows, :] = epilogue(prods, [e[rows, :] for e in extra_refs]).astype(out_ref.dtype)
        if n_live < tm:
            out_ref[n_live:tm, :] = jnp.zeros((tm - n_live, out_ref.shape[1]), out_ref.dtype)

    if sub_rows is None:
        compute(tm)
    else:
        live_chunks = (tvalid_ref[r] + (sub_rows - 1)) // sub_rows
        for c in range(tm // sub_rows + 1):
            pl.when(live_chunks == c)(functools.partial(compute, c * sub_rows))


def ws_matmul(lhs, ws, w_lhs, extras, extra_col_off, epilogue, out_dtype, tn, texp=None,
              tvalid=None, sub_rows=None, w_base=0, col_scale=None, tm=ROW_TILE,
              name="ws_matmul"):
    m = lhs[0].shape[0]
    n_total = ws[0].shape[2]
    n_rows = m // tm
    if texp is None:
        texp = jnp.zeros((n_rows,), jnp.int32)
    if tvalid is None:
        tvalid = jnp.full((n_rows,), tm, jnp.int32)
    no_next = jnp.iinfo(jnp.int32).max
    later = jnp.min(jnp.where(texp[None, :] > texp[:, None], texp[None, :], no_next), axis=1)
    tnext = jnp.where(later == no_next, -1, later + w_base).astype(jnp.int32)
    texp = texp + w_base
    in_specs = []
    for a in lhs:
        in_specs.append(pl.BlockSpec((tm, a.shape[1]), lambda n, r, *_: (r, 0)))
    for w in ws:
        in_specs.append(pl.BlockSpec(memory_space=pl.ANY))
    for off in extra_col_off:
        in_specs.append(pl.BlockSpec((tm, tn), lambda n, r, *_, off=off: (r, off + n)))
    scale_args = []
    if col_scale is not None:
        in_specs.append(pl.BlockSpec((1, tn), lambda n, r, *_: (0, n)))
        scale_args.append(col_scale.reshape(1, n_total))
    kern = functools.partial(_ws_kernel, n_lhs=len(lhs), w_lhs=tuple(w_lhs),
                             n_extra=len(extras), epilogue=epilogue, sub_rows=sub_rows, tn=tn,
                             scaled=col_scale is not None)
    scratch = ([pltpu.VMEM((2, w.shape[1], tn), F32) for w in ws]
               + [pltpu.VMEM((w.shape[1], tn), BF16) for w in ws]
               + [pltpu.SemaphoreType.DMA((len(ws), 2)), pltpu.SMEM((1,), jnp.int32)])
    return pl.pallas_call(
        kern,
        out_shape=jax.ShapeDtypeStruct((m, n_total), out_dtype),
        grid_spec=pltpu.PrefetchScalarGridSpec(
            num_scalar_prefetch=3,
            grid=(n_total // tn, n_rows),
            in_specs=in_specs,
            out_specs=pl.BlockSpec((tm, tn), lambda n, r, *_: (r, n)),
            scratch_shapes=scratch,
        ),
        compiler_params=_cparams(("arbitrary", "arbitrary")),
        name=name,
    )(texp, tnext, tvalid, *lhs, *ws, *extras, *scale_args)


def _ep_plain(prods, extras):
    return prods[0]


def _ep_residual(prods, extras):
    return extras[0] + prods[0]


def _ep_swiglu(prods, extras):
    g, u = prods
    return (g * jax.nn.sigmoid(g)) * u


def _ep_merge(prods, extras):
    ga, gb = extras
    return (jax.nn.sigmoid(ga.astype(F32)) * prods[0]
            + jax.nn.sigmoid(gb.astype(F32)) * prods[1])


def _attn_kernel(q_ref, k_ref, v_ref, o_ref, *, nb, blk, topk):
    seq = k_ref.shape[0]
    neg_inf = jnp.float32(-jnp.inf)
    avg = jnp.where(lax.broadcasted_iota(jnp.int32, (nb, seq), 1) // blk
                    == lax.broadcasted_iota(jnp.int32, (nb, seq), 0), 1.0 / blk, 0.0).astype(BF16)
    k_mean = jnp.dot(avg, k_ref[...], preferred_element_type=F32)
    k_mean_hi = k_mean.astype(BF16)
    k_mean_lo = (k_mean - k_mean_hi.astype(F32)).astype(BF16)
    blk_id = lax.broadcasted_iota(jnp.int32, (blk, nb), 1)
    causal = (lax.broadcasted_iota(jnp.int32, (blk, blk), 1)
              <= lax.broadcasted_iota(jnp.int32, (blk, blk), 0))
    nt_dims = (((1,), (1,)), ((), ()))

    for n in range(nb):
        qn = q_ref[n * blk:(n + 1) * blk, :]
        nk = (n + 1) * blk
        s = lax.dot_general(qn, k_ref[0:nk, :], nt_dims, preferred_element_type=F32)
        past = [s[:, j * blk:(j + 1) * blk] for j in range(n)]
        if n > topk:
            gate = (lax.dot_general(qn, k_mean_hi, nt_dims, preferred_element_type=F32)
                    + lax.dot_general(qn, k_mean_lo, nt_dims, preferred_element_type=F32))
            beaten = jnp.zeros((blk, nb), F32)
            for j in range(n):
                gj = gate[:, j:j + 1]
                wins = jnp.logical_or(gj > gate, jnp.logical_and(gj == gate, j < blk_id))
                beaten = beaten + wins.astype(F32)
            bias = jnp.where(beaten < topk, 0.0, neg_inf)
            past = [past[j] + bias[:, j:j + 1] for j in range(n)]
        own = jnp.where(causal, s[:, n * blk:nk], neg_inf)
        s = jnp.concatenate(past + [own], axis=1)
        m = jnp.max(s, axis=-1, keepdims=True)
        p = jnp.exp(s - m)
        l = jnp.sum(p, axis=-1, keepdims=True)
        o = jnp.dot(p.astype(BF16), v_ref[0:nk, :], preferred_element_type=F32)
        o_ref[n * blk:(n + 1) * blk, :] = (o / l).astype(o_ref.dtype)


def moba_attention(z, batch, seq):
    nb = seq // MOBA_BLOCK
    kern = functools.partial(_attn_kernel, nb=nb, blk=MOBA_BLOCK, topk=MOBA_TOPK)
    blockspec = lambda off: pl.BlockSpec((None, seq, HEAD_DIM), lambda b, h: (b, 0, off + h))
    return pl.pallas_call(
        kern,
        out_shape=jax.ShapeDtypeStruct((batch, seq, ATTN_HEADS * HEAD_DIM), BF16),
        grid=(batch, ATTN_HEADS),
        in_specs=[blockspec(0), blockspec(ATTN_HEADS), blockspec(2 * ATTN_HEADS)],
        out_specs=pl.BlockSpec((None, seq, HEAD_DIM), lambda b, h: (b, 0, h)),
        compiler_params=_cparams(("parallel", "parallel")),
        name="moba_attention",
    )(z, z, z)


def _gelu_tanh(x):
    return 0.5 * x * (1.0 + jnp.tanh(0.7978845608028654 * (x + 0.044715 * (x * x * x))))


def _sgu_kernel(u_ref, v_ref, g_ref, w_ref, bt_ref, o_ref, *, chunk, groups, gd):
    tm = u_ref.shape[0]
    u = _gelu_tanh(u_ref[...].astype(F32))
    v = _gelu_tanh(v_ref[...].astype(F32))
    mu = jnp.mean(v, axis=-1, keepdims=True)
    vc = v - mu
    var = jnp.mean(vc * vc, axis=-1, keepdims=True)
    vn = (vc * lax.rsqrt(var + NORM_EPS) * g_ref[...]).astype(BF16)
    t_i = lax.broadcasted_iota(jnp.int32, (chunk, chunk), 0)
    s_i = lax.broadcasted_iota(jnp.int32, (chunk, chunk), 1)
    lower = s_i <= t_i
    for g in range(groups):
        w = jnp.where(lower, w_ref[g], 0.0).astype(BF16)
        bias = bt_ref[:, g:g + 1]
        cols = slice(g * gd, (g + 1) * gd)
        for c in range(tm // chunk):
            rows = slice(c * chunk, (c + 1) * chunk)
            mixed = jnp.dot(w, vn[rows, cols], preferred_element_type=F32) + bias
            o_ref[rows, cols] = (u[rows, cols] * mixed).astype(o_ref.dtype)


def spatial_gating(z, g_sgu, w_s, b_s, col_u, col_v, tm=256):
    n = z.shape[0]
    width = SGU_GROUPS * SGU_GROUP_DIM
    kern = functools.partial(_sgu_kernel, chunk=SGU_CHUNK, groups=SGU_GROUPS, gd=SGU_GROUP_DIM)
    return pl.pallas_call(
        kern,
        out_shape=jax.ShapeDtypeStruct((n, width), BF16),
        grid=(n // tm,),
        in_specs=[pl.BlockSpec((tm, width), lambda i: (i, col_u)),
                  pl.BlockSpec((tm, width), lambda i: (i, col_v)),
                  pl.BlockSpec((1, width), lambda i: (0, 0)),
                  pl.BlockSpec(w_s.shape, lambda i: (0, 0, 0)),
                  pl.BlockSpec((SGU_CHUNK, SGU_GROUPS), lambda i: (0, 0))],
        out_specs=pl.BlockSpec((tm, width), lambda i: (i, 0)),
        compiler_params=_cparams(("parallel",)),
        name="spatial_gating",
    )(z, z, g_sgu.reshape(1, width), w_s, b_s.T)


def _router_kernel(x_ref, g_ref, rwt_ref, rb_ref, eidx_ref, wts_ref, rank_ref, cnt_ref):
    i = pl.program_id(0)
    ne = rwt_ref.shape[0]
    tm = x_ref.shape[0]

    @pl.when(i == 0)
    def _():
        cnt_ref[...] = jnp.zeros_like(cnt_ref)

    h = _rms(x_ref[...], g_ref[...])
    h_hi = h.astype(BF16)
    h_lo = (h - h_hi.astype(F32)).astype(BF16)
    rw = rwt_ref[...]
    rw_hi = rw.astype(BF16)
    rw_hi_f32 = rw_hi.astype(F32)
    rw_parts = jnp.concatenate([rw_hi_f32, rw - rw_hi_f32], axis=0).astype(BF16)
    nt_dims = (((1,), (1,)), ((), ()))
    both = lax.dot_general(rw_parts, h_hi, nt_dims, preferred_element_type=F32)
    logits = (both[:ne] + both[ne:]
              + lax.dot_general(rw_hi, h_lo, nt_dims, preferred_element_type=F32)
              + rb_ref[...])
    row = lax.broadcasted_iota(jnp.int32, (ne, tm), 0).astype(F32)
    neg_inf = jnp.float32(-jnp.inf)
    m0 = jnp.max(logits, axis=0, keepdims=True)
    i0 = jnp.min(jnp.where(logits == m0, row, float(ne)), axis=0, keepdims=True)
    rest = jnp.where(row == i0, neg_inf, logits)
    m1 = jnp.max(rest, axis=0, keepdims=True)
    i1 = jnp.min(jnp.where(jnp.logical_and(rest == m1, row != i0), row, float(ne)),
                 axis=0, keepdims=True)
    e1 = jnp.exp(m1 - m0)
    denom = 1.0 + e1
    eidx_ref[0:1, :] = i0.astype(jnp.int32)
    eidx_ref[1:2, :] = i1.astype(jnp.int32)
    wts_ref[0:1, :] = 1.0 / denom
    wts_ref[1:2, :] = e1 / denom

    hit0 = row == i0
    hit1 = row == i1
    chosen = jnp.logical_or(hit0, hit1).astype(BF16)
    earlier = (lax.broadcasted_iota(jnp.int32, (tm, tm), 0)
               < lax.broadcasted_iota(jnp.int32, (tm, tm), 1)).astype(BF16)
    before = jnp.dot(chosen, earlier, preferred_element_type=F32) + cnt_ref[:, 0:1]
    rank_ref[0:1, :] = jnp.sum(jnp.where(hit0, before, 0.0), axis=0, keepdims=True).astype(jnp.int32)
    rank_ref[1:2, :] = jnp.sum(jnp.where(hit1, before, 0.0), axis=0, keepdims=True).astype(jnp.int32)
    cnt_ref[...] = cnt_ref[...] + jnp.sum(chosen.astype(F32), axis=1, keepdims=True)


def moe_router(x, g, router_w, router_b, tm=ROW_TILE):
    n, d = x.shape
    ne = router_w.shape[1]
    out_shapes = (jax.ShapeDtypeStruct((TOP_K, n), jnp.int32),
                  jax.ShapeDtypeStruct((TOP_K, n), F32),
                  jax.ShapeDtypeStruct((TOP_K, n), jnp.int32),
                  jax.ShapeDtypeStruct((ne, 128), F32))
    tok_spec = pl.BlockSpec((TOP_K, tm), lambda i: (0, i))
    return pl.pallas_call(
        _router_kernel,
        out_shape=out_shapes,
        grid=(n // tm,),
        in_specs=[pl.BlockSpec((tm, d), lambda i: (i, 0)),
                  pl.BlockSpec((1, d), lambda i: (0, 0)),
                  pl.BlockSpec((ne, d), lambda i: (0, 0)),
                  pl.BlockSpec((ne, 1), lambda i: (0, 0))],
        out_specs=(tok_spec, tok_spec, tok_spec, pl.BlockSpec((ne, 128), lambda i: (0, 0))),
        compiler_params=_cparams(("arbitrary",)),
        name="moe_router",
    )(x, g.reshape(1, d), router_w.T, router_b.reshape(ne, 1))


ISSUE_UNROLL = 8


def _start_row_gather(src_hbm, idx_ref, idx_base, n_rows, buf, slot, row_base, sem):
    def body(jj, carry):
        for u in range(ISSUE_UNROLL):
            j = jj * ISSUE_UNROLL + u
            row = idx_ref[idx_base + j]
            pltpu.make_async_copy(src_hbm.at[pl.ds(row, 1), :],
                                  buf.at[slot, pl.ds(row_base + j, 1), :],
                                  sem.at[slot]).start(priority=u % 2)
        return carry
    lax.fori_loop(0, n_rows // ISSUE_UNROLL, body, 0)


def _wait_row_gather(src_hbm, buf, slot, sem):
    n_rows = buf.shape[1]
    pltpu.make_async_copy(src_hbm.at[pl.ds(0, n_rows), :], buf.at[slot], sem.at[slot]).wait()


def _dispatch_kernel(src_ref, x_hbm, g_ref, o_ref, buf, sem, *, tg):
    i = pl.program_id(0)
    nsteps = pl.num_programs(0)

    @pl.when(i == 0)
    def _():
        _start_row_gather(x_hbm, src_ref, 0, tg, buf, 0, 0, sem)

    @pl.when(i + 1 < nsteps)
    def _():
        _start_row_gather(x_hbm, src_ref, (i + 1) * tg, tg, buf, jnp.bitwise_and(i + 1, 1), 0, sem)

    slot = jnp.bitwise_and(i, 1)
    _wait_row_gather(x_hbm, buf, slot, sem)
    o_ref[...] = _rms(buf[slot], g_ref[...]).astype(o_ref.dtype)


def moe_dispatch(x, g, src, tg=256):
    n, d = x.shape
    n_slots = src.shape[0]
    kern = functools.partial(_dispatch_kernel, tg=tg)
    return pl.pallas_call(
        kern,
        out_shape=jax.ShapeDtypeStruct((n_slots, d), BF16),
        grid_spec=pltpu.PrefetchScalarGridSpec(
            num_scalar_prefetch=1,
            grid=(n_slots // tg,),
            in_specs=[pl.BlockSpec(memory_space=pl.ANY),
                      pl.BlockSpec((1, d), lambda i, s: (0, 0))],
            out_specs=pl.BlockSpec((tg, d), lambda i, s: (i, 0)),
            scratch_shapes=[pltpu.VMEM((2, tg, d), F32), pltpu.SemaphoreType.DMA((2,))],
        ),
        compiler_params=_cparams(("arbitrary",)),
        name="moe_dispatch",
    )(src, x, g.reshape(1, d))


def _combine_kernel(slot_ref, y_hbm, x_ref, w_ref, g_ref, o_ref, buf, sem, *, tc, final_norm):
    i = pl.program_id(0)
    nsteps = pl.num_programs(0)
    n_tok = nsteps * tc

    def fetch(step, slot):
        for k in range(TOP_K):
            _start_row_gather(y_hbm, slot_ref, k * n_tok + step * tc, tc, buf, slot, k * tc, sem)

    @pl.when(i == 0)
    def _():
        fetch(0, 0)

    @pl.when(i + 1 < nsteps)
    def _():
        fetch(i + 1, jnp.bitwise_and(i + 1, 1))

    slot = jnp.bitwise_and(i, 1)
    _wait_row_gather(y_hbm, buf, slot, sem)
    acc = x_ref[...]
    for k in range(TOP_K):
        acc = acc + w_ref[:, k:k + 1] * buf[slot, k * tc:(k + 1) * tc, :]
    if final_norm:
        acc = _rms(acc, g_ref[...])
    o_ref[...] = acc


def moe_combine(y, x, slot, wts, g, final_norm, tc=128):
    n, d = x.shape
    kern = functools.partial(_combine_kernel, tc=tc, final_norm=final_norm)
    return pl.pallas_call(
        kern,
        out_shape=jax.ShapeDtypeStruct((n, d), F32),
        grid_spec=pltpu.PrefetchScalarGridSpec(
            num_scalar_prefetch=1,
            grid=(n // tc,),
            in_specs=[pl.BlockSpec(memory_space=pl.ANY),
                      pl.BlockSpec((tc, d), lambda i, s: (i, 0)),
                      pl.BlockSpec((tc, TOP_K), lambda i, s: (i, 0)),
                      pl.BlockSpec((1, d), lambda i, s: (0, 0))],
            out_specs=pl.BlockSpec((tc, d), lambda i, s: (i, 0)),
            scratch_shapes=[pltpu.VMEM((2, TOP_K * tc, d), F32), pltpu.SemaphoreType.DMA((2,))],
        ),
        compiler_params=_cparams(("arbitrary",)),
        name="moe_combine",
    )(slot.reshape(-1), y, x, wts.T, g.reshape(1, d))


def token_mixer(x, batch, seq, layer, norm_g, w_in, g_sgu, w_s, b_s, w_pa, w_pb, w_o):
    n, d = x.shape
    aw = ATTN_HEADS * HEAD_DIM
    sw = SGU_GROUPS * SGU_GROUP_DIM
    tn = 1024
    h = rmsnorm(x, norm_g, BF16)
    q_scale = jnp.where(jnp.arange(w_in.shape[2]) < aw, HEAD_DIM ** -0.5, 1.0).astype(F32)
    z = ws_matmul([h], [w_in], [0], [], [], _ep_plain, BF16, tn, w_base=layer,
                  col_scale=q_scale, tm=DENSE_ROW_TILE, name="in_proj")
    attn = moba_attention(z.reshape(batch, seq, z.shape[1]), batch, seq).reshape(n, aw)
    sgu = spatial_gating(z, g_sgu, w_s, b_s, (3 * aw) // sw, (3 * aw + sw) // sw)
    gate_col = (3 * aw + 2 * sw) // tn
    merged = ws_matmul([attn, sgu], [w_pa, w_pb], [0, 1], [z, z],
                       [gate_col, gate_col + d // tn], _ep_merge, BF16, tn, w_base=layer,
                       tm=DENSE_ROW_TILE, name="branch_merge")
    return ws_matmul([merged], [w_o], [0], [x], [0], _ep_residual, F32, tn, w_base=layer,
                     tm=DENSE_ROW_TILE, name="out_proj")


def dense_ffn(x, norm_g, j, wg, wu, wd):
    h = rmsnorm(x, norm_g, BF16)
    act = ws_matmul([h], [wg, wu], [0, 0], [], [], _ep_swiglu, BF16, 512, w_base=j,
                    tm=DENSE_ROW_TILE, name="ffn_up")
    return ws_matmul([act], [wd], [0], [x], [0], _ep_residual, F32, 512, w_base=j, name="ffn_down")


def moe_ffn(x, norm_g, j, router_w, router_b, wg, wu, wd, final_g, final_norm):
    n, d = x.shape
    ne = router_w.shape[1]
    tm = ROW_TILE
    eidx, wts, rank, cnt = moe_router(x, norm_g, router_w, router_b)
    counts = cnt[:, 0].astype(jnp.int32)
    padded = ((counts + tm - 1) // tm) * tm
    ends = jnp.cumsum(padded)
    starts = ends - padded
    group_start = jnp.sum(jnp.where(eidx[..., None] == jnp.arange(ne), starts, 0), axis=-1)
    slot = group_start + rank
    n_tiles = (TOP_K * n) // tm + ne
    tok = jnp.tile(jnp.arange(n, dtype=jnp.int32), TOP_K)
    src = (jnp.arange(n_tiles * tm, dtype=jnp.int32) % n).at[slot.reshape(-1)].set(tok)
    tile_row = jnp.arange(n_tiles, dtype=jnp.int32) * tm
    last_used = jnp.max(jnp.where(counts > 0, jnp.arange(ne, dtype=jnp.int32), 0))
    texp = jnp.minimum(jnp.sum(tile_row[:, None] >= ends[None, :], axis=1), last_used).astype(jnp.int32)
    group_end = jnp.sum(jnp.where(texp[:, None] == jnp.arange(ne), starts + counts, 0), axis=-1)
    tvalid = jnp.clip(group_end - tile_row, 0, tm).astype(jnp.int32)

    hs = moe_dispatch(x, norm_g, src)
    act = ws_matmul([hs], [wg, wu], [0, 0], [], [], _ep_swiglu, BF16, 512, texp=texp,
                    tvalid=tvalid, sub_rows=MOE_SUB_ROWS, w_base=j * ne, name="moe_up")
    y = ws_matmul([act], [wd], [0], [], [], _ep_plain, F32, 512, texp=texp, tvalid=tvalid,
                  sub_rows=MOE_SUB_ROWS, w_base=j * ne, name="moe_down")
    return moe_combine(y, x, slot, wts, final_g, final_norm)


def kernel(x, mix_norm_g, w_in, sgu_norm_g, w_s, b_s, w_pa, w_pb, w_o, ffn_norm_g,
           dense_w_gate, dense_w_up, dense_w_down, router_w, router_b,
           expert_w_gate, expert_w_up, expert_w_down, final_norm_g):
    batch, seq, d = x.shape
    depth = mix_norm_g.shape[0]
    xf = x.reshape(batch * seq, d)
    merge_experts = lambda w: w.reshape((w.shape[0] * w.shape[1],) + w.shape[2:])
    ewg, ewu, ewd = (merge_experts(w) for w in (expert_w_gate, expert_w_up, expert_w_down))
    normed = False
    for i in range(depth):
        xf = token_mixer(xf, batch, seq, i, mix_norm_g[i], w_in, sgu_norm_g[i], w_s[i], b_s[i],
                         w_pa, w_pb, w_o)
        j = i // 2
        last = i == depth - 1
        if i % 2 == 0:
            xf = dense_ffn(xf, ffn_norm_g[i], j, dense_w_gate, dense_w_up, dense_w_down)
        else:
            xf = moe_ffn(xf, ffn_norm_g[i], j, router_w[j], router_b[j], ewg, ewu, ewd,
                         final_norm_g, last)
            normed = last
    if not normed:
        xf = rmsnorm(xf, final_norm_g, F32)
    return xf.reshape(batch, seq, d)
```

```python
import functools

import jax
import jax.numpy as jnp
from jax import lax
from jax.experimental import pallas as pl
from jax.experimental.pallas import tpu as pltpu

F32 = jnp.float32
BF16 = jnp.bfloat16

ATTN_HEADS = 8
HEAD_DIM = 128
MOBA_BLOCK = 256
MOBA_TOPK = 3
SGU_GROUPS = 8
SGU_GROUP_DIM = 128
SGU_CHUNK = 128
TOP_K = 2
NORM_EPS = 1e-6

V7X_VMEM_BYTES = 64 * 1024 * 1024
VMEM_LIMIT_BYTES = V7X_VMEM_BYTES - 8 * 1024 * 1024

ROW_TILE = 512
DENSE_ROW_TILE = 1024
CAST_ROWS = 256
MOE_SUB_ROWS = 128


def _cparams(sem):
    return pltpu.CompilerParams(dimension_semantics=sem, vmem_limit_bytes=VMEM_LIMIT_BYTES)


def _rms(x, g):
    ms = jnp.mean(x * x, axis=-1, keepdims=True)
    return x * lax.rsqrt(ms + NORM_EPS) * g


def _rmsnorm_kernel(x_ref, g_ref, o_ref):
    o_ref[...] = _rms(x_ref[...], g_ref[...]).astype(o_ref.dtype)


def rmsnorm(x, g, out_dtype, tm=ROW_TILE):
    n, d = x.shape
    return pl.pallas_call(
        _rmsnorm_kernel,
        out_shape=jax.ShapeDtypeStruct((n, d), out_dtype),
        grid=(n // tm,),
        in_specs=[pl.BlockSpec((tm, d), lambda i: (i, 0)),
                  pl.BlockSpec((1, d), lambda i: (0, 0))],
        out_specs=pl.BlockSpec((tm, d), lambda i: (i, 0)),
        compiler_params=_cparams(("parallel",)),
        name="rmsnorm",
    )(x, g.reshape(1, d))


def _ws_kernel(texp_ref, tnext_ref, tvalid_ref, *refs, n_lhs, w_lhs, n_extra, epilogue,
               sub_rows, tn, scaled):
    n_w = len(w_lhs)
    lhs_refs = refs[:n_lhs]
    w_hbm = refs[n_lhs:n_lhs + n_w]
    extra_refs = refs[n_lhs + n_w:n_lhs + n_w + n_extra]
    n_in = n_lhs + n_w + n_extra + int(scaled)
    scale_ref = refs[n_in - 1] if scaled else None
    out_ref = refs[n_in]
    scratch = refs[n_in + 1:]
    wf_refs = scratch[:n_w]
    wb_refs = scratch[n_w:2 * n_w]
    sem = scratch[2 * n_w]
    tm = out_ref.shape[0]

    n = pl.program_id(0)
    r = pl.program_id(1)
    n_col_tiles = pl.num_programs(0)

    def weight_copy(i, index, col_tile):
        cols = pl.ds(pl.multiple_of(col_tile * tn, tn), tn)
        return pltpu.make_async_copy(w_hbm[i].at[index, :, cols], wf_refs[i], sem.at[i])

    def start_weights(index, col_tile):
        for i in range(n_w):
            weight_copy(i, index, col_tile).start()

    @pl.when(jnp.logical_and(n == 0, r == 0))
    def _():
        start_weights(texp_ref[0], 0)

    new_group = jnp.logical_or(r == 0, texp_ref[r] != texp_ref[jnp.maximum(r - 1, 0)])

    @pl.when(new_group)
    def _():
        for i in range(n_w):
            weight_copy(i, texp_ref[r], n).wait()
        for wf_ref, wb_ref in zip(wf_refs, wb_refs):
            k = wb_ref.shape[0]

            def cast_rows(c, carry, wf_ref=wf_ref, wb_ref=wb_ref):
                rows = pl.ds(pl.multiple_of(c * CAST_ROWS, CAST_ROWS), CAST_ROWS)
                w = wf_ref[rows, :]
                if scaled:
                    w = w * scale_ref[...]
                wb_ref[rows, :] = w.astype(BF16)
                return carry

            lax.fori_loop(0, k // CAST_ROWS, cast_rows, 0)

        following = tnext_ref[r]

        @pl.when(following >= 0)
        def _():
            start_weights(following, n)

        @pl.when(jnp.logical_and(following < 0, n + 1 < n_col_tiles))
        def _():
            start_weights(texp_ref[0], n + 1)

    def compute(n_live):
        rows = slice(0, n_live)
        if n_live > 0:
            prods = [jnp.dot(lhs_refs[w_lhs[i]][rows, :], wb_refs[i][...],
                             preferred_element_type=F32) for i in range(n_w)]
            out_ref[rows, :] = epilogue(prods, [e[rows, :] for e in extra_refs]).astype(out_ref.dtype)
        if n_live < tm:
            out_ref[n_live:tm, :] = jnp.zeros((tm - n_live, out_ref.shape[1]), out_ref.dtype)

    if sub_rows is None:
        compute(tm)
    else:
        live_chunks = (tvalid_ref[r] + (sub_rows - 1)) // sub_rows
        for c in range(tm // sub_rows + 1):
            pl.when(live_chunks == c)(functools.partial(compute, c * sub_rows))


def ws_matmul(lhs, ws, w_lhs, extras, extra_col_off, epilogue, out_dtype, tn, texp=None,
              tvalid=None, sub_rows=None, w_base=0, col_scale=None, tm=ROW_TILE,
              name="ws_matmul"):
    m = lhs[0].shape[0]
    n_total = ws[0].shape[2]
    n_rows = m // tm
    if texp is None:
        texp = jnp.zeros((n_rows,), jnp.int32)
    if tvalid is None:
        tvalid = jnp.full((n_rows,), tm, jnp.int32)
    no_next = jnp.iinfo(jnp.int32).max
    later = jnp.min(jnp.where(texp[None, :] > texp[:, None], texp[None, :], no_next), axis=1)
    tnext = jnp.where(later == no_next, -1, later + w_base).astype(jnp.int32)
    texp = texp + w_base
    in_specs = []
    for a in lhs:
        in_specs.append(pl.BlockSpec((tm, a.shape[1]), lambda n, r, *_: (r, 0)))
    for w in ws:
        in_specs.append(pl.BlockSpec(memory_space=pl.ANY))
    for off in extra_col_off:
        in_specs.append(pl.BlockSpec((tm, tn), lambda n, r, *_, off=off: (r, off + n)))
    scale_args = []
    if col_scale is not None:
        in_specs.append(pl.BlockSpec((1, tn), lambda n, r, *_: (0, n)))
        scale_args.append(col_scale.reshape(1, n_total))
    kern = functools.partial(_ws_kernel, n_lhs=len(lhs), w_lhs=tuple(w_lhs),
                             n_extra=len(extras), epilogue=epilogue, sub_rows=sub_rows, tn=tn,
                             scaled=col_scale is not None)
    scratch = ([pltpu.VMEM((w.shape[1], tn), F32) for w in ws]
               + [pltpu.VMEM((w.shape[1], tn), BF16) for w in ws]
               + [pltpu.SemaphoreType.DMA((len(ws),))])
    return pl.pallas_call(
        kern,
        out_shape=jax.ShapeDtypeStruct((m, n_total), out_dtype),
        grid_spec=pltpu.PrefetchScalarGridSpec(
            num_scalar_prefetch=3,
            grid=(n_total // tn, n_rows),
            in_specs=in_specs,
            out_specs=pl.BlockSpec((tm, tn), lambda n, r, *_: (r, n)),
            scratch_shapes=scratch,
        ),
        compiler_params=_cparams(("arbitrary", "arbitrary")),
        name=name,
    )(texp, tnext, tvalid, *lhs, *ws, *extras, *scale_args)


def _ep_plain(prods, extras):
    return prods[0]


def _ep_residual(prods, extras):
    return extras[0] + prods[0]


def _ep_swiglu(prods, extras):
    g, u = prods
    return (g * jax.nn.sigmoid(g)) * u


def _ep_merge(prods, extras):
    ga, gb = extras
    return (jax.nn.sigmoid(ga.astype(F32)) * prods[0]
            + jax.nn.sigmoid(gb.astype(F32)) * prods[1])


def _attn_kernel(q_ref, k_ref, v_ref, o_ref, *, nb, blk, topk):
    seq = k_ref.shape[0]
    neg_inf = jnp.float32(-jnp.inf)
    avg = jnp.where(lax.broadcasted_iota(jnp.int32, (nb, seq), 1) // blk
                    == lax.broadcasted_iota(jnp.int32, (nb, seq), 0), 1.0 / blk, 0.0).astype(BF16)
    k_mean = jnp.dot(avg, k_ref[...], preferred_element_type=F32)
    k_mean_hi = k_mean.astype(BF16)
    k_mean_lo = (k_mean - k_mean_hi.astype(F32)).astype(BF16)
    blk_id = lax.broadcasted_iota(jnp.int32, (blk, nb), 1)
    causal = (lax.broadcasted_iota(jnp.int32, (blk, blk), 1)
              <= lax.broadcasted_iota(jnp.int32, (blk, blk), 0))
    nt_dims = (((1,), (1,)), ((), ()))

    for n in range(nb):
        qn = q_ref[n * blk:(n + 1) * blk, :]
        nk = (n + 1) * blk
        s = lax.dot_general(qn, k_ref[0:nk, :], nt_dims, preferred_element_type=F32)
        past = [s[:, j * blk:(j + 1) * blk] for j in range(n)]
        if n > topk:
            gate = (lax.dot_general(qn, k_mean_hi, nt_dims, preferred_element_type=F32)
                    + lax.dot_general(qn, k_mean_lo, nt_dims, preferred_element_type=F32))
            beaten = jnp.zeros((blk, nb), F32)
            for j in range(n):
                gj = gate[:, j:j + 1]
                wins = jnp.logical_or(gj > gate, jnp.logical_and(gj == gate, j < blk_id))
                beaten = beaten + wins.astype(F32)
            bias = jnp.where(beaten < topk, 0.0, neg_inf)
            past = [past[j] + bias[:, j:j + 1] for j in range(n)]
        own = jnp.where(causal, s[:, n * blk:nk], neg_inf)
        s = jnp.concatenate(past + [own], axis=1)
        m = jnp.max(s, axis=-1, keepdims=True)
        p = jnp.exp(s - m)
        l = jnp.sum(p, axis=-1, keepdims=True)
        o = jnp.dot(p.astype(BF16), v_ref[0:nk, :], preferred_element_type=F32)
        o_ref[n * blk:(n + 1) * blk, :] = (o / l).astype(o_ref.dtype)


def moba_attention(z, batch, seq):
    nb = seq // MOBA_BLOCK
    kern = functools.partial(_attn_kernel, nb=nb, blk=MOBA_BLOCK, topk=MOBA_TOPK)
    blockspec = lambda off: pl.BlockSpec((None, seq, HEAD_DIM), lambda b, h: (b, 0, off + h))
    return pl.pallas_call(
        kern,
        out_shape=jax.ShapeDtypeStruct((batch, seq, ATTN_HEADS * HEAD_DIM), BF16),
        grid=(batch, ATTN_HEADS),
        in_specs=[blockspec(0), blockspec(ATTN_HEADS), blockspec(2 * ATTN_HEADS)],
        out_specs=pl.BlockSpec((None, seq, HEAD_DIM), lambda b, h: (b, 0, h)),
        compiler_params=_cparams(("parallel", "parallel")),
        name="moba_attention",
    )(z, z, z)


def _gelu_tanh(x):
    return 0.5 * x * (1.0 + jnp.tanh(0.7978845608028654 * (x + 0.044715 * (x * x * x))))


def _sgu_kernel(u_ref, v_ref, g_ref, w_ref, bt_ref, o_ref, *, chunk, groups, gd):
    tm = u_ref.shape[0]
    u = _gelu_tanh(u_ref[...].astype(F32))
    v = _gelu_tanh(v_ref[...].astype(F32))
    mu = jnp.mean(v, axis=-1, keepdims=True)
    vc = v - mu
    var = jnp.mean(vc * vc, axis=-1, keepdims=True)
    vn = (vc * lax.rsqrt(var + NORM_EPS) * g_ref[...]).astype(BF16)
    t_i = lax.broadcasted_iota(jnp.int32, (chunk, chunk), 0)
    s_i = lax.broadcasted_iota(jnp.int32, (chunk, chunk), 1)
    lower = s_i <= t_i
    for g in range(groups):
        w = jnp.where(lower, w_ref[g], 0.0).astype(BF16)
        bias = bt_ref[:, g:g + 1]
        cols = slice(g * gd, (g + 1) * gd)
        for c in range(tm // chunk):
            rows = slice(c * chunk, (c + 1) * chunk)
            mixed = jnp.dot(w, vn[rows, cols], preferred_element_type=F32) + bias
            o_ref[rows, cols] = (u[rows, cols] * mixed).astype(o_ref.dtype)


def spatial_gating(z, g_sgu, w_s, b_s, col_u, col_v, tm=256):
    n = z.shape[0]
    width = SGU_GROUPS * SGU_GROUP_DIM
    kern = functools.partial(_sgu_kernel, chunk=SGU_CHUNK, groups=SGU_GROUPS, gd=SGU_GROUP_DIM)
    return pl.pallas_call(
        kern,
        out_shape=jax.ShapeDtypeStruct((n, width), BF16),
        grid=(n // tm,),
        in_specs=[pl.BlockSpec((tm, width), lambda i: (i, col_u)),
                  pl.BlockSpec((tm, width), lambda i: (i, col_v)),
                  pl.BlockSpec((1, width), lambda i: (0, 0)),
                  pl.BlockSpec(w_s.shape, lambda i: (0, 0, 0)),
                  pl.BlockSpec((SGU_CHUNK, SGU_GROUPS), lambda i: (0, 0))],
        out_specs=pl.BlockSpec((tm, width), lambda i: (i, 0)),
        compiler_params=_cparams(("parallel",)),
        name="spatial_gating",
    )(z, z, g_sgu.reshape(1, width), w_s, b_s.T)


def _router_kernel(x_ref, g_ref, rwt_ref, rb_ref, eidx_ref, wts_ref, rank_ref, cnt_ref):
    i = pl.program_id(0)
    ne = rwt_ref.shape[0]
    tm = x_ref.shape[0]

    @pl.when(i == 0)
    def _():
        cnt_ref[...] = jnp.zeros_like(cnt_ref)

    h = _rms(x_ref[...], g_ref[...])
    h_hi = h.astype(BF16)
    h_lo = (h - h_hi.astype(F32)).astype(BF16)
    rw = rwt_ref[...]
    rw_hi = rw.astype(BF16)
    rw_hi_f32 = rw_hi.astype(F32)
    rw_parts = jnp.concatenate([rw_hi_f32, rw - rw_hi_f32], axis=0).astype(BF16)
    nt_dims = (((1,), (1,)), ((), ()))
    both = lax.dot_general(rw_parts, h_hi, nt_dims, preferred_element_type=F32)
    logits = (both[:ne] + both[ne:]
              + lax.dot_general(rw_hi, h_lo, nt_dims, preferred_element_type=F32)
              + rb_ref[...])
    row = lax.broadcasted_iota(jnp.int32, (ne, tm), 0).astype(F32)
    neg_inf = jnp.float32(-jnp.inf)
    m0 = jnp.max(logits, axis=0, keepdims=True)
    i0 = jnp.min(jnp.where(logits == m0, row, float(ne)), axis=0, keepdims=True)
    rest = jnp.where(row == i0, neg_inf, logits)
    m1 = jnp.max(rest, axis=0, keepdims=True)
    i1 = jnp.min(jnp.where(jnp.logical_and(rest == m1, row != i0), row, float(ne)),
                 axis=0, keepdims=True)
    e1 = jnp.exp(m1 - m0)
    denom = 1.0 + e1
    eidx_ref[0:1, :] = i0.astype(jnp.int32)
    eidx_ref[1:2, :] = i1.astype(jnp.int32)
    wts_ref[0:1, :] = 1.0 / denom
    wts_ref[1:2, :] = e1 / denom

    hit0 = row == i0
    hit1 = row == i1
    chosen = jnp.logical_or(hit0, hit1).astype(BF16)
    earlier = (lax.broadcasted_iota(jnp.int32, (tm, tm), 0)
               < lax.broadcasted_iota(jnp.int32, (tm, tm), 1)).astype(BF16)
    before = jnp.dot(chosen, earlier, preferred_element_type=F32) + cnt_ref[:, 0:1]
    rank_ref[0:1, :] = jnp.sum(jnp.where(hit0, before, 0.0), axis=0, keepdims=True).astype(jnp.int32)
    rank_ref[1:2, :] = jnp.sum(jnp.where(hit1, before, 0.0), axis=0, keepdims=True).astype(jnp.int32)
    cnt_ref[...] = cnt_ref[...] + jnp.sum(chosen.astype(F32), axis=1, keepdims=True)


def moe_router(x, g, router_w, router_b, tm=ROW_TILE):
    n, d = x.shape
    ne = router_w.shape[1]
    out_shapes = (jax.ShapeDtypeStruct((TOP_K, n), jnp.int32),
                  jax.ShapeDtypeStruct((TOP_K, n), F32),
                  jax.ShapeDtypeStruct((TOP_K, n), jnp.int32),
                  jax.ShapeDtypeStruct((ne, 128), F32))
    tok_spec = pl.BlockSpec((TOP_K, tm), lambda i: (0, i))
    return pl.pallas_call(
        _router_kernel,
        out_shape=out_shapes,
        grid=(n // tm,),
        in_specs=[pl.BlockSpec((tm, d), lambda i: (i, 0)),
                  pl.BlockSpec((1, d), lambda i: (0, 0)),
                  pl.BlockSpec((ne, d), lambda i: (0, 0)),
                  pl.BlockSpec((ne, 1), lambda i: (0, 0))],
        out_specs=(tok_spec, tok_spec, tok_spec, pl.BlockSpec((ne, 128), lambda i: (0, 0))),
        compiler_params=_cparams(("arbitrary",)),
        name="moe_router",
    )(x, g.reshape(1, d), router_w.T, router_b.reshape(ne, 1))


ISSUE_UNROLL = 8


def _start_row_gather(src_hbm, idx_ref, idx_base, n_rows, buf, slot, row_base, sem):
    def body(jj, carry):
        for u in range(ISSUE_UNROLL):
            j = jj * ISSUE_UNROLL + u
            row = idx_ref[idx_base + j]
            pltpu.make_async_copy(src_hbm.at[pl.ds(row, 1), :],
                                  buf.at[slot, pl.ds(row_base + j, 1), :],
                                  sem.at[slot]).start(priority=u % 2)
        return carry
    lax.fori_loop(0, n_rows // ISSUE_UNROLL, body, 0)


def _wait_row_gather(src_hbm, buf, slot, sem):
    n_rows = buf.shape[1]
    pltpu.make_async_copy(src_hbm.at[pl.ds(0, n_rows), :], buf.at[slot], sem.at[slot]).wait()


def _dispatch_kernel(src_ref, x_hbm, g_ref, o_ref, buf, sem, *, tg):
    i = pl.program_id(0)
    nsteps = pl.num_programs(0)

    @pl.when(i == 0)
    def _():
        _start_row_gather(x_hbm, src_ref, 0, tg, buf, 0, 0, sem)

    @pl.when(i + 1 < nsteps)
    def _():
        _start_row_gather(x_hbm, src_ref, (i + 1) * tg, tg, buf, jnp.bitwise_and(i + 1, 1), 0, sem)

    slot = jnp.bitwise_and(i, 1)
    _wait_row_gather(x_hbm, buf, slot, sem)
    o_ref[...] = _rms(buf[slot], g_ref[...]).astype(o_ref.dtype)


def moe_dispatch(x, g, src, tg=256):
    n, d = x.shape
    n_slots = src.shape[0]
    kern = functools.partial(_dispatch_kernel, tg=tg)
    return pl.pallas_call(
        kern,
        out_shape=jax.ShapeDtypeStruct((n_slots, d), BF16),
        grid_spec=pltpu.PrefetchScalarGridSpec(
            num_scalar_prefetch=1,
            grid=(n_slots // tg,),
            in_specs=[pl.BlockSpec(memory_space=pl.ANY),
                      pl.BlockSpec((1, d), lambda i, s: (0, 0))],
            out_specs=pl.BlockSpec((tg, d), lambda i, s: (i, 0)),
            scratch_shapes=[pltpu.VMEM((2, tg, d), F32), pltpu.SemaphoreType.DMA((2,))],
        ),
        compiler_params=_cparams(("arbitrary",)),
        name="moe_dispatch",
    )(src, x, g.reshape(1, d))


def _combine_kernel(slot_ref, y_hbm, x_ref, w_ref, g_ref, o_ref, buf, sem, *, tc, final_norm):
    i = pl.program_id(0)
    nsteps = pl.num_programs(0)
    n_tok = nsteps * tc

    def fetch(step, slot):
        for k in range(TOP_K):
            _start_row_gather(y_hbm, slot_ref, k * n_tok + step * tc, tc, buf, slot, k * tc, sem)

    @pl.when(i == 0)
    def _():
        fetch(0, 0)

    @pl.when(i + 1 < nsteps)
    def _():
        fetch(i + 1, jnp.bitwise_and(i + 1, 1))

    slot = jnp.bitwise_and(i, 1)
    _wait_row_gather(y_hbm, buf, slot, sem)
    acc = x_ref[...]
    for k in range(TOP_K):
        acc = acc + w_ref[:, k:k + 1] * buf[slot, k * tc:(k + 1) * tc, :]
    if final_norm:
        acc = _rms(acc, g_ref[...])
    o_ref[...] = acc


def moe_combine(y, x, slot, wts, g, final_norm, tc=128):
    n, d = x.shape
    kern = functools.partial(_combine_kernel, tc=tc, final_norm=final_norm)
    return pl.pallas_call(
        kern,
        out_shape=jax.ShapeDtypeStruct((n, d), F32),
        grid_spec=pltpu.PrefetchScalarGridSpec(
            num_scalar_prefetch=1,
            grid=(n // tc,),
            in_specs=[pl.BlockSpec(memory_space=pl.ANY),
                      pl.BlockSpec((tc, d), lambda i, s: (i, 0)),
                      pl.BlockSpec((tc, TOP_K), lambda i, s: (i, 0)),
                      pl.BlockSpec((1, d), lambda i, s: (0, 0))],
            out_specs=pl.BlockSpec((tc, d), lambda i, s: (i, 0)),
            scratch_shapes=[pltpu.VMEM((2, TOP_K * tc, d), F32), pltpu.SemaphoreType.DMA((2,))],
        ),
        compiler_params=_cparams(("arbitrary",)),
        name="moe_combine",
    )(slot.reshape(-1), y, x, wts.T, g.reshape(1, d))


def token_mixer(x, batch, seq, layer, norm_g, w_in, g_sgu, w_s, b_s, w_pa, w_pb, w_o):
    n, d = x.shape
    aw = ATTN_HEADS * HEAD_DIM
    sw = SGU_GROUPS * SGU_GROUP_DIM
    tn = 1024
    h = rmsnorm(x, norm_g, BF16)
    q_scale = jnp.where(jnp.arange(w_in.shape[2]) < aw, HEAD_DIM ** -0.5, 1.0).astype(F32)
    z = ws_matmul([h], [w_in], [0], [], [], _ep_plain, BF16, tn, w_base=layer,
                  col_scale=q_scale, tm=DENSE_ROW_TILE, name="in_proj")
    attn = moba_attention(z.reshape(batch, seq, z.shape[1]), batch, seq).reshape(n, aw)
    sgu = spatial_gating(z, g_sgu, w_s, b_s, (3 * aw) // sw, (3 * aw + sw) // sw)
    gate_col = (3 * aw + 2 * sw) // tn
    merged = ws_matmul([attn, sgu], [w_pa, w_pb], [0, 1], [z, z],
                       [gate_col, gate_col + d // tn], _ep_merge, BF16, tn, w_base=layer,
                       tm=DENSE_ROW_TILE, name="branch_merge")
    return ws_matmul([merged], [w_o], [0], [x], [0], _ep_residual, F32, tn, w_base=layer,
                     tm=DENSE_ROW_TILE, name="out_proj")


def dense_ffn(x, norm_g, j, wg, wu, wd):
    h = rmsnorm(x, norm_g, BF16)
    act = ws_matmul([h], [wg, wu], [0, 0], [], [], _ep_swiglu, BF16, 512, w_base=j,
                    tm=DENSE_ROW_TILE, name="ffn_up")
    return ws_matmul([act], [wd], [0], [x], [0], _ep_residual, F32, 512, w_base=j,
                     tm=DENSE_ROW_TILE, name="ffn_down")


def moe_ffn(x, norm_g, j, router_w, router_b, wg, wu, wd, final_g, final_norm):
    n, d = x.shape
    ne = router_w.shape[1]
    tm = ROW_TILE
    eidx, wts, rank, cnt = moe_router(x, norm_g, router_w, router_b)
    counts = cnt[:, 0].astype(jnp.int32)
    padded = ((counts + tm - 1) // tm) * tm
    ends = jnp.cumsum(padded)
    starts = ends - padded
    group_start = jnp.sum(jnp.where(eidx[..., None] == jnp.arange(ne), starts, 0), axis=-1)
    slot = group_start + rank
    n_tiles = (TOP_K * n) // tm + ne
    tok = jnp.tile(jnp.arange(n, dtype=jnp.int32), TOP_K)
    src = (jnp.arange(n_tiles * tm, dtype=jnp.int32) % n).at[slot.reshape(-1)].set(tok)
    tile_row = jnp.arange(n_tiles, dtype=jnp.int32) * tm
    last_used = jnp.max(jnp.where(counts > 0, jnp.arange(ne, dtype=jnp.int32), 0))
    texp = jnp.minimum(jnp.sum(tile_row[:, None] >= ends[None, :], axis=1), last_used).astype(jnp.int32)
    group_end = jnp.sum(jnp.where(texp[:, None] == jnp.arange(ne), starts + counts, 0), axis=-1)
    tvalid = jnp.clip(group_end - tile_row, 0, tm).astype(jnp.int32)

    hs = moe_dispatch(x, norm_g, src)
    act = ws_matmul([hs], [wg, wu], [0, 0], [], [], _ep_swiglu, BF16, 512, texp=texp,
                    tvalid=tvalid, sub_rows=MOE_SUB_ROWS, w_base=j * ne, name="moe_up")
    y = ws_matmul([act], [wd], [0], [], [], _ep_plain, F32, 1024, texp=texp, tvalid=tvalid,
                  sub_rows=MOE_SUB_ROWS, w_base=j * ne, name="moe_down")
    return moe_combine(y, x, slot, wts, final_g, final_norm)


def kernel(x, mix_norm_g, w_in, sgu_norm_g, w_s, b_s, w_pa, w_pb, w_o, ffn_norm_g,
           dense_w_gate, dense_w_up, dense_w_down, router_w, router_b,
           expert_w_gate, expert_w_up, expert_w_down, final_norm_g):
    batch, seq, d = x.shape
    depth = mix_norm_g.shape[0]
    xf = x.reshape(batch * seq, d)
    merge_experts = lambda w: w.reshape((w.shape[0] * w.shape[1],) + w.shape[2:])
    ewg, ewu, ewd = (merge_experts(w) for w in (expert_w_gate, expert_w_up, expert_w_down))
    normed = False
    for i in range(depth):
        xf = token_mixer(xf, batch, seq, i, mix_norm_g[i], w_in, sgu_norm_g[i], w_s[i], b_s[i],
                         w_pa, w_pb, w_o)
        j = i // 2
        last = i == depth - 1
        if i % 2 == 0:
            xf = dense_ffn(xf, ffn_norm_g[i], j, dense_w_gate, dense_w_up, dense_w_down)
        else:
            xf = moe_ffn(xf, ffn_norm_g[i], j, router_w[j], router_b[j], ewg, ewu, ewd,
                         final_norm_g, last)
            normed = last
    if not normed:
        xf = rmsnorm(xf, final_norm_g, F32)
    return xf.reshape(batch, seq, d)
```

```python
import functools

import jax
import jax.numpy as jnp
from jax import lax
from jax.experimental import pallas as pl
from jax.experimental.pallas import tpu as pltpu

F32 = jnp.float32
BF16 = jnp.bfloat16

ATTN_HEADS = 8
HEAD_DIM = 128
MOBA_BLOCK = 256
MOBA_TOPK = 3
SGU_GROUPS = 8
SGU_GROUP_DIM = 128
SGU_CHUNK = 128
TOP_K = 2
NORM_EPS = 1e-6
LOG2_E = 1.4426950408889634

V7X_VMEM_BYTES = 64 * 1024 * 1024
VMEM_LIMIT_BYTES = V7X_VMEM_BYTES - 8 * 1024 * 1024

ROW_TILE = 512
DENSE_ROW_TILE = 1024
CAST_ROWS = 256
MOE_SUB_ROWS = 128
MOE_GROUP_ROWS = 1024
DISPATCH_ROWS = 256


def _cparams(sem):
    return pltpu.CompilerParams(dimension_semantics=sem, vmem_limit_bytes=VMEM_LIMIT_BYTES)


def _rms(x, g):
    ms = jnp.mean(x * x, axis=-1, keepdims=True)
    return x * lax.rsqrt(ms + NORM_EPS) * g


def _rmsnorm_kernel(x_ref, g_ref, o_ref):
    o_ref[...] = _rms(x_ref[...], g_ref[...]).astype(o_ref.dtype)


def rmsnorm(x, g, out_dtype, tm=ROW_TILE):
    n, d = x.shape
    return pl.pallas_call(
        _rmsnorm_kernel,
        out_shape=jax.ShapeDtypeStruct((n, d), out_dtype),
        grid=(n // tm,),
        in_specs=[pl.BlockSpec((tm, d), lambda i: (i, 0)),
                  pl.BlockSpec((1, d), lambda i: (0, 0))],
        out_specs=pl.BlockSpec((tm, d), lambda i: (i, 0)),
        compiler_params=_cparams(("parallel",)),
        name="rmsnorm",
    )(x, g.reshape(1, d))


def _ws_kernel(texp_ref, tnext_ref, tvalid_ref, *refs, n_lhs, w_lhs, n_extra, epilogue,
               sub_rows, tn, scaled):
    n_w = len(w_lhs)
    lhs_refs = refs[:n_lhs]
    w_hbm = refs[n_lhs:n_lhs + n_w]
    extra_refs = refs[n_lhs + n_w:n_lhs + n_w + n_extra]
    n_in = n_lhs + n_w + n_extra + int(scaled)
    scale_ref = refs[n_in - 1] if scaled else None
    out_ref = refs[n_in]
    scratch = refs[n_in + 1:]
    wf_refs = scratch[:n_w]
    wb_refs = scratch[n_w:-1]
    sem = scratch[-1]
    tm = out_ref.shape[0]
    shared_lhs = n_w > 1 and len(set(w_lhs)) == 1

    n = pl.program_id(0)
    r = pl.program_id(1)
    n_col_tiles = pl.num_programs(0)

    def weight_copy(i, index, col_tile):
        cols = pl.ds(pl.multiple_of(col_tile * tn, tn), tn)
        return pltpu.make_async_copy(w_hbm[i].at[index, :, cols], wf_refs[i], sem.at[i])

    def start_weights(index, col_tile):
        for i in range(n_w):
            weight_copy(i, index, col_tile).start()

    @pl.when(jnp.logical_and(n == 0, r == 0))
    def _():
        start_weights(texp_ref[0], 0)

    new_group = jnp.logical_or(r == 0, texp_ref[r] != texp_ref[jnp.maximum(r - 1, 0)])

    @pl.when(new_group)
    def _():
        for i in range(n_w):
            weight_copy(i, texp_ref[r], n).wait()
        for i, wf_ref in enumerate(wf_refs):
            wb_ref = wb_refs[0] if shared_lhs else wb_refs[i]
            cols = slice(i * tn, (i + 1) * tn) if shared_lhs else slice(None)
            k = wf_ref.shape[0]

            def cast_rows(c, carry, wf_ref=wf_ref, wb_ref=wb_ref, cols=cols):
                rows = pl.ds(pl.multiple_of(c * CAST_ROWS, CAST_ROWS), CAST_ROWS)
                w = wf_ref[rows, :]
                if scaled:
                    w = w * scale_ref[...]
                wb_ref[rows, cols] = w.astype(BF16)
                return carry

            lax.fori_loop(0, k // CAST_ROWS, cast_rows, 0)

        following = tnext_ref[r]

        @pl.when(following >= 0)
        def _():
            start_weights(following, n)

        @pl.when(jnp.logical_and(following < 0, n + 1 < n_col_tiles))
        def _():
            start_weights(texp_ref[0], n + 1)

    def compute(n_live):
        rows = slice(0, n_live)
        if n_live > 0 and shared_lhs:
            wide = jnp.dot(lhs_refs[w_lhs[0]][rows, :], wb_refs[0][...], preferred_element_type=F32)
            prods = [wide[:, i * tn:(i + 1) * tn] for i in range(n_w)]
        elif n_live > 0:
            prods = [jnp.dot(lhs_refs[w_lhs[i]][rows, :], wb_refs[i][...],
                             preferred_element_type=F32) for i in range(n_w)]
        if n_live > 0:
            out_ref[rows, :] = epilogue(prods, [e[rows, :] for e in extra_refs]).astype(out_ref.dtype)
        if n_live < tm:
            out_ref[n_live:tm, :] = jnp.zeros((tm - n_live, out_ref.shape[1]), out_ref.dtype)

    if sub_rows is None:
        compute(tm)
    else:
        live_chunks = (tvalid_ref[r] + (sub_rows - 1)) // sub_rows
        for c in range(tm // sub_rows + 1):
            pl.when(live_chunks == c)(functools.partial(compute, c * sub_rows))


def ws_matmul(lhs, ws, w_lhs, extras, extra_col_off, epilogue, out_dtype, tn, texp=None,
              tvalid=None, sub_rows=None, w_base=0, col_scale=None, tm=ROW_TILE,
              name="ws_matmul"):
    m = lhs[0].shape[0]
    n_total = ws[0].shape[2]
    n_rows = m // tm
    if texp is None:
        texp = jnp.zeros((n_rows,), jnp.int32)
    if tvalid is None:
        tvalid = jnp.full((n_rows,), tm, jnp.int32)
    no_next = jnp.iinfo(jnp.int32).max
    later = jnp.min(jnp.where(texp[None, :] > texp[:, None], texp[None, :], no_next), axis=1)
    tnext = jnp.where(later == no_next, -1, later + w_base).astype(jnp.int32)
    texp = texp + w_base
    in_specs = []
    for a in lhs:
        in_specs.append(pl.BlockSpec((tm, a.shape[1]), lambda n, r, *_: (r, 0)))
    for w in ws:
        in_specs.append(pl.BlockSpec(memory_space=pl.ANY))
    for off in extra_col_off:
        in_specs.append(pl.BlockSpec((tm, tn), lambda n, r, *_, off=off: (r, off + n)))
    scale_args = []
    if col_scale is not None:
        in_specs.append(pl.BlockSpec((1, tn), lambda n, r, *_: (0, n)))
        scale_args.append(col_scale.reshape(1, n_total))
    kern = functools.partial(_ws_kernel, n_lhs=len(lhs), w_lhs=tuple(w_lhs),
                             n_extra=len(extras), epilogue=epilogue, sub_rows=sub_rows, tn=tn,
                             scaled=col_scale is not None)
    shared_lhs = len(ws) > 1 and len(set(w_lhs)) == 1
    operands = ([pltpu.VMEM((ws[0].shape[1], len(ws) * tn), BF16)] if shared_lhs
                else [pltpu.VMEM((w.shape[1], tn), BF16) for w in ws])
    scratch = ([pltpu.VMEM((w.shape[1], tn), F32) for w in ws] + operands
               + [pltpu.SemaphoreType.DMA((len(ws),))])
    return pl.pallas_call(
        kern,
        out_shape=jax.ShapeDtypeStruct((m, n_total), out_dtype),
        grid_spec=pltpu.PrefetchScalarGridSpec(
            num_scalar_prefetch=3,
            grid=(n_total // tn, n_rows),
            in_specs=in_specs,
            out_specs=pl.BlockSpec((tm, tn), lambda n, r, *_: (r, n)),
            scratch_shapes=scratch,
        ),
        compiler_params=_cparams(("arbitrary", "arbitrary")),
        name=name,
    )(texp, tnext, tvalid, *lhs, *ws, *extras, *scale_args)


def _ep_plain(prods, extras):
    return prods[0]


def _ep_residual(prods, extras):
    return extras[0] + prods[0]


def _ep_swiglu(prods, extras):
    g, u = prods
    return (g * jax.nn.sigmoid(g)) * u


def _ep_merge(prods, extras):
    ga, gb = extras
    return (jax.nn.sigmoid(ga.astype(F32)) * prods[0]
            + jax.nn.sigmoid(gb.astype(F32)) * prods[1])


def _attn_kernel(q_ref, k_ref, v_ref, o_ref, *, nb, blk, topk):
    seq = k_ref.shape[0]
    neg_inf = jnp.float32(-jnp.inf)
    avg = jnp.where(lax.broadcasted_iota(jnp.int32, (nb, seq), 1) // blk
                    == lax.broadcasted_iota(jnp.int32, (nb, seq), 0), 1.0 / blk, 0.0).astype(BF16)
    k_mean = jnp.dot(avg, k_ref[...], preferred_element_type=F32)
    k_mean_hi = k_mean.astype(BF16)
    k_mean_lo = (k_mean - k_mean_hi.astype(F32)).astype(BF16)
    blk_id = lax.broadcasted_iota(jnp.int32, (blk, nb), 1)
    causal = (lax.broadcasted_iota(jnp.int32, (blk, blk), 1)
              <= lax.broadcasted_iota(jnp.int32, (blk, blk), 0))
    nt_dims = (((1,), (1,)), ((), ()))

    for n in range(nb):
        qn = q_ref[n * blk:(n + 1) * blk, :]
        nk = (n + 1) * blk
        s = lax.dot_general(qn, k_ref[0:nk, :], nt_dims, preferred_element_type=F32)
        past = [s[:, j * blk:(j + 1) * blk] for j in range(n)]
        if n > topk:
            gate = (lax.dot_general(qn, k_mean_hi, nt_dims, preferred_element_type=F32)
                    + lax.dot_general(qn, k_mean_lo, nt_dims, preferred_element_type=F32))
            beaten = jnp.zeros((blk, nb), F32)
            for j in range(n):
                gj = gate[:, j:j + 1]
                wins = jnp.logical_or(gj > gate, jnp.logical_and(gj == gate, j < blk_id))
                beaten = beaten + wins.astype(F32)
            bias = jnp.where(beaten < topk, 0.0, neg_inf)
            past = [past[j] + bias[:, j:j + 1] for j in range(n)]
        own = jnp.where(causal, s[:, n * blk:nk], neg_inf)
        s = jnp.concatenate(past + [own], axis=1)
        m = jnp.max(s, axis=-1, keepdims=True)
        p = jnp.exp2(s - m)
        l = jnp.sum(p, axis=-1, keepdims=True)
        o = jnp.dot(p.astype(BF16), v_ref[0:nk, :], preferred_element_type=F32)
        o_ref[n * blk:(n + 1) * blk, :] = (o / l).astype(o_ref.dtype)


def moba_attention(z, batch, seq):
    nb = seq // MOBA_BLOCK
    kern = functools.partial(_attn_kernel, nb=nb, blk=MOBA_BLOCK, topk=MOBA_TOPK)
    blockspec = lambda off: pl.BlockSpec((None, seq, HEAD_DIM), lambda b, h: (b, 0, off + h))
    return pl.pallas_call(
        kern,
        out_shape=jax.ShapeDtypeStruct((batch, seq, ATTN_HEADS * HEAD_DIM), BF16),
        grid=(batch, ATTN_HEADS),
        in_specs=[blockspec(0), blockspec(ATTN_HEADS), blockspec(2 * ATTN_HEADS)],
        out_specs=pl.BlockSpec((None, seq, HEAD_DIM), lambda b, h: (b, 0, h)),
        compiler_params=_cparams(("parallel", "parallel")),
        name="moba_attention",
    )(z, z, z)


def _gelu_tanh(x):
    return 0.5 * x * (1.0 + jnp.tanh(0.7978845608028654 * (x + 0.044715 * (x * x * x))))


def _sgu_kernel(u_ref, v_ref, g_ref, w_ref, bt_ref, o_ref, *, chunk, groups, gd):
    tm = u_ref.shape[0]
    u = _gelu_tanh(u_ref[...].astype(F32))
    v = _gelu_tanh(v_ref[...].astype(F32))
    mu = jnp.mean(v, axis=-1, keepdims=True)
    vc = v - mu
    var = jnp.mean(vc * vc, axis=-1, keepdims=True)
    vn = (vc * lax.rsqrt(var + NORM_EPS) * g_ref[...]).astype(BF16)
    t_i = lax.broadcasted_iota(jnp.int32, (chunk, chunk), 0)
    s_i = lax.broadcasted_iota(jnp.int32, (chunk, chunk), 1)
    lower = s_i <= t_i
    for g in range(groups):
        w = jnp.where(lower, w_ref[g], 0.0).astype(BF16)
        bias = bt_ref[:, g:g + 1]
        cols = slice(g * gd, (g + 1) * gd)
        for c in range(tm // chunk):
            rows = slice(c * chunk, (c + 1) * chunk)
            mixed = jnp.dot(w, vn[rows, cols], preferred_element_type=F32) + bias
            o_ref[rows, cols] = (u[rows, cols] * mixed).astype(o_ref.dtype)


def spatial_gating(z, g_sgu, w_s, b_s, col_u, col_v, tm=256):
    n = z.shape[0]
    width = SGU_GROUPS * SGU_GROUP_DIM
    kern = functools.partial(_sgu_kernel, chunk=SGU_CHUNK, groups=SGU_GROUPS, gd=SGU_GROUP_DIM)
    return pl.pallas_call(
        kern,
        out_shape=jax.ShapeDtypeStruct((n, width), BF16),
        grid=(n // tm,),
        in_specs=[pl.BlockSpec((tm, width), lambda i: (i, col_u)),
                  pl.BlockSpec((tm, width), lambda i: (i, col_v)),
                  pl.BlockSpec((1, width), lambda i: (0, 0)),
                  pl.BlockSpec(w_s.shape, lambda i: (0, 0, 0)),
                  pl.BlockSpec((SGU_CHUNK, SGU_GROUPS), lambda i: (0, 0))],
        out_specs=pl.BlockSpec((tm, width), lambda i: (i, 0)),
        compiler_params=_cparams(("parallel",)),
        name="spatial_gating",
    )(z, z, g_sgu.reshape(1, width), w_s, b_s.T)


def _router_kernel(x_ref, g_ref, rwt_ref, rb_ref, eidx_ref, wts_ref, rank_ref, cnt_ref):
    i = pl.program_id(0)
    ne = rwt_ref.shape[0]
    tm = x_ref.shape[0]

    @pl.when(i == 0)
    def _():
        cnt_ref[...] = jnp.zeros_like(cnt_ref)

    h = _rms(x_ref[...], g_ref[...])
    h_hi = h.astype(BF16)
    h_lo = (h - h_hi.astype(F32)).astype(BF16)
    rw = rwt_ref[...]
    rw_hi = rw.astype(BF16)
    rw_hi_f32 = rw_hi.astype(F32)
    rw_parts = jnp.concatenate([rw_hi_f32, rw - rw_hi_f32], axis=0).astype(BF16)
    nt_dims = (((1,), (1,)), ((), ()))
    both = lax.dot_general(rw_parts, h_hi, nt_dims, preferred_element_type=F32)
    logits = (both[:ne] + both[ne:]
              + lax.dot_general(rw_hi, h_lo, nt_dims, preferred_element_type=F32)
              + rb_ref[...])
    row = lax.broadcasted_iota(jnp.int32, (ne, tm), 0).astype(F32)
    neg_inf = jnp.float32(-jnp.inf)
    m0 = jnp.max(logits, axis=0, keepdims=True)
    i0 = jnp.min(jnp.where(logits == m0, row, float(ne)), axis=0, keepdims=True)
    rest = jnp.where(row == i0, neg_inf, logits)
    m1 = jnp.max(rest, axis=0, keepdims=True)
    i1 = jnp.min(jnp.where(jnp.logical_and(rest == m1, row != i0), row, float(ne)),
                 axis=0, keepdims=True)
    e1 = jnp.exp(m1 - m0)
    denom = 1.0 + e1
    eidx_ref[0:1, :] = i0.astype(jnp.int32)
    eidx_ref[1:2, :] = i1.astype(jnp.int32)
    wts_ref[0:1, :] = 1.0 / denom
    wts_ref[1:2, :] = e1 / denom

    hit0 = row == i0
    hit1 = row == i1
    chosen = jnp.logical_or(hit0, hit1).astype(BF16)
    earlier = (lax.broadcasted_iota(jnp.int32, (tm, tm), 0)
               < lax.broadcasted_iota(jnp.int32, (tm, tm), 1)).astype(BF16)
    before = jnp.dot(chosen, earlier, preferred_element_type=F32) + cnt_ref[:, 0:1]
    rank_ref[0:1, :] = jnp.sum(jnp.where(hit0, before, 0.0), axis=0, keepdims=True).astype(jnp.int32)
    rank_ref[1:2, :] = jnp.sum(jnp.where(hit1, before, 0.0), axis=0, keepdims=True).astype(jnp.int32)
    cnt_ref[...] = cnt_ref[...] + jnp.sum(chosen.astype(F32), axis=1, keepdims=True)


def moe_router(x, g, router_w, router_b, tm=ROW_TILE):
    n, d = x.shape
    ne = router_w.shape[1]
    out_shapes = (jax.ShapeDtypeStruct((TOP_K, n), jnp.int32),
                  jax.ShapeDtypeStruct((TOP_K, n), F32),
                  jax.ShapeDtypeStruct((TOP_K, n), jnp.int32),
                  jax.ShapeDtypeStruct((ne, 128), F32))
    tok_spec = pl.BlockSpec((TOP_K, tm), lambda i: (0, i))
    return pl.pallas_call(
        _router_kernel,
        out_shape=out_shapes,
        grid=(n // tm,),
        in_specs=[pl.BlockSpec((tm, d), lambda i: (i, 0)),
                  pl.BlockSpec((1, d), lambda i: (0, 0)),
                  pl.BlockSpec((ne, d), lambda i: (0, 0)),
                  pl.BlockSpec((ne, 1), lambda i: (0, 0))],
        out_specs=(tok_spec, tok_spec, tok_spec, pl.BlockSpec((ne, 128), lambda i: (0, 0))),
        compiler_params=_cparams(("arbitrary",)),
        name="moe_router",
    )(x, g.reshape(1, d), router_w.T, router_b.reshape(ne, 1))


ISSUE_UNROLL = 8


def _start_row_gather(src_hbm, idx_ref, idx_base, n_groups, buf, slot, row_base, sem):
    def body(jj, carry):
        for u in range(ISSUE_UNROLL):
            j = jj * ISSUE_UNROLL + u
            row = idx_ref[idx_base + j]
            pltpu.make_async_copy(src_hbm.at[pl.ds(row, 1), :],
                                  buf.at[slot, pl.ds(row_base + j, 1), :], sem.at[slot]).start()
        return carry
    lax.fori_loop(0, n_groups, body, 0)


def _wait_row_gather(src_hbm, n_groups, buf, slot, sem):
    def body(jj, carry):
        pltpu.make_async_copy(src_hbm.at[pl.ds(0, ISSUE_UNROLL), :],
                              buf.at[slot, pl.ds(0, ISSUE_UNROLL), :], sem.at[slot]).wait()
        return carry
    lax.fori_loop(0, n_groups, body, 0)


def _dispatch_kernel(src_ref, nvalid_ref, x_hbm, g_ref, o_ref, buf, sem, *, tg):
    i = pl.program_id(0)
    nsteps = pl.num_programs(0)

    def groups(step):
        return lax.shift_right_logical(nvalid_ref[step] + (ISSUE_UNROLL - 1), 3)

    @pl.when(i == 0)
    def _():
        buf[...] = jnp.zeros_like(buf)
        _start_row_gather(x_hbm, src_ref, 0, groups(0), buf, 0, 0, sem)

    @pl.when(i + 1 < nsteps)
    def _():
        _start_row_gather(x_hbm, src_ref, (i + 1) * tg, groups(i + 1), buf,
                          jnp.bitwise_and(i + 1, 1), 0, sem)

    slot = jnp.bitwise_and(i, 1)
    _wait_row_gather(x_hbm, groups(i), buf, slot, sem)
    o_ref[...] = _rms(buf[slot], g_ref[...]).astype(o_ref.dtype)


def moe_dispatch(x, g, src, nvalid, tg):
    n, d = x.shape
    n_slots = src.shape[0]
    kern = functools.partial(_dispatch_kernel, tg=tg)
    return pl.pallas_call(
        kern,
        out_shape=jax.ShapeDtypeStruct((n_slots, d), BF16),
        grid_spec=pltpu.PrefetchScalarGridSpec(
            num_scalar_prefetch=2,
            grid=(n_slots // tg,),
            in_specs=[pl.BlockSpec(memory_space=pl.ANY),
                      pl.BlockSpec((1, d), lambda i, *_: (0, 0))],
            out_specs=pl.BlockSpec((tg, d), lambda i, *_: (i, 0)),
            scratch_shapes=[pltpu.VMEM((2, tg, d), F32), pltpu.SemaphoreType.DMA((2,))],
        ),
        compiler_params=_cparams(("arbitrary",)),
        name="moe_dispatch",
    )(src, nvalid, x, g.reshape(1, d))


def _combine_kernel(slot_ref, y_hbm, x_ref, w_ref, g_ref, o_ref, buf, sem, *, tc, final_norm):
    i = pl.program_id(0)
    nsteps = pl.num_programs(0)
    n_tok = nsteps * tc

    def fetch(step, slot):
        for k in range(TOP_K):
            _start_row_gather(y_hbm, slot_ref, k * n_tok + step * tc, tc // ISSUE_UNROLL, buf,
                              slot, k * tc, sem)

    @pl.when(i == 0)
    def _():
        fetch(0, 0)

    @pl.when(i + 1 < nsteps)
    def _():
        fetch(i + 1, jnp.bitwise_and(i + 1, 1))

    slot = jnp.bitwise_and(i, 1)
    _wait_row_gather(y_hbm, TOP_K * tc // ISSUE_UNROLL, buf, slot, sem)
    acc = x_ref[...]
    for k in range(TOP_K):
        acc = acc + w_ref[:, k:k + 1] * buf[slot, k * tc:(k + 1) * tc, :]
    if final_norm:
        acc = _rms(acc, g_ref[...])
    o_ref[...] = acc


def moe_combine(y, x, slot, wts, g, final_norm, tc=128):
    n, d = x.shape
    kern = functools.partial(_combine_kernel, tc=tc, final_norm=final_norm)
    return pl.pallas_call(
        kern,
        out_shape=jax.ShapeDtypeStruct((n, d), F32),
        grid_spec=pltpu.PrefetchScalarGridSpec(
            num_scalar_prefetch=1,
            grid=(n // tc,),
            in_specs=[pl.BlockSpec(memory_space=pl.ANY),
                      pl.BlockSpec((tc, d), lambda i, s: (i, 0)),
                      pl.BlockSpec((tc, TOP_K), lambda i, s: (i, 0)),
                      pl.BlockSpec((1, d), lambda i, s: (0, 0))],
            out_specs=pl.BlockSpec((tc, d), lambda i, s: (i, 0)),
            scratch_shapes=[pltpu.VMEM((2, TOP_K * tc, d), F32), pltpu.SemaphoreType.DMA((2,))],
        ),
        compiler_params=_cparams(("arbitrary",)),
        name="moe_combine",
    )(slot.reshape(-1), y, x, wts.T, g.reshape(1, d))


def token_mixer(x, batch, seq, layer, norm_g, w_in, g_sgu, w_s, b_s, w_pa, w_pb, w_o):
    n, d = x.shape
    aw = ATTN_HEADS * HEAD_DIM
    sw = SGU_GROUPS * SGU_GROUP_DIM
    tn = 1024
    h = rmsnorm(x, norm_g, BF16)
    q_scale = jnp.where(jnp.arange(w_in.shape[2]) < aw, HEAD_DIM ** -0.5 * LOG2_E, 1.0).astype(F32)
    z = ws_matmul([h], [w_in], [0], [], [], _ep_plain, BF16, tn, w_base=layer,
                  col_scale=q_scale, tm=DENSE_ROW_TILE, name="in_proj")
    attn = moba_attention(z.reshape(batch, seq, z.shape[1]), batch, seq).reshape(n, aw)
    sgu = spatial_gating(z, g_sgu, w_s, b_s, (3 * aw) // sw, (3 * aw + sw) // sw)
    gate_col = (3 * aw + 2 * sw) // tn
    merged = ws_matmul([attn, sgu], [w_pa, w_pb], [0, 1], [z, z],
                       [gate_col, gate_col + d // tn], _ep_merge, BF16, tn, w_base=layer,
                       tm=DENSE_ROW_TILE, name="branch_merge")
    return ws_matmul([merged], [w_o], [0], [x], [0], _ep_residual, F32, tn, w_base=layer,
                     tm=DENSE_ROW_TILE, name="out_proj")


def dense_ffn(x, norm_g, j, wg, wu, wd):
    h = rmsnorm(x, norm_g, BF16)
    act = ws_matmul([h], [wg, wu], [0, 0], [], [], _ep_swiglu, BF16, 512, w_base=j,
                    tm=DENSE_ROW_TILE, name="ffn_up")
    return ws_matmul([act], [wd], [0], [x], [0], _ep_residual, F32, 512, w_base=j,
                     tm=DENSE_ROW_TILE, name="ffn_down")


def moe_ffn(x, norm_g, j, router_w, router_b, wg, wu, wd, final_g, final_norm):
    n, d = x.shape
    ne = router_w.shape[1]
    gran = MOE_GROUP_ROWS
    eidx, wts, rank, cnt = moe_router(x, norm_g, router_w, router_b)
    counts = cnt[:, 0].astype(jnp.int32)
    padded = ((counts + gran - 1) // gran) * gran
    ends = jnp.cumsum(padded)
    starts = ends - padded
    experts = jnp.arange(ne, dtype=jnp.int32)
    group_start = jnp.sum(jnp.where(eidx[..., None] == experts, starts, 0), axis=-1)
    slot = group_start + rank
    n_slots = TOP_K * n + ne * gran
    tok = jnp.tile(jnp.arange(n, dtype=jnp.int32), TOP_K)
    src = (jnp.arange(n_slots, dtype=jnp.int32) % n).at[slot.reshape(-1)].set(tok)
    last_used = jnp.max(jnp.where(counts > 0, experts, 0))

    def tiling(rows_per_tile):
        first_row = jnp.arange(n_slots // rows_per_tile, dtype=jnp.int32) * rows_per_tile
        texp = jnp.minimum(jnp.sum(first_row[:, None] >= ends[None, :], axis=1), last_used)
        group_end = jnp.sum(jnp.where(texp[:, None] == experts, starts + counts, 0), axis=-1)
        return texp.astype(jnp.int32), jnp.clip(group_end - first_row, 0, rows_per_tile).astype(jnp.int32)

    _, gather_valid = tiling(DISPATCH_ROWS)
    up_texp, up_valid = tiling(gran)
    down_texp, down_valid = tiling(ROW_TILE)

    hs = moe_dispatch(x, norm_g, src, gather_valid, DISPATCH_ROWS)
    act = ws_matmul([hs], [wg, wu], [0, 0], [], [], _ep_swiglu, BF16, 512, texp=up_texp,
                    tvalid=up_valid, sub_rows=2 * MOE_SUB_ROWS, w_base=j * ne, tm=gran, name="moe_up")
    y = ws_matmul([act], [wd], [0], [], [], _ep_plain, F32, 1024, texp=down_texp, tvalid=down_valid,
                  sub_rows=MOE_SUB_ROWS, w_base=j * ne, name="moe_down")
    return moe_combine(y, x, slot, wts, final_g, final_norm)


def kernel(x, mix_norm_g, w_in, sgu_norm_g, w_s, b_s, w_pa, w_pb, w_o, ffn_norm_g,
           dense_w_gate, dense_w_up, dense_w_down, router_w, router_b,
           expert_w_gate, expert_w_up, expert_w_down, final_norm_g):
    batch, seq, d = x.shape
    depth = mix_norm_g.shape[0]
    xf = x.reshape(batch * seq, d)
    merge_experts = lambda w: w.reshape((w.shape[0] * w.shape[1],) + w.shape[2:])
    ewg, ewu, ewd = (merge_experts(w) for w in (expert_w_gate, expert_w_up, expert_w_down))
    normed = False
    for i in range(depth):
        xf = token_mixer(xf, batch, seq, i, mix_norm_g[i], w_in, sgu_norm_g[i], w_s[i], b_s[i],
                         w_pa, w_pb, w_o)
        j = i // 2
        last = i == depth - 1
        if i % 2 == 0:
            xf = dense_ffn(xf, ffn_norm_g[i], j, dense_w_gate, dense_w_up, dense_w_down)
        else:
            xf = moe_ffn(xf, ffn_norm_g[i], j, router_w[j], router_b[j], ewg, ewu, ewd,
                         final_norm_g, last)
            normed = last
    if not normed:
        xf = rmsnorm(xf, final_norm_g, F32)
    return xf.reshape(batch, seq, d)
```

```python
import functools

import jax
import jax.numpy as jnp
from jax import lax
from jax.experimental import pallas as pl
from jax.experimental.pallas import tpu as pltpu

F32 = jnp.float32
BF16 = jnp.bfloat16

ATTN_HEADS = 8
HEAD_DIM = 128
MOBA_BLOCK = 256
MOBA_TOPK = 3
SGU_GROUPS = 8
SGU_GROUP_DIM = 128
SGU_CHUNK = 128
TOP_K = 2
NORM_EPS = 1e-6
LOG2_E = 1.4426950408889634

V7X_VMEM_BYTES = 64 * 1024 * 1024
VMEM_LIMIT_BYTES = V7X_VMEM_BYTES - 8 * 1024 * 1024

ROW_TILE = 512
DENSE_ROW_TILE = 1024
CAST_ROWS = 256
MOE_SUB_ROWS = 128
SOFTMAX_LAG = 2
VALUES_LAG = 1


def _cparams(sem):
    return pltpu.CompilerParams(dimension_semantics=sem, vmem_limit_bytes=VMEM_LIMIT_BYTES)


def _rms(x, g):
    ms = jnp.mean(x * x, axis=-1, keepdims=True)
    return x * lax.rsqrt(ms + NORM_EPS) * g


def _rmsnorm_kernel(x_ref, g_ref, o_ref):
    o_ref[...] = _rms(x_ref[...], g_ref[...]).astype(o_ref.dtype)


def rmsnorm(x, g, out_dtype, tm=ROW_TILE):
    n, d = x.shape
    return pl.pallas_call(
        _rmsnorm_kernel,
        out_shape=jax.ShapeDtypeStruct((n, d), out_dtype),
        grid=(n // tm,),
        in_specs=[pl.BlockSpec((tm, d), lambda i: (i, 0)),
                  pl.BlockSpec((1, d), lambda i: (0, 0))],
        out_specs=pl.BlockSpec((tm, d), lambda i: (i, 0)),
        compiler_params=_cparams(("parallel",)),
        name="rmsnorm",
    )(x, g.reshape(1, d))


def _ws_kernel(texp_ref, tnext_ref, tvalid_ref, *refs, n_lhs, w_lhs, n_extra, epilogue,
               sub_rows, tn, scaled):
    n_w = len(w_lhs)
    lhs_refs = refs[:n_lhs]
    w_hbm = refs[n_lhs:n_lhs + n_w]
    extra_refs = refs[n_lhs + n_w:n_lhs + n_w + n_extra]
    n_in = n_lhs + n_w + n_extra + int(scaled)
    scale_ref = refs[n_in - 1] if scaled else None
    out_ref = refs[n_in]
    scratch = refs[n_in + 1:]
    wf_refs = scratch[:n_w]
    wb_refs = scratch[n_w:2 * n_w]
    sem = scratch[2 * n_w]
    tm = out_ref.shape[0]

    n = pl.program_id(0)
    r = pl.program_id(1)
    n_col_tiles = pl.num_programs(0)

    def weight_copy(i, index, col_tile):
        cols = pl.ds(pl.multiple_of(col_tile * tn, tn), tn)
        return pltpu.make_async_copy(w_hbm[i].at[index, :, cols], wf_refs[i], sem.at[i])

    def start_weights(index, col_tile):
        for i in range(n_w):
            weight_copy(i, index, col_tile).start(priority=1)

    @pl.when(jnp.logical_and(n == 0, r == 0))
    def _():
        start_weights(texp_ref[0], 0)

    new_group = jnp.logical_or(r == 0, texp_ref[r] != texp_ref[jnp.maximum(r - 1, 0)])

    @pl.when(new_group)
    def _():
        for i in range(n_w):
            weight_copy(i, texp_ref[r], n).wait()
        for wf_ref, wb_ref in zip(wf_refs, wb_refs):
            k = wb_ref.shape[0]

            def cast_rows(c, carry, wf_ref=wf_ref, wb_ref=wb_ref):
                rows = pl.ds(pl.multiple_of(c * CAST_ROWS, CAST_ROWS), CAST_ROWS)
                w = wf_ref[rows, :]
                if scaled:
                    w = w * scale_ref[...]
                wb_ref[rows, :] = w.astype(BF16)
                return carry

            lax.fori_loop(0, k // CAST_ROWS, cast_rows, 0)

        following = tnext_ref[r]

        @pl.when(following >= 0)
        def _():
            start_weights(following, n)

        @pl.when(jnp.logical_and(following < 0, n + 1 < n_col_tiles))
        def _():
            start_weights(texp_ref[0], n + 1)

    def compute(n_live):
        rows = slice(0, n_live)
        if n_live > 0:
            prods = [jnp.dot(lhs_refs[w_lhs[i]][rows, :], wb_refs[i][...],
                             preferred_element_type=F32) for i in range(n_w)]
            out_ref[rows, :] = epilogue(prods, [e[rows, :] for e in extra_refs]).astype(out_ref.dtype)
        if n_live < tm:
            out_ref[n_live:tm, :] = jnp.zeros((tm - n_live, out_ref.shape[1]), out_ref.dtype)

    if sub_rows is None:
        compute(tm)
    else:
        live_chunks = (tvalid_ref[r] + (sub_rows - 1)) // sub_rows
        for c in range(tm // sub_rows + 1):
            pl.when(live_chunks == c)(functools.partial(compute, c * sub_rows))


def ws_matmul(lhs, ws, w_lhs, extras, extra_col_off, epilogue, out_dtype, tn, texp=None,
              tvalid=None, sub_rows=None, w_base=0, col_scale=None, tm=ROW_TILE,
              name="ws_matmul"):
    m = lhs[0].shape[0]
    n_total = ws[0].shape[2]
    n_rows = m // tm
    if texp is None:
        texp = jnp.zeros((n_rows,), jnp.int32)
    if tvalid is None:
        tvalid = jnp.full((n_rows,), tm, jnp.int32)
    no_next = jnp.iinfo(jnp.int32).max
    later = jnp.min(jnp.where(texp[None, :] > texp[:, None], texp[None, :], no_next), axis=1)
    tnext = jnp.where(later == no_next, -1, later + w_base).astype(jnp.int32)
    texp = texp + w_base
    in_specs = []
    for a in lhs:
        in_specs.append(pl.BlockSpec((tm, a.shape[1]), lambda n, r, *_: (r, 0)))
    for w in ws:
        in_specs.append(pl.BlockSpec(memory_space=pl.ANY))
    for off in extra_col_off:
        in_specs.append(pl.BlockSpec((tm, tn), lambda n, r, *_, off=off: (r, off + n)))
    scale_args = []
    if col_scale is not None:
        in_specs.append(pl.BlockSpec((1, tn), lambda n, r, *_: (0, n)))
        scale_args.append(col_scale.reshape(1, n_total))
    kern = functools.partial(_ws_kernel, n_lhs=len(lhs), w_lhs=tuple(w_lhs),
                             n_extra=len(extras), epilogue=epilogue, sub_rows=sub_rows, tn=tn,
                             scaled=col_scale is not None)
    scratch = ([pltpu.VMEM((w.shape[1], tn), F32) for w in ws]
               + [pltpu.VMEM((w.shape[1], tn), BF16) for w in ws]
               + [pltpu.SemaphoreType.DMA((len(ws),))])
    return pl.pallas_call(
        kern,
        out_shape=jax.ShapeDtypeStruct((m, n_total), out_dtype),
        grid_spec=pltpu.PrefetchScalarGridSpec(
            num_scalar_prefetch=3,
            grid=(n_total // tn, n_rows),
            in_specs=in_specs,
            out_specs=pl.BlockSpec((tm, tn), lambda n, r, *_: (r, n)),
            scratch_shapes=scratch,
        ),
        compiler_params=_cparams(("arbitrary", "arbitrary")),
        name=name,
    )(texp, tnext, tvalid, *lhs, *ws, *extras, *scale_args)


def _ep_plain(prods, extras):
    return prods[0]


def _ep_residual(prods, extras):
    return extras[0] + prods[0]


def _ep_swiglu(prods, extras):
    g, u = prods
    return (g * jax.nn.sigmoid(g)) * u


def _ep_merge(prods, extras):
    ga, gb = extras
    return (jax.nn.sigmoid(ga.astype(F32)) * prods[0]
            + jax.nn.sigmoid(gb.astype(F32)) * prods[1])


def _attn_kernel(q_ref, k_ref, v_ref, o_ref, *, nb, blk, topk):
    seq = k_ref.shape[0]
    neg_inf = jnp.float32(-jnp.inf)
    avg = jnp.where(lax.broadcasted_iota(jnp.int32, (nb, seq), 1) // blk
                    == lax.broadcasted_iota(jnp.int32, (nb, seq), 0), 1.0 / blk, 0.0).astype(BF16)
    k_mean = jnp.dot(avg, k_ref[...], preferred_element_type=F32)
    k_mean_hi = k_mean.astype(BF16).astype(F32)
    k_mean_parts = jnp.concatenate([k_mean_hi, k_mean - k_mean_hi], axis=0).astype(BF16)
    blk_id = lax.broadcasted_iota(jnp.int32, (nb, blk), 0)
    causal = (lax.broadcasted_iota(jnp.int32, (blk, blk), 0)
              <= lax.broadcasted_iota(jnp.int32, (blk, blk), 1))
    nt_dims = (((1,), (1,)), ((), ()))
    v_t = v_ref[...].T

    def scores(n):
        qn = q_ref[n * blk:(n + 1) * blk, :]
        s = lax.dot_general(k_ref[0:(n + 1) * blk, :], qn, nt_dims, preferred_element_type=F32)
        gate_parts = None
        if n > topk:
            gate_parts = lax.dot_general(k_mean_parts, qn, nt_dims, preferred_element_type=F32)
        return s, gate_parts

    def masked_softmax(n, s, gate_parts):
        past = [s[j * blk:(j + 1) * blk, :] for j in range(n)]
        if n > topk:
            gate = gate_parts[:nb] + gate_parts[nb:]
            beaten = jnp.zeros((nb, blk), F32)
            for j in range(n):
                gj = gate[j:j + 1, :]
                wins = jnp.logical_or(gj > gate, jnp.logical_and(gj == gate, j < blk_id))
                beaten = beaten + wins.astype(F32)
            bias = jnp.where(beaten < topk, 0.0, neg_inf)
            past = [past[j] + bias[j:j + 1, :] for j in range(n)]
        own = jnp.where(causal, s[n * blk:(n + 1) * blk, :], neg_inf)
        s = jnp.concatenate(past + [own], axis=0)
        m = jnp.max(s, axis=0, keepdims=True)
        p = jnp.exp2(s - m)
        return p.astype(BF16), jnp.sum(p, axis=0, keepdims=True)

    def weighted_values(n, p, l):
        o_t = jnp.dot(v_t[:, 0:(n + 1) * blk], p, preferred_element_type=F32)
        o_ref[n * blk:(n + 1) * blk, :] = (o_t / l).T.astype(o_ref.dtype)

    order = list(range(nb - 1, -1, -1))
    s_out, p_out = {}, {}
    for t in range(nb + SOFTMAX_LAG + VALUES_LAG):
        if t < nb:
            s_out[order[t]] = scores(order[t])
        if 0 <= t - SOFTMAX_LAG < nb:
            n = order[t - SOFTMAX_LAG]
            p_out[n] = masked_softmax(n, *s_out.pop(n))
        if 0 <= t - SOFTMAX_LAG - VALUES_LAG < nb:
            n = order[t - SOFTMAX_LAG - VALUES_LAG]
            weighted_values(n, *p_out.pop(n))


def moba_attention(z, batch, seq):
    nb = seq // MOBA_BLOCK
    kern = functools.partial(_attn_kernel, nb=nb, blk=MOBA_BLOCK, topk=MOBA_TOPK)
    blockspec = lambda off: pl.BlockSpec((None, seq, HEAD_DIM), lambda b, h: (b, 0, off + h))
    return pl.pallas_call(
        kern,
        out_shape=jax.ShapeDtypeStruct((batch, seq, ATTN_HEADS * HEAD_DIM), BF16),
        grid=(batch, ATTN_HEADS),
        in_specs=[blockspec(0), blockspec(ATTN_HEADS), blockspec(2 * ATTN_HEADS)],
        out_specs=pl.BlockSpec((None, seq, HEAD_DIM), lambda b, h: (b, 0, h)),
        compiler_params=_cparams(("parallel", "parallel")),
        name="moba_attention",
    )(z, z, z)


def _gelu_tanh(x):
    return 0.5 * x * (1.0 + jnp.tanh(0.7978845608028654 * (x + 0.044715 * (x * x * x))))


def _sgu_kernel(u_ref, v_ref, g_ref, w_ref, bt_ref, o_ref, *, chunk, groups, gd):
    tm = u_ref.shape[0]
    u = _gelu_tanh(u_ref[...].astype(F32))
    v = _gelu_tanh(v_ref[...].astype(F32))
    mu = jnp.mean(v, axis=-1, keepdims=True)
    vc = v - mu
    var = jnp.mean(vc * vc, axis=-1, keepdims=True)
    vn = (vc * lax.rsqrt(var + NORM_EPS) * g_ref[...]).astype(BF16)
    t_i = lax.broadcasted_iota(jnp.int32, (chunk, chunk), 0)
    s_i = lax.broadcasted_iota(jnp.int32, (chunk, chunk), 1)
    lower = s_i <= t_i
    for g in range(groups):
        w = jnp.where(lower, w_ref[g], 0.0).astype(BF16)
        bias = bt_ref[:, g:g + 1]
        cols = slice(g * gd, (g + 1) * gd)
        for c in range(tm // chunk):
            rows = slice(c * chunk, (c + 1) * chunk)
            mixed = jnp.dot(w, vn[rows, cols], preferred_element_type=F32) + bias
            o_ref[rows, cols] = (u[rows, cols] * mixed).astype(o_ref.dtype)


def spatial_gating(z, g_sgu, w_s, b_s, col_u, col_v, tm=256):
    n = z.shape[0]
    width = SGU_GROUPS * SGU_GROUP_DIM
    kern = functools.partial(_sgu_kernel, chunk=SGU_CHUNK, groups=SGU_GROUPS, gd=SGU_GROUP_DIM)
    return pl.pallas_call(
        kern,
        out_shape=jax.ShapeDtypeStruct((n, width), BF16),
        grid=(n // tm,),
        in_specs=[pl.BlockSpec((tm, width), lambda i: (i, col_u)),
                  pl.BlockSpec((tm, width), lambda i: (i, col_v)),
                  pl.BlockSpec((1, width), lambda i: (0, 0)),
                  pl.BlockSpec(w_s.shape, lambda i: (0, 0, 0)),
                  pl.BlockSpec((SGU_CHUNK, SGU_GROUPS), lambda i: (0, 0))],
        out_specs=pl.BlockSpec((tm, width), lambda i: (i, 0)),
        compiler_params=_cparams(("parallel",)),
        name="spatial_gating",
    )(z, z, g_sgu.reshape(1, width), w_s, b_s.T)


def _router_kernel(x_ref, g_ref, rwt_ref, rb_ref, eidx_ref, wts_ref, rank_ref, cnt_ref):
    i = pl.program_id(0)
    ne = rwt_ref.shape[0]
    tm = x_ref.shape[0]

    @pl.when(i == 0)
    def _():
        cnt_ref[...] = jnp.zeros_like(cnt_ref)

    h = _rms(x_ref[...], g_ref[...])
    h_hi = h.astype(BF16)
    h_lo = (h - h_hi.astype(F32)).astype(BF16)
    rw = rwt_ref[...]
    rw_hi = rw.astype(BF16)
    rw_hi_f32 = rw_hi.astype(F32)
    rw_parts = jnp.concatenate([rw_hi_f32, rw - rw_hi_f32], axis=0).astype(BF16)
    nt_dims = (((1,), (1,)), ((), ()))
    both = lax.dot_general(rw_parts, h_hi, nt_dims, preferred_element_type=F32)
    logits = (both[:ne] + both[ne:]
              + lax.dot_general(rw_hi, h_lo, nt_dims, preferred_element_type=F32)
              + rb_ref[...])
    row = lax.broadcasted_iota(jnp.int32, (ne, tm), 0).astype(F32)
    neg_inf = jnp.float32(-jnp.inf)
    m0 = jnp.max(logits, axis=0, keepdims=True)
    i0 = jnp.min(jnp.where(logits == m0, row, float(ne)), axis=0, keepdims=True)
    rest = jnp.where(row == i0, neg_inf, logits)
    m1 = jnp.max(rest, axis=0, keepdims=True)
    i1 = jnp.min(jnp.where(jnp.logical_and(rest == m1, row != i0), row, float(ne)),
                 axis=0, keepdims=True)
    e1 = jnp.exp(m1 - m0)
    denom = 1.0 + e1
    eidx_ref[0:1, :] = i0.astype(jnp.int32)
    eidx_ref[1:2, :] = i1.astype(jnp.int32)
    wts_ref[0:1, :] = 1.0 / denom
    wts_ref[1:2, :] = e1 / denom

    hit0 = row == i0
    hit1 = row == i1
    chosen = jnp.logical_or(hit0, hit1).astype(BF16)
    earlier = (lax.broadcasted_iota(jnp.int32, (tm, tm), 0)
               < lax.broadcasted_iota(jnp.int32, (tm, tm), 1)).astype(BF16)
    before = jnp.dot(chosen, earlier, preferred_element_type=F32) + cnt_ref[:, 0:1]
    rank_ref[0:1, :] = jnp.sum(jnp.where(hit0, before, 0.0), axis=0, keepdims=True).astype(jnp.int32)
    rank_ref[1:2, :] = jnp.sum(jnp.where(hit1, before, 0.0), axis=0, keepdims=True).astype(jnp.int32)
    cnt_ref[...] = cnt_ref[...] + jnp.sum(chosen.astype(F32), axis=1, keepdims=True)


def moe_router(x, g, router_w, router_b, tm=ROW_TILE):
    n, d = x.shape
    ne = router_w.shape[1]
    out_shapes = (jax.ShapeDtypeStruct((TOP_K, n), jnp.int32),
                  jax.ShapeDtypeStruct((TOP_K, n), F32),
                  jax.ShapeDtypeStruct((TOP_K, n), jnp.int32),
                  jax.ShapeDtypeStruct((ne, 128), F32))
    tok_spec = pl.BlockSpec((TOP_K, tm), lambda i: (0, i))
    return pl.pallas_call(
        _router_kernel,
        out_shape=out_shapes,
        grid=(n // tm,),
        in_specs=[pl.BlockSpec((tm, d), lambda i: (i, 0)),
                  pl.BlockSpec((1, d), lambda i: (0, 0)),
                  pl.BlockSpec((ne, d), lambda i: (0, 0)),
                  pl.BlockSpec((ne, 1), lambda i: (0, 0))],
        out_specs=(tok_spec, tok_spec, tok_spec, pl.BlockSpec((ne, 128), lambda i: (0, 0))),
        compiler_params=_cparams(("arbitrary",)),
        name="moe_router",
    )(x, g.reshape(1, d), router_w.T, router_b.reshape(ne, 1))


ISSUE_UNROLL = 8


def _start_row_gather(src_hbm, idx_ref, idx_base, n_rows, buf, slot, row_base, sem):
    def body(j, carry):
        row = idx_ref[idx_base + j]
        pltpu.make_async_copy(src_hbm.at[pl.ds(row, 1), :],
                              buf.at[slot, pl.ds(row_base + j, 1), :], sem.at[slot]).start()
        return carry
    lax.fori_loop(0, n_rows, body, 0, unroll=ISSUE_UNROLL)


def _wait_row_gather(src_hbm, buf, slot, sem):
    n_rows = buf.shape[1]
    pltpu.make_async_copy(src_hbm.at[pl.ds(0, n_rows), :], buf.at[slot], sem.at[slot]).wait()


def _dispatch_kernel(src_ref, x_hbm, g_ref, o_ref, buf, sem, *, tg):
    i = pl.program_id(0)
    nsteps = pl.num_programs(0)

    @pl.when(i == 0)
    def _():
        _start_row_gather(x_hbm, src_ref, 0, tg, buf, 0, 0, sem)

    @pl.when(i + 1 < nsteps)
    def _():
        _start_row_gather(x_hbm, src_ref, (i + 1) * tg, tg, buf, jnp.bitwise_and(i + 1, 1), 0, sem)

    slot = jnp.bitwise_and(i, 1)
    _wait_row_gather(x_hbm, buf, slot, sem)
    o_ref[...] = _rms(buf[slot], g_ref[...]).astype(o_ref.dtype)


def moe_dispatch(x, g, src, tg=256):
    n, d = x.shape
    n_slots = src.shape[0]
    kern = functools.partial(_dispatch_kernel, tg=tg)
    return pl.pallas_call(
        kern,
        out_shape=jax.ShapeDtypeStruct((n_slots, d), BF16),
        grid_spec=pltpu.PrefetchScalarGridSpec(
            num_scalar_prefetch=1,
            grid=(n_slots // tg,),
            in_specs=[pl.BlockSpec(memory_space=pl.ANY),
                      pl.BlockSpec((1, d), lambda i, s: (0, 0))],
            out_specs=pl.BlockSpec((tg, d), lambda i, s: (i, 0)),
            scratch_shapes=[pltpu.VMEM((2, tg, d), F32), pltpu.SemaphoreType.DMA((2,))],
        ),
        compiler_params=_cparams(("arbitrary",)),
        name="moe_dispatch",
    )(src, x, g.reshape(1, d))


def _combine_kernel(slot_ref, y_hbm, x_ref, w_ref, g_ref, o_ref, buf, sem, *, tc, final_norm):
    i = pl.program_id(0)
    nsteps = pl.num_programs(0)
    n_tok = nsteps * tc

    def fetch(step, slot):
        for k in range(TOP_K):
            _start_row_gather(y_hbm, slot_ref, k * n_tok + step * tc, tc, buf, slot, k * tc, sem)

    @pl.when(i == 0)
    def _():
        fetch(0, 0)

    @pl.when(i + 1 < nsteps)
    def _():
        fetch(i + 1, jnp.bitwise_and(i + 1, 1))

    slot = jnp.bitwise_and(i, 1)
    _wait_row_gather(y_hbm, buf, slot, sem)
    acc = x_ref[...]
    for k in range(TOP_K):
        acc = acc + w_ref[:, k:k + 1] * buf[slot, k * tc:(k + 1) * tc, :]
    if final_norm:
        acc = _rms(acc, g_ref[...])
    o_ref[...] = acc


def moe_combine(y, x, slot, wts, g, final_norm, tc=128):
    n, d = x.shape
    kern = functools.partial(_combine_kernel, tc=tc, final_norm=final_norm)
    return pl.pallas_call(
        kern,
        out_shape=jax.ShapeDtypeStruct((n, d), F32),
        grid_spec=pltpu.PrefetchScalarGridSpec(
            num_scalar_prefetch=1,
            grid=(n // tc,),
            in_specs=[pl.BlockSpec(memory_space=pl.ANY),
                      pl.BlockSpec((tc, d), lambda i, s: (i, 0)),
                      pl.BlockSpec((tc, TOP_K), lambda i, s: (i, 0)),
                      pl.BlockSpec((1, d), lambda i, s: (0, 0))],
            out_specs=pl.BlockSpec((tc, d), lambda i, s: (i, 0)),
            scratch_shapes=[pltpu.VMEM((2, TOP_K * tc, d), F32), pltpu.SemaphoreType.DMA((2,))],
        ),
        compiler_params=_cparams(("arbitrary",)),
        name="moe_combine",
    )(slot.reshape(-1), y, x, wts.T, g.reshape(1, d))


def token_mixer(x, batch, seq, layer, norm_g, w_in, g_sgu, w_s, b_s, w_pa, w_pb, w_o):
    n, d = x.shape
    aw = ATTN_HEADS * HEAD_DIM
    sw = SGU_GROUPS * SGU_GROUP_DIM
    tn = 1024
    h = rmsnorm(x, norm_g, BF16)
    q_scale = jnp.where(jnp.arange(w_in.shape[2]) < aw, HEAD_DIM ** -0.5 * LOG2_E, 1.0).astype(F32)
    z = ws_matmul([h], [w_in], [0], [], [], _ep_plain, BF16, tn, w_base=layer,
                  col_scale=q_scale, tm=DENSE_ROW_TILE, name="in_proj")
    attn = moba_attention(z.reshape(batch, seq, z.shape[1]), batch, seq).reshape(n, aw)
    sgu = spatial_gating(z, g_sgu, w_s, b_s, (3 * aw) // sw, (3 * aw + sw) // sw)
    gate_col = (3 * aw + 2 * sw) // tn
    merged = ws_matmul([attn, sgu], [w_pa, w_pb], [0, 1], [z, z],
                       [gate_col, gate_col + d // tn], _ep_merge, BF16, tn, w_base=layer,
                       tm=DENSE_ROW_TILE, name="branch_merge")
    return ws_matmul([merged], [w_o], [0], [x], [0], _ep_residual, F32, tn, w_base=layer,
                     tm=DENSE_ROW_TILE, name="out_proj")


def dense_ffn(x, norm_g, j, wg, wu, wd):
    h = rmsnorm(x, norm_g, BF16)
    act = ws_matmul([h], [wg, wu], [0, 0], [], [], _ep_swiglu, BF16, 512, w_base=j,
                    tm=DENSE_ROW_TILE, name="ffn_up")
    return ws_matmul([act], [wd], [0], [x], [0], _ep_residual, F32, 512, w_base=j,
                     tm=DENSE_ROW_TILE, name="ffn_down")


def moe_ffn(x, norm_g, j, router_w, router_b, wg, wu, wd, final_g, final_norm):
    n, d = x.shape
    ne = router_w.shape[1]
    tm = ROW_TILE
    eidx, wts, rank, cnt = moe_router(x, norm_g, router_w, router_b)
    counts = cnt[:, 0].astype(jnp.int32)
    padded = ((counts + tm - 1) // tm) * tm
    ends = jnp.cumsum(padded)
    starts = ends - padded
    group_start = jnp.sum(jnp.where(eidx[..., None] == jnp.arange(ne), starts, 0), axis=-1)
    slot = group_start + rank
    n_tiles = (TOP_K * n) // tm + ne
    tok = jnp.tile(jnp.arange(n, dtype=jnp.int32), TOP_K)
    src = (jnp.arange(n_tiles * tm, dtype=jnp.int32) % n).at[slot.reshape(-1)].set(tok)
    tile_row = jnp.arange(n_tiles, dtype=jnp.int32) * tm
    last_used = jnp.max(jnp.where(counts > 0, jnp.arange(ne, dtype=jnp.int32), 0))
    texp = jnp.minimum(jnp.sum(tile_row[:, None] >= ends[None, :], axis=1), last_used).astype(jnp.int32)
    group_end = jnp.sum(jnp.where(texp[:, None] == jnp.arange(ne), starts + counts, 0), axis=-1)
    tvalid = jnp.clip(group_end - tile_row, 0, tm).astype(jnp.int32)

    hs = moe_dispatch(x, norm_g, src)
    act = ws_matmul([hs], [wg, wu], [0, 0], [], [], _ep_swiglu, BF16, 512, texp=texp,
                    tvalid=tvalid, sub_rows=MOE_SUB_ROWS, w_base=j * ne, name="moe_up")
    y = ws_matmul([act], [wd], [0], [], [], _ep_plain, F32, 1024, texp=texp, tvalid=tvalid,
                  sub_rows=MOE_SUB_ROWS, w_base=j * ne, name="moe_down")
    return moe_combine(y, x, slot, wts, final_g, final_norm)


def kernel(x, mix_norm_g, w_in, sgu_norm_g, w_s, b_s, w_pa, w_pb, w_o, ffn_norm_g,
           dense_w_gate, dense_w_up, dense_w_down, router_w, router_b,
           expert_w_gate, expert_w_up, expert_w_down, final_norm_g):
    batch, seq, d = x.shape
    depth = mix_norm_g.shape[0]
    xf = x.reshape(batch * seq, d)
    merge_experts = lambda w: w.reshape((w.shape[0] * w.shape[1],) + w.shape[2:])
    ewg, ewu, ewd = (merge_experts(w) for w in (expert_w_gate, expert_w_up, expert_w_down))
    normed = False
    for i in range(depth):
        xf = token_mixer(xf, batch, seq, i, mix_norm_g[i], w_in, sgu_norm_g[i], w_s[i], b_s[i],
                         w_pa, w_pb, w_o)
        j = i // 2
        last = i == depth - 1
        if i % 2 == 0:
            xf = dense_ffn(xf, ffn_norm_g[i], j, dense_w_gate, dense_w_up, dense_w_down)
        else:
            xf = moe_ffn(xf, ffn_norm_g[i], j, router_w[j], router_b[j], ewg, ewu, ewd,
                         final_norm_g, last)
            normed = last
    if not normed:
        xf = rmsnorm(xf, final_norm_g, F32)
    return xf.reshape(batch, seq, d)
```

```python
import functools

import jax
import jax.numpy as jnp
from jax import lax
from jax.experimental import pallas as pl
from jax.experimental.pallas import tpu as pltpu

F32 = jnp.float32
BF16 = jnp.bfloat16

ATTN_HEADS = 8
HEAD_DIM = 128
MOBA_BLOCK = 256
MOBA_TOPK = 3
SGU_GROUPS = 8
SGU_GROUP_DIM = 128
SGU_CHUNK = 128
TOP_K = 2
NORM_EPS = 1e-6
LOG2_E = 1.4426950408889634

V7X_VMEM_BYTES = 64 * 1024 * 1024
VMEM_LIMIT_BYTES = V7X_VMEM_BYTES - 8 * 1024 * 1024

ROW_TILE = 512
DENSE_ROW_TILE = 1024
WIDE_ROW_TILE = 2048
CAST_ROWS = 256
MOE_SUB_ROWS = 128
SOFTMAX_LAG = 2
VALUES_LAG = 1


def _cparams(sem):
    return pltpu.CompilerParams(dimension_semantics=sem, vmem_limit_bytes=VMEM_LIMIT_BYTES)


def _rms(x, g):
    ms = jnp.mean(x * x, axis=-1, keepdims=True)
    return x * lax.rsqrt(ms + NORM_EPS) * g


def _rmsnorm_kernel(x_ref, g_ref, o_ref):
    o_ref[...] = _rms(x_ref[...], g_ref[...]).astype(o_ref.dtype)


def rmsnorm(x, g, out_dtype, tm=ROW_TILE):
    n, d = x.shape
    return pl.pallas_call(
        _rmsnorm_kernel,
        out_shape=jax.ShapeDtypeStruct((n, d), out_dtype),
        grid=(n // tm,),
        in_specs=[pl.BlockSpec((tm, d), lambda i: (i, 0)),
                  pl.BlockSpec((1, d), lambda i: (0, 0))],
        out_specs=pl.BlockSpec((tm, d), lambda i: (i, 0)),
        compiler_params=_cparams(("parallel",)),
        name="rmsnorm",
    )(x, g.reshape(1, d))


def _ws_kernel(texp_ref, tnext_ref, tvalid_ref, *refs, n_lhs, w_lhs, n_extra, epilogue,
               sub_rows, tn, scaled, round_once):
    n_w = len(w_lhs)
    lhs_refs = refs[:n_lhs]
    w_hbm = refs[n_lhs:n_lhs + n_w]
    extra_refs = refs[n_lhs + n_w:n_lhs + n_w + n_extra]
    n_in = n_lhs + n_w + n_extra + int(scaled)
    scale_ref = refs[n_in - 1] if scaled else None
    out_ref = refs[n_in]
    scratch = refs[n_in + 1:]
    wf_refs = scratch[:n_w]
    tm = out_ref.shape[0]
    if round_once:
        wb_refs = scratch[n_w:2 * n_w]
        sem = scratch[2 * n_w]
    else:
        sem, groups_seen = scratch[n_w:]

    n = pl.program_id(0)
    r = pl.program_id(1)
    n_col_tiles = pl.num_programs(0)

    def weight_copy(i, index, col_tile, slot):
        cols = pl.ds(pl.multiple_of(col_tile * tn, tn), tn)
        dst, dst_sem = (wf_refs[i], sem.at[i]) if round_once else (wf_refs[i].at[slot], sem.at[i, slot])
        return pltpu.make_async_copy(w_hbm[i].at[index, :, cols], dst, dst_sem)

    def start_weights(index, col_tile, slot):
        for i in range(n_w):
            weight_copy(i, index, col_tile, slot).start()

    def start_following(slot):
        following = tnext_ref[r]

        @pl.when(following >= 0)
        def _():
            start_weights(following, n, slot)

        @pl.when(jnp.logical_and(following < 0, n + 1 < n_col_tiles))
        def _():
            start_weights(texp_ref[0], n + 1, slot)

    def rounded(w):
        if scaled:
            w = w * scale_ref[...]
        return w.astype(BF16)

    @pl.when(jnp.logical_and(n == 0, r == 0))
    def _():
        if not round_once:
            groups_seen[0] = 0
        start_weights(texp_ref[0], 0, 0)

    new_group = jnp.logical_or(r == 0, texp_ref[r] != texp_ref[jnp.maximum(r - 1, 0)])

    @pl.when(new_group)
    def _():
        if round_once:
            for i in range(n_w):
                weight_copy(i, texp_ref[r], n, 0).wait()
            for wf_ref, wb_ref in zip(wf_refs, wb_refs):
                def cast_rows(c, carry, wf_ref=wf_ref, wb_ref=wb_ref):
                    rows = pl.ds(pl.multiple_of(c * CAST_ROWS, CAST_ROWS), CAST_ROWS)
                    wb_ref[rows, :] = rounded(wf_ref[rows, :])
                    return carry

                lax.fori_loop(0, wb_ref.shape[0] // CAST_ROWS, cast_rows, 0)
            start_following(0)
        else:
            seen = groups_seen[0]
            slot = jnp.bitwise_and(seen, 1)
            groups_seen[0] = seen + 1
            for i in range(n_w):
                weight_copy(i, texp_ref[r], n, slot).wait()
            start_following(1 - slot)

    def operand(i):
        if round_once:
            return wb_refs[i][...]
        return rounded(wf_refs[i][jnp.bitwise_and(groups_seen[0] - 1, 1)])

    def compute(n_live):
        rows = slice(0, n_live)
        if n_live > 0:
            prods = [jnp.dot(lhs_refs[w_lhs[i]][rows, :], operand(i),
                             preferred_element_type=F32) for i in range(n_w)]
            out_ref[rows, :] = epilogue(prods, [e[rows, :] for e in extra_refs]).astype(out_ref.dtype)
        if n_live < tm:
            out_ref[n_live:tm, :] = jnp.zeros((tm - n_live, out_ref.shape[1]), out_ref.dtype)

    if sub_rows is None:
        compute(tm)
    else:
        live_chunks = (tvalid_ref[r] + (sub_rows - 1)) // sub_rows
        for c in range(tm // sub_rows + 1):
            pl.when(live_chunks == c)(functools.partial(compute, c * sub_rows))


def ws_matmul(lhs, ws, w_lhs, extras, extra_col_off, epilogue, out_dtype, tn, texp=None,
              tvalid=None, sub_rows=None, w_base=0, col_scale=None, round_once=False,
              tm=ROW_TILE, name="ws_matmul"):
    m = lhs[0].shape[0]
    n_total = ws[0].shape[2]
    n_rows = m // tm
    if texp is None:
        texp = jnp.zeros((n_rows,), jnp.int32)
    if tvalid is None:
        tvalid = jnp.full((n_rows,), tm, jnp.int32)
    no_next = jnp.iinfo(jnp.int32).max
    later = jnp.min(jnp.where(texp[None, :] > texp[:, None], texp[None, :], no_next), axis=1)
    tnext = jnp.where(later == no_next, -1, later + w_base).astype(jnp.int32)
    texp = texp + w_base
    in_specs = []
    for a in lhs:
        in_specs.append(pl.BlockSpec((tm, a.shape[1]), lambda n, r, *_: (r, 0)))
    for w in ws:
        in_specs.append(pl.BlockSpec(memory_space=pl.ANY))
    for off in extra_col_off:
        in_specs.append(pl.BlockSpec((tm, tn), lambda n, r, *_, off=off: (r, off + n)))
    scale_args = []
    if col_scale is not None:
        in_specs.append(pl.BlockSpec((1, tn), lambda n, r, *_: (0, n)))
        scale_args.append(col_scale.reshape(1, n_total))
    kern = functools.partial(_ws_kernel, n_lhs=len(lhs), w_lhs=tuple(w_lhs),
                             n_extra=len(extras), epilogue=epilogue, sub_rows=sub_rows, tn=tn,
                             scaled=col_scale is not None, round_once=round_once)
    if round_once:
        scratch = ([pltpu.VMEM((w.shape[1], tn), F32) for w in ws]
                   + [pltpu.VMEM((w.shape[1], tn), BF16) for w in ws]
                   + [pltpu.SemaphoreType.DMA((len(ws),))])
    else:
        scratch = ([pltpu.VMEM((2, w.shape[1], tn), F32) for w in ws]
                   + [pltpu.SemaphoreType.DMA((len(ws), 2)), pltpu.SMEM((1,), jnp.int32)])
    return pl.pallas_call(
        kern,
        out_shape=jax.ShapeDtypeStruct((m, n_total), out_dtype),
        grid_spec=pltpu.PrefetchScalarGridSpec(
            num_scalar_prefetch=3,
            grid=(n_total // tn, n_rows),
            in_specs=in_specs,
            out_specs=pl.BlockSpec((tm, tn), lambda n, r, *_: (r, n)),
            scratch_shapes=scratch,
        ),
        compiler_params=_cparams(("arbitrary", "arbitrary")),
        name=name,
    )(texp, tnext, tvalid, *lhs, *ws, *extras, *scale_args)


def _ep_plain(prods, extras):
    return prods[0]


def _ep_residual(prods, extras):
    return extras[0] + prods[0]


def _ep_swiglu(prods, extras):
    g, u = prods
    return (g * jax.nn.sigmoid(g)) * u


def _ep_merge(prods, extras):
    ga, gb = extras
    return (jax.nn.sigmoid(ga.astype(F32)) * prods[0]
            + jax.nn.sigmoid(gb.astype(F32)) * prods[1])


def _attn_kernel(q_ref, k_ref, v_ref, o_ref, *, nb, blk, topk):
    seq = k_ref.shape[0]
    neg_inf = jnp.float32(-jnp.inf)
    avg = jnp.where(lax.broadcasted_iota(jnp.int32, (nb, seq), 1) // blk
                    == lax.broadcasted_iota(jnp.int32, (nb, seq), 0), 1.0 / blk, 0.0).astype(BF16)
    k_mean = jnp.dot(avg, k_ref[...], preferred_element_type=F32)
    k_mean_hi = k_mean.astype(BF16).astype(F32)
    k_mean_parts = jnp.concatenate([k_mean_hi, k_mean - k_mean_hi], axis=0).astype(BF16)
    blk_id = lax.broadcasted_iota(jnp.int32, (nb, blk), 0)
    causal = (lax.broadcasted_iota(jnp.int32, (blk, blk), 0)
              <= lax.broadcasted_iota(jnp.int32, (blk, blk), 1))
    nt_dims = (((1,), (1,)), ((), ()))
    v_t = v_ref[...].T

    def scores(n):
        qn = q_ref[n * blk:(n + 1) * blk, :]
        s = lax.dot_general(k_ref[0:(n + 1) * blk, :], qn, nt_dims, preferred_element_type=F32)
        gate_parts = None
        if n > topk:
            gate_parts = lax.dot_general(k_mean_parts, qn, nt_dims, preferred_element_type=F32)
        return s, gate_parts

    def masked_softmax(n, s, gate_parts):
        past = [s[j * blk:(j + 1) * blk, :] for j in range(n)]
        if n > topk:
            gate = gate_parts[:nb] + gate_parts[nb:]
            beaten = jnp.zeros((nb, blk), F32)
            for j in range(n):
                gj = gate[j:j + 1, :]
                wins = jnp.logical_or(gj > gate, jnp.logical_and(gj == gate, j < blk_id))
                beaten = beaten + wins.astype(F32)
            bias = jnp.where(beaten < topk, 0.0, neg_inf)
            past = [past[j] + bias[j:j + 1, :] for j in range(n)]
        own = jnp.where(causal, s[n * blk:(n + 1) * blk, :], neg_inf)
        s = jnp.concatenate(past + [own], axis=0)
        m = jnp.max(s, axis=0, keepdims=True)
        p = jnp.exp2(s - m)
        return p.astype(BF16), jnp.sum(p, axis=0, keepdims=True)

    def weighted_values(n, p, l):
        o_t = jnp.dot(v_t[:, 0:(n + 1) * blk], p, preferred_element_type=F32)
        o_ref[n * blk:(n + 1) * blk, :] = (o_t / l).T.astype(o_ref.dtype)

    order = list(range(nb - 1, -1, -1))
    s_out, p_out = {}, {}
    for t in range(nb + SOFTMAX_LAG + VALUES_LAG):
        if t < nb:
            s_out[order[t]] = scores(order[t])
        if 0 <= t - SOFTMAX_LAG < nb:
            n = order[t - SOFTMAX_LAG]
            p_out[n] = masked_softmax(n, *s_out.pop(n))
        if 0 <= t - SOFTMAX_LAG - VALUES_LAG < nb:
            n = order[t - SOFTMAX_LAG - VALUES_LAG]
            weighted_values(n, *p_out.pop(n))


def moba_attention(z, batch, seq):
    nb = seq // MOBA_BLOCK
    kern = functools.partial(_attn_kernel, nb=nb, blk=MOBA_BLOCK, topk=MOBA_TOPK)
    blockspec = lambda off: pl.BlockSpec((None, seq, HEAD_DIM), lambda b, h: (b, 0, off + h))
    return pl.pallas_call(
        kern,
        out_shape=jax.ShapeDtypeStruct((batch, seq, ATTN_HEADS * HEAD_DIM), BF16),
        grid=(batch, ATTN_HEADS),
        in_specs=[blockspec(0), blockspec(ATTN_HEADS), blockspec(2 * ATTN_HEADS)],
        out_specs=pl.BlockSpec((None, seq, HEAD_DIM), lambda b, h: (b, 0, h)),
        compiler_params=_cparams(("parallel", "parallel")),
        name="moba_attention",
    )(z, z, z)


def _gelu_tanh(x):
    return 0.5 * x * (1.0 + jnp.tanh(0.7978845608028654 * (x + 0.044715 * (x * x * x))))


def _sgu_kernel(u_ref, v_ref, g_ref, w_ref, bt_ref, o_ref, *, chunk, groups, gd):
    tm = u_ref.shape[0]
    u = _gelu_tanh(u_ref[...].astype(F32))
    v = _gelu_tanh(v_ref[...].astype(F32))
    mu = jnp.mean(v, axis=-1, keepdims=True)
    vc = v - mu
    var = jnp.mean(vc * vc, axis=-1, keepdims=True)
    vn = (vc * lax.rsqrt(var + NORM_EPS) * g_ref[...]).astype(BF16)
    t_i = lax.broadcasted_iota(jnp.int32, (chunk, chunk), 0)
    s_i = lax.broadcasted_iota(jnp.int32, (chunk, chunk), 1)
    lower = s_i <= t_i
    for g in range(groups):
        w = jnp.where(lower, w_ref[g], 0.0).astype(BF16)
        bias = bt_ref[:, g:g + 1]
        cols = slice(g * gd, (g + 1) * gd)
        for c in range(tm // chunk):
            rows = slice(c * chunk, (c + 1) * chunk)
            mixed = jnp.dot(w, vn[rows, cols], preferred_element_type=F32) + bias
            o_ref[rows, cols] = (u[rows, cols] * mixed).astype(o_ref.dtype)


def spatial_gating(z, g_sgu, w_s, b_s, col_u, col_v, tm=256):
    n = z.shape[0]
    width = SGU_GROUPS * SGU_GROUP_DIM
    kern = functools.partial(_sgu_kernel, chunk=SGU_CHUNK, groups=SGU_GROUPS, gd=SGU_GROUP_DIM)
    return pl.pallas_call(
        kern,
        out_shape=jax.ShapeDtypeStruct((n, width), BF16),
        grid=(n // tm,),
        in_specs=[pl.BlockSpec((tm, width), lambda i: (i, col_u)),
                  pl.BlockSpec((tm, width), lambda i: (i, col_v)),
                  pl.BlockSpec((1, width), lambda i: (0, 0)),
                  pl.BlockSpec(w_s.shape, lambda i: (0, 0, 0)),
                  pl.BlockSpec((SGU_CHUNK, SGU_GROUPS), lambda i: (0, 0))],
        out_specs=pl.BlockSpec((tm, width), lambda i: (i, 0)),
        compiler_params=_cparams(("parallel",)),
        name="spatial_gating",
    )(z, z, g_sgu.reshape(1, width), w_s, b_s.T)


def _router_kernel(x_ref, g_ref, rwt_ref, rb_ref, eidx_ref, wts_ref, rank_ref, cnt_ref):
    i = pl.program_id(0)
    ne = rwt_ref.shape[0]
    tm = x_ref.shape[0]

    @pl.when(i == 0)
    def _():
        cnt_ref[...] = jnp.zeros_like(cnt_ref)

    h = _rms(x_ref[...], g_ref[...])
    h_hi = h.astype(BF16)
    h_lo = (h - h_hi.astype(F32)).astype(BF16)
    rw = rwt_ref[...]
    rw_hi = rw.astype(BF16)
    rw_hi_f32 = rw_hi.astype(F32)
    rw_parts = jnp.concatenate([rw_hi_f32, rw - rw_hi_f32], axis=0).astype(BF16)
    nt_dims = (((1,), (1,)), ((), ()))
    both = lax.dot_general(rw_parts, h_hi, nt_dims, preferred_element_type=F32)
    logits = (both[:ne] + both[ne:]
              + lax.dot_general(rw_hi, h_lo, nt_dims, preferred_element_type=F32)
              + rb_ref[...])
    row = lax.broadcasted_iota(jnp.int32, (ne, tm), 0).astype(F32)
    neg_inf = jnp.float32(-jnp.inf)
    m0 = jnp.max(logits, axis=0, keepdims=True)
    i0 = jnp.min(jnp.where(logits == m0, row, float(ne)), axis=0, keepdims=True)
    rest = jnp.where(row == i0, neg_inf, logits)
    m1 = jnp.max(rest, axis=0, keepdims=True)
    i1 = jnp.min(jnp.where(jnp.logical_and(rest == m1, row != i0), row, float(ne)),
                 axis=0, keepdims=True)
    e1 = jnp.exp(m1 - m0)
    denom = 1.0 + e1
    eidx_ref[0:1, :] = i0.astype(jnp.int32)
    eidx_ref[1:2, :] = i1.astype(jnp.int32)
    wts_ref[0:1, :] = 1.0 / denom
    wts_ref[1:2, :] = e1 / denom

    hit0 = row == i0
    hit1 = row == i1
    chosen = jnp.logical_or(hit0, hit1).astype(BF16)
    earlier = (lax.broadcasted_iota(jnp.int32, (tm, tm), 0)
               < lax.broadcasted_iota(jnp.int32, (tm, tm), 1)).astype(BF16)
    before = jnp.dot(chosen, earlier, preferred_element_type=F32) + cnt_ref[:, 0:1]
    rank_ref[0:1, :] = jnp.sum(jnp.where(hit0, before, 0.0), axis=0, keepdims=True).astype(jnp.int32)
    rank_ref[1:2, :] = jnp.sum(jnp.where(hit1, before, 0.0), axis=0, keepdims=True).astype(jnp.int32)
    cnt_ref[...] = cnt_ref[...] + jnp.sum(chosen.astype(F32), axis=1, keepdims=True)


def moe_router(x, g, router_w, router_b, tm=ROW_TILE):
    n, d = x.shape
    ne = router_w.shape[1]
    out_shapes = (jax.ShapeDtypeStruct((TOP_K, n), jnp.int32),
                  jax.ShapeDtypeStruct((TOP_K, n), F32),
                  jax.ShapeDtypeStruct((TOP_K, n), jnp.int32),
                  jax.ShapeDtypeStruct((ne, 128), F32))
    tok_spec = pl.BlockSpec((TOP_K, tm), lambda i: (0, i))
    return pl.pallas_call(
        _router_kernel,
        out_shape=out_shapes,
        grid=(n // tm,),
        in_specs=[pl.BlockSpec((tm, d), lambda i: (i, 0)),
                  pl.BlockSpec((1, d), lambda i: (0, 0)),
                  pl.BlockSpec((ne, d), lambda i: (0, 0)),
                  pl.BlockSpec((ne, 1), lambda i: (0, 0))],
        out_specs=(tok_spec, tok_spec, tok_spec, pl.BlockSpec((ne, 128), lambda i: (0, 0))),
        compiler_params=_cparams(("arbitrary",)),
        name="moe_router",
    )(x, g.reshape(1, d), router_w.T, router_b.reshape(ne, 1))


ISSUE_UNROLL = 8


def _start_row_gather(src_hbm, idx_ref, idx_base, n_rows, buf, slot, row_base, sem):
    def body(j, carry):
        row = idx_ref[idx_base + j]
        pltpu.make_async_copy(src_hbm.at[pl.ds(row, 1), :],
                              buf.at[slot, pl.ds(row_base + j, 1), :], sem.at[slot]).start()
        return carry
    lax.fori_loop(0, n_rows, body, 0, unroll=ISSUE_UNROLL)


def _wait_row_gather(src_hbm, buf, slot, sem):
    n_rows = buf.shape[1]
    pltpu.make_async_copy(src_hbm.at[pl.ds(0, n_rows), :], buf.at[slot], sem.at[slot]).wait()


def _dispatch_kernel(src_ref, x_hbm, g_ref, o_ref, buf, sem, *, tg):
    i = pl.program_id(0)
    nsteps = pl.num_programs(0)

    @pl.when(i == 0)
    def _():
        _start_row_gather(x_hbm, src_ref, 0, tg, buf, 0, 0, sem)

    @pl.when(i + 1 < nsteps)
    def _():
        _start_row_gather(x_hbm, src_ref, (i + 1) * tg, tg, buf, jnp.bitwise_and(i + 1, 1), 0, sem)

    slot = jnp.bitwise_and(i, 1)
    _wait_row_gather(x_hbm, buf, slot, sem)
    o_ref[...] = _rms(buf[slot], g_ref[...]).astype(o_ref.dtype)


def moe_dispatch(x, g, src, tg=256):
    n, d = x.shape
    n_slots = src.shape[0]
    kern = functools.partial(_dispatch_kernel, tg=tg)
    return pl.pallas_call(
        kern,
        out_shape=jax.ShapeDtypeStruct((n_slots, d), BF16),
        grid_spec=pltpu.PrefetchScalarGridSpec(
            num_scalar_prefetch=1,
            grid=(n_slots // tg,),
            in_specs=[pl.BlockSpec(memory_space=pl.ANY),
                      pl.BlockSpec((1, d), lambda i, s: (0, 0))],
            out_specs=pl.BlockSpec((tg, d), lambda i, s: (i, 0)),
            scratch_shapes=[pltpu.VMEM((2, tg, d), F32), pltpu.SemaphoreType.DMA((2,))],
        ),
        compiler_params=_cparams(("arbitrary",)),
        name="moe_dispatch",
    )(src, x, g.reshape(1, d))


def _combine_kernel(slot_ref, y_hbm, x_ref, w_ref, g_ref, o_ref, buf, sem, *, tc, final_norm):
    i = pl.program_id(0)
    nsteps = pl.num_programs(0)
    n_tok = nsteps * tc

    def fetch(step, slot):
        for k in range(TOP_K):
            _start_row_gather(y_hbm, slot_ref, k * n_tok + step * tc, tc, buf, slot, k * tc, sem)

    @pl.when(i == 0)
    def _():
        fetch(0, 0)

    @pl.when(i + 1 < nsteps)
    def _():
        fetch(i + 1, jnp.bitwise_and(i + 1, 1))

    slot = jnp.bitwise_and(i, 1)
    _wait_row_gather(y_hbm, buf, slot, sem)
    acc = x_ref[...]
    for k in range(TOP_K):
        acc = acc + w_ref[:, k:k + 1] * buf[slot, k * tc:(k + 1) * tc, :]
    if final_norm:
        acc = _rms(acc, g_ref[...])
    o_ref[...] = acc


def moe_combine(y, x, slot, wts, g, final_norm, tc=128):
    n, d = x.shape
    kern = functools.partial(_combine_kernel, tc=tc, final_norm=final_norm)
    return pl.pallas_call(
        kern,
        out_shape=jax.ShapeDtypeStruct((n, d), F32),
        grid_spec=pltpu.PrefetchScalarGridSpec(
            num_scalar_prefetch=1,
            grid=(n // tc,),
            in_specs=[pl.BlockSpec(memory_space=pl.ANY),
                      pl.BlockSpec((tc, d), lambda i, s: (i, 0)),
                      pl.BlockSpec((tc, TOP_K), lambda i, s: (i, 0)),
                      pl.BlockSpec((1, d), lambda i, s: (0, 0))],
            out_specs=pl.BlockSpec((tc, d), lambda i, s: (i, 0)),
            scratch_shapes=[pltpu.VMEM((2, TOP_K * tc, d), F32), pltpu.SemaphoreType.DMA((2,))],
        ),
        compiler_params=_cparams(("arbitrary",)),
        name="moe_combine",
    )(slot.reshape(-1), y, x, wts.T, g.reshape(1, d))


def token_mixer(x, batch, seq, layer, norm_g, w_in, g_sgu, w_s, b_s, w_pa, w_pb, w_o):
    n, d = x.shape
    aw = ATTN_HEADS * HEAD_DIM
    sw = SGU_GROUPS * SGU_GROUP_DIM
    tn = 1024
    h = rmsnorm(x, norm_g, BF16)
    q_scale = jnp.where(jnp.arange(w_in.shape[2]) < aw, HEAD_DIM ** -0.5 * LOG2_E, 1.0).astype(F32)
    z = ws_matmul([h], [w_in], [0], [], [], _ep_plain, BF16, tn, w_base=layer,
                  col_scale=q_scale, tm=WIDE_ROW_TILE, name="in_proj")
    attn = moba_attention(z.reshape(batch, seq, z.shape[1]), batch, seq).reshape(n, aw)
    sgu = spatial_gating(z, g_sgu, w_s, b_s, (3 * aw) // sw, (3 * aw + sw) // sw)
    gate_col = (3 * aw + 2 * sw) // tn
    merged = ws_matmul([attn, sgu], [w_pa, w_pb], [0, 1], [z, z],
                       [gate_col, gate_col + d // tn], _ep_merge, BF16, tn, w_base=layer,
                       tm=DENSE_ROW_TILE, name="branch_merge")
    return ws_matmul([merged], [w_o], [0], [x], [0], _ep_residual, F32, tn, w_base=layer,
                     tm=DENSE_ROW_TILE, name="out_proj")


def dense_ffn(x, norm_g, j, wg, wu, wd):
    h = rmsnorm(x, norm_g, BF16)
    act = ws_matmul([h], [wg, wu], [0, 0], [], [], _ep_swiglu, BF16, 512, w_base=j,
                    tm=WIDE_ROW_TILE, name="ffn_up")
    return ws_matmul([act], [wd], [0], [x], [0], _ep_residual, F32, 512, w_base=j,
                     round_once=True, tm=DENSE_ROW_TILE, name="ffn_down")


def moe_ffn(x, norm_g, j, router_w, router_b, wg, wu, wd, final_g, final_norm):
    n, d = x.shape
    ne = router_w.shape[1]
    tm = ROW_TILE
    eidx, wts, rank, cnt = moe_router(x, norm_g, router_w, router_b)
    counts = cnt[:, 0].astype(jnp.int32)
    padded = ((counts + tm - 1) // tm) * tm
    ends = jnp.cumsum(padded)
    starts = ends - padded
    group_start = jnp.sum(jnp.where(eidx[..., None] == jnp.arange(ne), starts, 0), axis=-1)
    slot = group_start + rank
    n_tiles = (TOP_K * n) // tm + ne
    tok = jnp.tile(jnp.arange(n, dtype=jnp.int32), TOP_K)
    src = (jnp.arange(n_tiles * tm, dtype=jnp.int32) % n).at[slot.reshape(-1)].set(tok)
    tile_row = jnp.arange(n_tiles, dtype=jnp.int32) * tm
    last_used = jnp.max(jnp.where(counts > 0, jnp.arange(ne, dtype=jnp.int32), 0))
    texp = jnp.minimum(jnp.sum(tile_row[:, None] >= ends[None, :], axis=1), last_used).astype(jnp.int32)
    group_end = jnp.sum(jnp.where(texp[:, None] == jnp.arange(ne), starts + counts, 0), axis=-1)
    tvalid = jnp.clip(group_end - tile_row, 0, tm).astype(jnp.int32)

    hs = moe_dispatch(x, norm_g, src)
    act = ws_matmul([hs], [wg, wu], [0, 0], [], [], _ep_swiglu, BF16, 512, texp=texp,
                    tvalid=tvalid, sub_rows=MOE_SUB_ROWS, w_base=j * ne, name="moe_up")
    y = ws_matmul([act], [wd], [0], [], [], _ep_plain, F32, 1024, texp=texp, tvalid=tvalid,
                  sub_rows=MOE_SUB_ROWS, w_base=j * ne, round_once=True, name="moe_down")
    return moe_combine(y, x, slot, wts, final_g, final_norm)


def kernel(x, mix_norm_g, w_in, sgu_norm_g, w_s, b_s, w_pa, w_pb, w_o, ffn_norm_g,
           dense_w_gate, dense_w_up, dense_w_down, router_w, router_b,
           expert_w_gate, expert_w_up, expert_w_down, final_norm_g):
    batch, seq, d = x.shape
    depth = mix_norm_g.shape[0]
    xf = x.reshape(batch * seq, d)
    merge_experts = lambda w: w.reshape((w.shape[0] * w.shape[1],) + w.shape[2:])
    ewg, ewu, ewd = (merge_experts(w) for w in (expert_w_gate, expert_w_up, expert_w_down))
    normed = False
    for i in range(depth):
        xf = token_mixer(xf, batch, seq, i, mix_norm_g[i], w_in, sgu_norm_g[i], w_s[i], b_s[i],
                         w_pa, w_pb, w_o)
        j = i // 2
        last = i == depth - 1
        if i % 2 == 0:
            xf = dense_ffn(xf, ffn_norm_g[i], j, dense_w_gate, dense_w_up, dense_w_down)
        else:
            xf = moe_ffn(xf, ffn_norm_g[i], j, router_w[j], router_b[j], ewg, ewu, ewd,
                         final_norm_g, last)
            normed = last
    if not normed:
        xf = rmsnorm(xf, final_norm_g, F32)
    return xf.reshape(batch, seq, d)
```

```python
import functools

import jax
import jax.numpy as jnp
from jax import lax
from jax.experimental import pallas as pl
from jax.experimental.pallas import tpu as pltpu

F32 = jnp.float32
BF16 = jnp.bfloat16

ATTN_HEADS = 8
HEAD_DIM = 128
MOBA_BLOCK = 256
MOBA_TOPK = 3
SGU_GROUPS = 8
SGU_GROUP_DIM = 128
SGU_CHUNK = 128
TOP_K = 2
NORM_EPS = 1e-6
LOG2_E = 1.4426950408889634

V7X_VMEM_BYTES = 64 * 1024 * 1024
VMEM_LIMIT_BYTES = V7X_VMEM_BYTES - 8 * 1024 * 1024

ROW_TILE = 512
DENSE_ROW_TILE = 1024
WIDE_ROW_TILE = 2048
CAST_ROWS = 256
MOE_SUB_ROWS = 128
SOFTMAX_LAG = 2
VALUES_LAG = 1


def _cparams(sem):
    return pltpu.CompilerParams(dimension_semantics=sem, vmem_limit_bytes=VMEM_LIMIT_BYTES)


def _rms(x, g):
    ms = jnp.mean(x * x, axis=-1, keepdims=True)
    return x * lax.rsqrt(ms + NORM_EPS) * g


def _rmsnorm_kernel(x_ref, g_ref, o_ref):
    o_ref[...] = _rms(x_ref[...], g_ref[...]).astype(o_ref.dtype)


def rmsnorm(x, g, out_dtype, tm=ROW_TILE):
    n, d = x.shape
    return pl.pallas_call(
        _rmsnorm_kernel,
        out_shape=jax.ShapeDtypeStruct((n, d), out_dtype),
        grid=(n // tm,),
        in_specs=[pl.BlockSpec((tm, d), lambda i: (i, 0)),
                  pl.BlockSpec((1, d), lambda i: (0, 0))],
        out_specs=pl.BlockSpec((tm, d), lambda i: (i, 0)),
        compiler_params=_cparams(("parallel",)),
        name="rmsnorm",
    )(x, g.reshape(1, d))


def _ws_kernel(texp_ref, tnext_ref, tvalid_ref, *refs, n_lhs, w_lhs, n_extra, epilogue,
               sub_rows, tn, scaled, round_once):
    n_w = len(w_lhs)
    lhs_refs = refs[:n_lhs]
    w_hbm = refs[n_lhs:n_lhs + n_w]
    extra_refs = refs[n_lhs + n_w:n_lhs + n_w + n_extra]
    n_in = n_lhs + n_w + n_extra + int(scaled)
    scale_ref = refs[n_in - 1] if scaled else None
    out_ref = refs[n_in]
    scratch = refs[n_in + 1:]
    wf_refs = scratch[:n_w]
    tm = out_ref.shape[0]
    if round_once:
        wb_refs = scratch[n_w:2 * n_w]
        sem = scratch[2 * n_w]
    else:
        sem, groups_seen = scratch[n_w:]

    n = pl.program_id(0)
    r = pl.program_id(1)
    n_col_tiles = pl.num_programs(0)

    def weight_copy(i, index, col_tile, slot):
        cols = pl.ds(pl.multiple_of(col_tile * tn, tn), tn)
        dst, dst_sem = (wf_refs[i], sem.at[i]) if round_once else (wf_refs[i].at[slot], sem.at[i, slot])
        return pltpu.make_async_copy(w_hbm[i].at[index, :, cols], dst, dst_sem)

    def start_weights(index, col_tile, slot):
        for i in range(n_w):
            weight_copy(i, index, col_tile, slot).start()

    def start_following(slot):
        following = tnext_ref[r]

        @pl.when(following >= 0)
        def _():
            start_weights(following, n, slot)

        @pl.when(jnp.logical_and(following < 0, n + 1 < n_col_tiles))
        def _():
            start_weights(texp_ref[0], n + 1, slot)

    def rounded(w):
        if scaled:
            w = w * scale_ref[...]
        return w.astype(BF16)

    @pl.when(jnp.logical_and(n == 0, r == 0))
    def _():
        if not round_once:
            groups_seen[0] = 0
        start_weights(texp_ref[0], 0, 0)

    new_group = jnp.logical_or(r == 0, texp_ref[r] != texp_ref[jnp.maximum(r - 1, 0)])

    @pl.when(new_group)
    def _():
        if round_once:
            for i in range(n_w):
                weight_copy(i, texp_ref[r], n, 0).wait()
            for wf_ref, wb_ref in zip(wf_refs, wb_refs):
                def cast_rows(c, carry, wf_ref=wf_ref, wb_ref=wb_ref):
                    rows = pl.ds(pl.multiple_of(c * CAST_ROWS, CAST_ROWS), CAST_ROWS)
                    wb_ref[rows, :] = rounded(wf_ref[rows, :])
                    return carry

                lax.fori_loop(0, wb_ref.shape[0] // CAST_ROWS, cast_rows, 0)
            start_following(0)
        else:
            seen = groups_seen[0]
            slot = jnp.bitwise_and(seen, 1)
            groups_seen[0] = seen + 1
            for i in range(n_w):
                weight_copy(i, texp_ref[r], n, slot).wait()
            start_following(1 - slot)

    def operand(i):
        if round_once:
            return wb_refs[i][...]
        return rounded(wf_refs[i][jnp.bitwise_and(groups_seen[0] - 1, 1)])

    def compute(n_live):
        rows = slice(0, n_live)
        if n_live > 0:
            prods = [jnp.dot(lhs_refs[w_lhs[i]][rows, :], operand(i),
                             preferred_element_type=F32) for i in range(n_w)]
            out_ref[rows, :] = epilogue(prods, [e[rows, :] for e in extra_refs]).astype(out_ref.dtype)
        if n_live < tm:
            out_ref[n_live:tm, :] = jnp.zeros((tm - n_live, out_ref.shape[1]), out_ref.dtype)

    if sub_rows is None:
        compute(tm)
    else:
        live_chunks = (tvalid_ref[r] + (sub_rows - 1)) // sub_rows
        for c in range(tm // sub_rows + 1):
            pl.when(live_chunks == c)(functools.partial(compute, c * sub_rows))


def ws_matmul(lhs, ws, w_lhs, extras, extra_col_off, epilogue, out_dtype, tn, texp=None,
              tvalid=None, sub_rows=None, w_base=0, col_scale=None, round_once=False,
              tm=ROW_TILE, name="ws_matmul"):
    m = lhs[0].shape[0]
    n_total = ws[0].shape[2]
    n_rows = m // tm
    if texp is None:
        texp = jnp.zeros((n_rows,), jnp.int32)
    if tvalid is None:
        tvalid = jnp.full((n_rows,), tm, jnp.int32)
    no_next = jnp.iinfo(jnp.int32).max
    later = jnp.min(jnp.where(texp[None, :] > texp[:, None], texp[None, :], no_next), axis=1)
    tnext = jnp.where(later == no_next, -1, later + w_base).astype(jnp.int32)
    texp = texp + w_base
    in_specs = []
    for a in lhs:
        in_specs.append(pl.BlockSpec((tm, a.shape[1]), lambda n, r, *_: (r, 0)))
    for w in ws:
        in_specs.append(pl.BlockSpec(memory_space=pl.ANY))
    for off in extra_col_off:
        in_specs.append(pl.BlockSpec((tm, tn), lambda n, r, *_, off=off: (r, off + n)))
    scale_args = []
    if col_scale is not None:
        in_specs.append(pl.BlockSpec((1, tn), lambda n, r, *_: (0, n)))
        scale_args.append(col_scale.reshape(1, n_total))
    kern = functools.partial(_ws_kernel, n_lhs=len(lhs), w_lhs=tuple(w_lhs),
                             n_extra=len(extras), epilogue=epilogue, sub_rows=sub_rows, tn=tn,
                             scaled=col_scale is not None, round_once=round_once)
    if round_once:
        scratch = ([pltpu.VMEM((w.shape[1], tn), F32) for w in ws]
                   + [pltpu.VMEM((w.shape[1], tn), BF16) for w in ws]
                   + [pltpu.SemaphoreType.DMA((len(ws),))])
    else:
        scratch = ([pltpu.VMEM((2, w.shape[1], tn), F32) for w in ws]
                   + [pltpu.SemaphoreType.DMA((len(ws), 2)), pltpu.SMEM((1,), jnp.int32)])
    return pl.pallas_call(
        kern,
        out_shape=jax.ShapeDtypeStruct((m, n_total), out_dtype),
        grid_spec=pltpu.PrefetchScalarGridSpec(
            num_scalar_prefetch=3,
            grid=(n_total // tn, n_rows),
            in_specs=in_specs,
            out_specs=pl.BlockSpec((tm, tn), lambda n, r, *_: (r, n)),
            scratch_shapes=scratch,
        ),
        compiler_params=_cparams(("arbitrary", "arbitrary")),
        name=name,
    )(texp, tnext, tvalid, *lhs, *ws, *extras, *scale_args)


def _ep_plain(prods, extras):
    return prods[0]


def _ep_residual(prods, extras):
    return extras[0] + prods[0]


def _ep_swiglu(prods, extras):
    g, u = prods
    return (g * jax.nn.sigmoid(g)) * u


def _ep_merge(prods, extras):
    ga, gb = extras
    return (jax.nn.sigmoid(ga.astype(F32)) * prods[0]
            + jax.nn.sigmoid(gb.astype(F32)) * prods[1])


def _attn_kernel(q_ref, k_ref, v_ref, o_ref, *, nb, blk, topk):
    seq = k_ref.shape[0]
    neg_inf = jnp.float32(-jnp.inf)
    avg = jnp.where(lax.broadcasted_iota(jnp.int32, (nb, seq), 1) // blk
                    == lax.broadcasted_iota(jnp.int32, (nb, seq), 0), 1.0 / blk, 0.0).astype(BF16)
    k_mean = jnp.dot(avg, k_ref[...], preferred_element_type=F32)
    k_mean_hi = k_mean.astype(BF16).astype(F32)
    k_mean_parts = jnp.concatenate([k_mean_hi, k_mean - k_mean_hi], axis=0).astype(BF16)
    blk_id = lax.broadcasted_iota(jnp.int32, (nb, blk), 0)
    causal = (lax.broadcasted_iota(jnp.int32, (blk, blk), 0)
              <= lax.broadcasted_iota(jnp.int32, (blk, blk), 1))
    nt_dims = (((1,), (1,)), ((), ()))
    v_t = v_ref[...].T

    def scores(n):
        qn = q_ref[n * blk:(n + 1) * blk, :]
        s = lax.dot_general(k_ref[0:(n + 1) * blk, :], qn, nt_dims, preferred_element_type=F32)
        gate_parts = None
        if n > topk:
            gate_parts = lax.dot_general(k_mean_parts, qn, nt_dims, preferred_element_type=F32)
        return s, gate_parts

    def masked_softmax(n, s, gate_parts):
        past = [s[j * blk:(j + 1) * blk, :] for j in range(n)]
        if n > topk:
            gate = gate_parts[:nb] + gate_parts[nb:]
            beaten = jnp.zeros((nb, blk), F32)
            for j in range(n):
                gj = gate[j:j + 1, :]
                wins = jnp.logical_or(gj > gate, jnp.logical_and(gj == gate, j < blk_id))
                beaten = beaten + wins.astype(F32)
            bias = jnp.where(beaten < topk, 0.0, neg_inf)
            past = [past[j] + bias[j:j + 1, :] for j in range(n)]
        own = jnp.where(causal, s[n * blk:(n + 1) * blk, :], neg_inf)
        s = jnp.concatenate(past + [own], axis=0)
        m = jnp.max(s, axis=0, keepdims=True)
        p = jnp.exp2(s - m)
        return p.astype(BF16), jnp.sum(p, axis=0, keepdims=True)

    def weighted_values(n, p, l):
        o_t = jnp.dot(v_t[:, 0:(n + 1) * blk], p, preferred_element_type=F32)
        o_ref[n * blk:(n + 1) * blk, :] = (o_t / l).T.astype(o_ref.dtype)

    order = list(range(nb - 1, -1, -1))
    s_out, p_out = {}, {}
    for t in range(nb + SOFTMAX_LAG + VALUES_LAG):
        if t < nb:
            s_out[order[t]] = scores(order[t])
        if 0 <= t - SOFTMAX_LAG < nb:
            n = order[t - SOFTMAX_LAG]
            p_out[n] = masked_softmax(n, *s_out.pop(n))
        if 0 <= t - SOFTMAX_LAG - VALUES_LAG < nb:
            n = order[t - SOFTMAX_LAG - VALUES_LAG]
            weighted_values(n, *p_out.pop(n))


def moba_attention(z, batch, seq):
    nb = seq // MOBA_BLOCK
    kern = functools.partial(_attn_kernel, nb=nb, blk=MOBA_BLOCK, topk=MOBA_TOPK)
    blockspec = lambda off: pl.BlockSpec((None, seq, HEAD_DIM), lambda b, h: (b, 0, off + h))
    return pl.pallas_call(
        kern,
        out_shape=jax.ShapeDtypeStruct((batch, seq, ATTN_HEADS * HEAD_DIM), BF16),
        grid=(batch, ATTN_HEADS),
        in_specs=[blockspec(0), blockspec(ATTN_HEADS), blockspec(2 * ATTN_HEADS)],
        out_specs=pl.BlockSpec((None, seq, HEAD_DIM), lambda b, h: (b, 0, h)),
        compiler_params=_cparams(("parallel", "parallel")),
        name="moba_attention",
    )(z, z, z)


def _gelu_tanh(x):
    return 0.5 * x * (1.0 + jnp.tanh(0.7978845608028654 * (x + 0.044715 * (x * x * x))))


def _sgu_kernel(u_ref, v_ref, g_ref, w_ref, bt_ref, o_ref, *, chunk, groups, gd):
    tm = u_ref.shape[0]
    u = _gelu_tanh(u_ref[...].astype(F32))
    v = _gelu_tanh(v_ref[...].astype(F32))
    mu = jnp.mean(v, axis=-1, keepdims=True)
    vc = v - mu
    var = jnp.mean(vc * vc, axis=-1, keepdims=True)
    vn = (vc * lax.rsqrt(var + NORM_EPS) * g_ref[...]).astype(BF16)
    t_i = lax.broadcasted_iota(jnp.int32, (chunk, chunk), 0)
    s_i = lax.broadcasted_iota(jnp.int32, (chunk, chunk), 1)
    lower = s_i <= t_i
    for g in range(groups):
        w = jnp.where(lower, w_ref[g], 0.0).astype(BF16)
        bias = bt_ref[:, g:g + 1]
        cols = slice(g * gd, (g + 1) * gd)
        for c in range(tm // chunk):
            rows = slice(c * chunk, (c + 1) * chunk)
            mixed = jnp.dot(w, vn[rows, cols], preferred_element_type=F32) + bias
            o_ref[rows, cols] = (u[rows, cols] * mixed).astype(o_ref.dtype)


def spatial_gating(z, g_sgu, w_s, b_s, col_u, col_v, tm=256):
    n = z.shape[0]
    width = SGU_GROUPS * SGU_GROUP_DIM
    kern = functools.partial(_sgu_kernel, chunk=SGU_CHUNK, groups=SGU_GROUPS, gd=SGU_GROUP_DIM)
    return pl.pallas_call(
        kern,
        out_shape=jax.ShapeDtypeStruct((n, width), BF16),
        grid=(n // tm,),
        in_specs=[pl.BlockSpec((tm, width), lambda i: (i, col_u)),
                  pl.BlockSpec((tm, width), lambda i: (i, col_v)),
                  pl.BlockSpec((1, width), lambda i: (0, 0)),
                  pl.BlockSpec(w_s.shape, lambda i: (0, 0, 0)),
                  pl.BlockSpec((SGU_CHUNK, SGU_GROUPS), lambda i: (0, 0))],
        out_specs=pl.BlockSpec((tm, width), lambda i: (i, 0)),
        compiler_params=_cparams(("parallel",)),
        name="spatial_gating",
    )(z, z, g_sgu.reshape(1, width), w_s, b_s.T)


def _router_kernel(x_ref, g_ref, rwt_ref, rb_ref, eidx_ref, wts_ref, rank_ref, cnt_ref):
    i = pl.program_id(0)
    ne = rwt_ref.shape[0]
    tm = x_ref.shape[0]

    @pl.when(i == 0)
    def _():
        cnt_ref[...] = jnp.zeros_like(cnt_ref)

    h = _rms(x_ref[...], g_ref[...])
    h_hi = h.astype(BF16)
    h_lo = (h - h_hi.astype(F32)).astype(BF16)
    rw = rwt_ref[...]
    rw_hi = rw.astype(BF16)
    rw_hi_f32 = rw_hi.astype(F32)
    rw_parts = jnp.concatenate([rw_hi_f32, rw - rw_hi_f32], axis=0).astype(BF16)
    nt_dims = (((1,), (1,)), ((), ()))
    both = lax.dot_general(rw_parts, h_hi, nt_dims, preferred_element_type=F32)
    logits = (both[:ne] + both[ne:]
              + lax.dot_general(rw_hi, h_lo, nt_dims, preferred_element_type=F32)
              + rb_ref[...])
    row = lax.broadcasted_iota(jnp.int32, (ne, tm), 0).astype(F32)
    neg_inf = jnp.float32(-jnp.inf)
    m0 = jnp.max(logits, axis=0, keepdims=True)
    i0 = jnp.min(jnp.where(logits == m0, row, float(ne)), axis=0, keepdims=True)
    rest = jnp.where(row == i0, neg_inf, logits)
    m1 = jnp.max(rest, axis=0, keepdims=True)
    i1 = jnp.min(jnp.where(jnp.logical_and(rest == m1, row != i0), row, float(ne)),
                 axis=0, keepdims=True)
    e1 = jnp.exp(m1 - m0)
    denom = 1.0 + e1
    eidx_ref[0:1, :] = i0.astype(jnp.int32)
    eidx_ref[1:2, :] = i1.astype(jnp.int32)
    wts_ref[0:1, :] = 1.0 / denom
    wts_ref[1:2, :] = e1 / denom

    hit0 = row == i0
    hit1 = row == i1
    chosen = jnp.logical_or(hit0, hit1).astype(BF16)
    earlier = (lax.broadcasted_iota(jnp.int32, (tm, tm), 0)
               < lax.broadcasted_iota(jnp.int32, (tm, tm), 1)).astype(BF16)
    before = jnp.dot(chosen, earlier, preferred_element_type=F32) + cnt_ref[:, 0:1]
    rank_ref[0:1, :] = jnp.sum(jnp.where(hit0, before, 0.0), axis=0, keepdims=True).astype(jnp.int32)
    rank_ref[1:2, :] = jnp.sum(jnp.where(hit1, before, 0.0), axis=0, keepdims=True).astype(jnp.int32)
    cnt_ref[...] = cnt_ref[...] + jnp.sum(chosen.astype(F32), axis=1, keepdims=True)


def moe_router(x, g, router_w, router_b, tm=ROW_TILE):
    n, d = x.shape
    ne = router_w.shape[1]
    out_shapes = (jax.ShapeDtypeStruct((TOP_K, n), jnp.int32),
                  jax.ShapeDtypeStruct((TOP_K, n), F32),
                  jax.ShapeDtypeStruct((TOP_K, n), jnp.int32),
                  jax.ShapeDtypeStruct((ne, 128), F32))
    tok_spec = pl.BlockSpec((TOP_K, tm), lambda i: (0, i))
    return pl.pallas_call(
        _router_kernel,
        out_shape=out_shapes,
        grid=(n // tm,),
        in_specs=[pl.BlockSpec((tm, d), lambda i: (i, 0)),
                  pl.BlockSpec((1, d), lambda i: (0, 0)),
                  pl.BlockSpec((ne, d), lambda i: (0, 0)),
                  pl.BlockSpec((ne, 1), lambda i: (0, 0))],
        out_specs=(tok_spec, tok_spec, tok_spec, pl.BlockSpec((ne, 128), lambda i: (0, 0))),
        compiler_params=_cparams(("arbitrary",)),
        name="moe_router",
    )(x, g.reshape(1, d), router_w.T, router_b.reshape(ne, 1))


SUBLANES = 8


def _tile_rows(a):
    return a.reshape(a.shape[0] // SUBLANES, SUBLANES, a.shape[1])


def _start_row_gather(src_hbm, hi_ref, lo_ref, idx_base, n_rows, buf, slot, row_base, sem):
    for j in range(n_rows):
        dst_row = row_base + j
        pltpu.make_async_copy(
            src_hbm.at[hi_ref[idx_base + j], pl.ds(lo_ref[idx_base + j], 1), :],
            buf.at[slot, dst_row // SUBLANES, pl.ds(dst_row % SUBLANES, 1), :],
            sem.at[slot]).start()


def _wait_row_gather(src_hbm, buf, slot, sem):
    pltpu.make_async_copy(src_hbm.at[pl.ds(0, buf.shape[1])], buf.at[slot], sem.at[slot]).wait()


def _dispatch_kernel(hi_ref, lo_ref, x_hbm, g_ref, o_ref, buf, sem, *, tg):
    i = pl.program_id(0)
    last = pl.num_programs(0) - 1

    @pl.when(i == 0)
    def _():
        _start_row_gather(x_hbm, hi_ref, lo_ref, 0, tg, buf, 0, 0, sem)

    slot = jnp.bitwise_and(i, 1)
    _wait_row_gather(x_hbm, buf, slot, sem)
    _start_row_gather(x_hbm, hi_ref, lo_ref, jnp.minimum(i + 1, last) * tg, tg, buf, 1 - slot, 0, sem)
    rows = buf[slot].reshape(tg, buf.shape[3])
    o_ref[...] = _rms(rows, g_ref[...]).astype(o_ref.dtype)

    @pl.when(i == last)
    def _():
        _wait_row_gather(x_hbm, buf, 1 - slot, sem)


def moe_dispatch(x, g, src, tg=256):
    n, d = x.shape
    n_slots = src.shape[0]
    kern = functools.partial(_dispatch_kernel, tg=tg)
    return pl.pallas_call(
        kern,
        out_shape=jax.ShapeDtypeStruct((n_slots, d), BF16),
        grid_spec=pltpu.PrefetchScalarGridSpec(
            num_scalar_prefetch=2,
            grid=(n_slots // tg,),
            in_specs=[pl.BlockSpec(memory_space=pl.ANY),
                      pl.BlockSpec((1, d), lambda i, *_: (0, 0))],
            out_specs=pl.BlockSpec((tg, d), lambda i, *_: (i, 0)),
            scratch_shapes=[pltpu.VMEM((2, tg // SUBLANES, SUBLANES, d), F32),
                            pltpu.SemaphoreType.DMA((2,))],
        ),
        compiler_params=_cparams(("arbitrary",)),
        name="moe_dispatch",
    )(src // SUBLANES, src % SUBLANES, _tile_rows(x), g.reshape(1, d))


def _combine_kernel(hi_ref, lo_ref, y_hbm, x_ref, w_ref, g_ref, o_ref, buf, sem, *, tc, final_norm):
    i = pl.program_id(0)
    last = pl.num_programs(0) - 1
    n_tok = (last + 1) * tc

    def fetch(step, slot):
        for k in range(TOP_K):
            _start_row_gather(y_hbm, hi_ref, lo_ref, k * n_tok + step * tc, tc, buf, slot, k * tc, sem)

    @pl.when(i == 0)
    def _():
        fetch(0, 0)

    slot = jnp.bitwise_and(i, 1)
    _wait_row_gather(y_hbm, buf, slot, sem)
    fetch(jnp.minimum(i + 1, last), 1 - slot)
    rows = buf[slot].reshape(TOP_K * tc, buf.shape[3])
    acc = x_ref[...]
    for k in range(TOP_K):
        acc = acc + w_ref[:, k:k + 1] * rows[k * tc:(k + 1) * tc, :]
    if final_norm:
        acc = _rms(acc, g_ref[...])
    o_ref[...] = acc

    @pl.when(i == last)
    def _():
        _wait_row_gather(y_hbm, buf, 1 - slot, sem)


def moe_combine(y, x, slot, wts, g, final_norm, tc=128):
    n, d = x.shape
    kern = functools.partial(_combine_kernel, tc=tc, final_norm=final_norm)
    flat = slot.reshape(-1)
    return pl.pallas_call(
        kern,
        out_shape=jax.ShapeDtypeStruct((n, d), F32),
        grid_spec=pltpu.PrefetchScalarGridSpec(
            num_scalar_prefetch=2,
            grid=(n // tc,),
            in_specs=[pl.BlockSpec(memory_space=pl.ANY),
                      pl.BlockSpec((tc, d), lambda i, *_: (i, 0)),
                      pl.BlockSpec((tc, TOP_K), lambda i, *_: (i, 0)),
                      pl.BlockSpec((1, d), lambda i, *_: (0, 0))],
            out_specs=pl.BlockSpec((tc, d), lambda i, *_: (i, 0)),
            scratch_shapes=[pltpu.VMEM((2, TOP_K * tc // SUBLANES, SUBLANES, d), F32),
                            pltpu.SemaphoreType.DMA((2,))],
        ),
        compiler_params=_cparams(("arbitrary",)),
        name="moe_combine",
    )(flat // SUBLANES, flat % SUBLANES, _tile_rows(y), x, wts.T, g.reshape(1, d))


def token_mixer(x, batch, seq, layer, norm_g, w_in, g_sgu, w_s, b_s, w_pa, w_pb, w_o):
    n, d = x.shape
    aw = ATTN_HEADS * HEAD_DIM
    sw = SGU_GROUPS * SGU_GROUP_DIM
    tn = 1024
    h = rmsnorm(x, norm_g, BF16)
    q_scale = jnp.where(jnp.arange(w_in.shape[2]) < aw, HEAD_DIM ** -0.5 * LOG2_E, 1.0).astype(F32)
    z = ws_matmul([h], [w_in], [0], [], [], _ep_plain, BF16, tn, w_base=layer,
                  col_scale=q_scale, tm=WIDE_ROW_TILE, name="in_proj")
    attn = moba_attention(z.reshape(batch, seq, z.shape[1]), batch, seq).reshape(n, aw)
    sgu = spatial_gating(z, g_sgu, w_s, b_s, (3 * aw) // sw, (3 * aw + sw) // sw)
    gate_col = (3 * aw + 2 * sw) // tn
    merged = ws_matmul([attn, sgu], [w_pa, w_pb], [0, 1], [z, z],
                       [gate_col, gate_col + d // tn], _ep_merge, BF16, tn, w_base=layer,
                       tm=DENSE_ROW_TILE, name="branch_merge")
    return ws_matmul([merged], [w_o], [0], [x], [0], _ep_residual, F32, tn, w_base=layer,
                     tm=DENSE_ROW_TILE, name="out_proj")


def dense_ffn(x, norm_g, j, wg, wu, wd):
    h = rmsnorm(x, norm_g, BF16)
    act = ws_matmul([h], [wg, wu], [0, 0], [], [], _ep_swiglu, BF16, 512, w_base=j,
                    tm=DENSE_ROW_TILE, name="ffn_up")
    return ws_matmul([act], [wd], [0], [x], [0], _ep_residual, F32, 512, w_base=j,
                     round_once=True, tm=DENSE_ROW_TILE, name="ffn_down")


def moe_ffn(x, norm_g, j, router_w, router_b, wg, wu, wd, final_g, final_norm):
    n, d = x.shape
    ne = router_w.shape[1]
    tm = ROW_TILE
    eidx, wts, rank, cnt = moe_router(x, norm_g, router_w, router_b)
    counts = cnt[:, 0].astype(jnp.int32)
    padded = ((counts + tm - 1) // tm) * tm
    ends = jnp.cumsum(padded)
    starts = ends - padded
    group_start = jnp.sum(jnp.where(eidx[..., None] == jnp.arange(ne), starts, 0), axis=-1)
    slot = group_start + rank
    n_tiles = (TOP_K * n) // tm + ne
    tok = jnp.tile(jnp.arange(n, dtype=jnp.int32), TOP_K)
    src = (jnp.arange(n_tiles * tm, dtype=jnp.int32) % n).at[slot.reshape(-1)].set(tok)
    tile_row = jnp.arange(n_tiles, dtype=jnp.int32) * tm
    last_used = jnp.max(jnp.where(counts > 0, jnp.arange(ne, dtype=jnp.int32), 0))
    texp = jnp.minimum(jnp.sum(tile_row[:, None] >= ends[None, :], axis=1), last_used).astype(jnp.int32)
    group_end = jnp.sum(jnp.where(texp[:, None] == jnp.arange(ne), starts + counts, 0), axis=-1)
    tvalid = jnp.clip(group_end - tile_row, 0, tm).astype(jnp.int32)

    hs = moe_dispatch(x, norm_g, src)
    act = ws_matmul([hs], [wg, wu], [0, 0], [], [], _ep_swiglu, BF16, 512, texp=texp,
                    tvalid=tvalid, sub_rows=MOE_SUB_ROWS, w_base=j * ne, name="moe_up")
    y = ws_matmul([act], [wd], [0], [], [], _ep_plain, F32, 1024, texp=texp, tvalid=tvalid,
                  sub_rows=MOE_SUB_ROWS, w_base=j * ne, round_once=True, name="moe_down")
    return moe_combine(y, x, slot, wts, final_g, final_norm)


def kernel(x, mix_norm_g, w_in, sgu_norm_g, w_s, b_s, w_pa, w_pb, w_o, ffn_norm_g,
           dense_w_gate, dense_w_up, dense_w_down, router_w, router_b,
           expert_w_gate, expert_w_up, expert_w_down, final_norm_g):
    batch, seq, d = x.shape
    depth = mix_norm_g.shape[0]
    xf = x.reshape(batch * seq, d)
    merge_experts = lambda w: w.reshape((w.shape[0] * w.shape[1],) + w.shape[2:])
    ewg, ewu, ewd = (merge_experts(w) for w in (expert_w_gate, expert_w_up, expert_w_down))
    normed = False
    for i in range(depth):
        xf = token_mixer(xf, batch, seq, i, mix_norm_g[i], w_in, sgu_norm_g[i], w_s[i], b_s[i],
                         w_pa, w_pb, w_o)
        j = i // 2
        last = i == depth - 1
        if i % 2 == 0:
            xf = dense_ffn(xf, ffn_norm_g[i], j, dense_w_gate, dense_w_up, dense_w_down)
        else:
            xf = moe_ffn(xf, ffn_norm_g[i], j, router_w[j], router_b[j], ewg, ewu, ewd,
                         final_norm_g, last)
            normed = last
    if not normed:
        xf = rmsnorm(xf, final_norm_g, F32)
    return xf.reshape(batch, seq, d)
```

```python
import functools

import jax
import jax.numpy as jnp
from jax import lax
from jax.experimental import pallas as pl
from jax.experimental.pallas import tpu as pltpu

F32 = jnp.float32
BF16 = jnp.bfloat16

ATTN_HEADS = 8
HEAD_DIM = 128
MOBA_BLOCK = 256
MOBA_TOPK = 3
SGU_GROUPS = 8
SGU_GROUP_DIM = 128
SGU_CHUNK = 128
TOP_K = 2
NORM_EPS = 1e-6
STAT_LANES = 128
LOG2_E = 1.4426950408889634

V7X_VMEM_BYTES = 64 * 1024 * 1024
VMEM_LIMIT_BYTES = V7X_VMEM_BYTES - 8 * 1024 * 1024

ROW_TILE = 512
DENSE_ROW_TILE = 1024
WIDE_ROW_TILE = 2048
CAST_ROWS = 256
MOE_SUB_ROWS = 128
SOFTMAX_LAG = 2
VALUES_LAG = 1


def _cparams(sem):
    return pltpu.CompilerParams(dimension_semantics=sem, vmem_limit_bytes=VMEM_LIMIT_BYTES)


def _rms(x, g):
    ms = jnp.mean(x * x, axis=-1, keepdims=True)
    return x * lax.rsqrt(ms + NORM_EPS) * g


def _rmsnorm_kernel(x_ref, g_ref, o_ref):
    o_ref[...] = _rms(x_ref[...], g_ref[...]).astype(o_ref.dtype)


def rmsnorm(x, g, out_dtype, tm=ROW_TILE):
    n, d = x.shape
    return pl.pallas_call(
        _rmsnorm_kernel,
        out_shape=jax.ShapeDtypeStruct((n, d), out_dtype),
        grid=(n // tm,),
        in_specs=[pl.BlockSpec((tm, d), lambda i: (i, 0)),
                  pl.BlockSpec((1, d), lambda i: (0, 0))],
        out_specs=pl.BlockSpec((tm, d), lambda i: (i, 0)),
        compiler_params=_cparams(("parallel",)),
        name="rmsnorm",
    )(x, g.reshape(1, d))


def _prenorm_kernel(x_ref, g_ref, o_ref, s_ref):
    x = x_ref[...]
    o_ref[...] = (x * g_ref[...]).astype(o_ref.dtype)
    s_ref[...] = jnp.broadcast_to(jnp.sum(x * x, axis=1, keepdims=True), s_ref.shape)


def prenorm(x, g, tm=ROW_TILE):
    n, d = x.shape
    return pl.pallas_call(
        _prenorm_kernel,
        out_shape=(jax.ShapeDtypeStruct((n, d), BF16), jax.ShapeDtypeStruct((n, STAT_LANES), F32)),
        grid=(n // tm,),
        in_specs=[pl.BlockSpec((tm, d), lambda i: (i, 0)),
                  pl.BlockSpec((1, d), lambda i: (0, 0))],
        out_specs=(pl.BlockSpec((tm, d), lambda i: (i, 0)),
                   pl.BlockSpec((tm, STAT_LANES), lambda i: (i, 0))),
        compiler_params=_cparams(("parallel",)),
        name="prenorm",
    )(x, g.reshape(1, d))


def _ws_kernel(texp_ref, tnext_ref, tvalid_ref, *refs, n_lhs, w_lhs, n_extra, epilogue,
               sub_rows, tn, scaled, round_once, normed_lhs, emit_prenorm):
    n_w = len(w_lhs)
    lhs_refs = refs[:n_lhs]
    w_hbm = refs[n_lhs:n_lhs + n_w]
    extra_refs = refs[n_lhs + n_w:n_lhs + n_w + n_extra]
    pos = n_lhs + n_w + n_extra
    scale_ref = refs[pos] if scaled else None
    pos += int(scaled)
    stats_ref = refs[pos] if normed_lhs else None
    pos += int(normed_lhs)
    gain_ref = refs[pos] if emit_prenorm else None
    pos += int(emit_prenorm)
    out_ref = refs[pos]
    prenorm_ref, sumsq_ref = refs[pos + 1:pos + 3] if emit_prenorm else (None, None)
    scratch = refs[pos + 1 + 2 * int(emit_prenorm):]
    wf_refs = scratch[:n_w]
    tm = out_ref.shape[0]
    if round_once:
        wb_refs = scratch[n_w:2 * n_w]
        sem = scratch[2 * n_w]
    else:
        sem, groups_seen = scratch[n_w:]

    n = pl.program_id(0)
    r = pl.program_id(1)
    n_col_tiles = pl.num_programs(0)

    def weight_copy(i, index, col_tile, slot):
        cols = pl.ds(pl.multiple_of(col_tile * tn, tn), tn)
        dst, dst_sem = (wf_refs[i], sem.at[i]) if round_once else (wf_refs[i].at[slot], sem.at[i, slot])
        return pltpu.make_async_copy(w_hbm[i].at[index, :, cols], dst, dst_sem)

    def start_weights(index, col_tile, slot):
        for i in range(n_w):
            weight_copy(i, index, col_tile, slot).start()

    def start_following(slot):
        following = tnext_ref[r]

        @pl.when(following >= 0)
        def _():
            start_weights(following, n, slot)

        @pl.when(jnp.logical_and(following < 0, n + 1 < n_col_tiles))
        def _():
            start_weights(texp_ref[0], n + 1, slot)

    def rounded(w):
        if scaled:
            w = w * scale_ref[...]
        return w.astype(BF16)

    @pl.when(jnp.logical_and(n == 0, r == 0))
    def _():
        if not round_once:
            groups_seen[0] = 0
        start_weights(texp_ref[0], 0, 0)

    new_group = jnp.logical_or(r == 0, texp_ref[r] != texp_ref[jnp.maximum(r - 1, 0)])

    @pl.when(new_group)
    def _():
        if round_once:
            for i in range(n_w):
                weight_copy(i, texp_ref[r], n, 0).wait()
            for wf_ref, wb_ref in zip(wf_refs, wb_refs):
                def cast_rows(c, carry, wf_ref=wf_ref, wb_ref=wb_ref):
                    rows = pl.ds(pl.multiple_of(c * CAST_ROWS, CAST_ROWS), CAST_ROWS)
                    wb_ref[rows, :] = rounded(wf_ref[rows, :])
                    return carry

                lax.fori_loop(0, wb_ref.shape[0] // CAST_ROWS, cast_rows, 0)
            start_following(0)
        else:
            seen = groups_seen[0]
            slot = jnp.bitwise_and(seen, 1)
            groups_seen[0] = seen + 1
            for i in range(n_w):
                weight_copy(i, texp_ref[r], n, slot).wait()
            start_following(1 - slot)

    def operand(i):
        if round_once:
            return wb_refs[i][...]
        return rounded(wf_refs[i][jnp.bitwise_and(groups_seen[0] - 1, 1)])

    def compute(n_live):
        rows = slice(0, n_live)
        if n_live > 0:
            prods = [jnp.dot(lhs_refs[w_lhs[i]][rows, :], operand(i),
                             preferred_element_type=F32) for i in range(n_w)]
            if normed_lhs:
                width = lhs_refs[0].shape[1]
                sumsq = sum(stats_ref[rows, p:p + 1] for p in range(0, stats_ref.shape[1], STAT_LANES))
                inv_rms = lax.rsqrt(sumsq / width + NORM_EPS)
                prods = [p * inv_rms for p in prods]
            res = epilogue(prods, [e[rows, :] for e in extra_refs])
            out_ref[rows, :] = res.astype(out_ref.dtype)
            if emit_prenorm:
                prenorm_ref[rows, :] = (res * gain_ref[...]).astype(prenorm_ref.dtype)
                sumsq_ref[rows, :] = jnp.broadcast_to(jnp.sum(res * res, axis=1, keepdims=True),
                                                      (n_live, STAT_LANES))
        if n_live < tm:
            for ref in (out_ref, prenorm_ref, sumsq_ref):
                if ref is not None:
                    ref[n_live:tm, :] = jnp.zeros((tm - n_live, ref.shape[1]), ref.dtype)

    if sub_rows is None:
        compute(tm)
    else:
        live_chunks = (tvalid_ref[r] + (sub_rows - 1)) // sub_rows
        for c in range(tm // sub_rows + 1):
            pl.when(live_chunks == c)(functools.partial(compute, c * sub_rows))


def ws_matmul(lhs, ws, w_lhs, extras, extra_col_off, epilogue, out_dtype, tn, texp=None,
              tvalid=None, sub_rows=None, w_base=0, col_scale=None, round_once=False,
              row_sumsq=None, next_gain=None, tm=ROW_TILE, name="ws_matmul"):
    m = lhs[0].shape[0]
    n_total = ws[0].shape[2]
    n_rows = m // tm
    if texp is None:
        texp = jnp.zeros((n_rows,), jnp.int32)
    if tvalid is None:
        tvalid = jnp.full((n_rows,), tm, jnp.int32)
    no_next = jnp.iinfo(jnp.int32).max
    later = jnp.min(jnp.where(texp[None, :] > texp[:, None], texp[None, :], no_next), axis=1)
    tnext = jnp.where(later == no_next, -1, later + w_base).astype(jnp.int32)
    texp = texp + w_base
    in_specs = []
    for a in lhs:
        in_specs.append(pl.BlockSpec((tm, a.shape[1]), lambda n, r, *_: (r, 0)))
    for w in ws:
        in_specs.append(pl.BlockSpec(memory_space=pl.ANY))
    for off in extra_col_off:
        in_specs.append(pl.BlockSpec((tm, tn), lambda n, r, *_, off=off: (r, off + n)))
    scale_args = []
    if col_scale is not None:
        in_specs.append(pl.BlockSpec((1, tn), lambda n, r, *_: (0, n)))
        scale_args.append(col_scale.reshape(1, n_total))
    if row_sumsq is not None:
        in_specs.append(pl.BlockSpec((tm, row_sumsq.shape[1]), lambda n, r, *_: (r, 0)))
        scale_args.append(row_sumsq)
    out_shapes = [jax.ShapeDtypeStruct((m, n_total), out_dtype)]
    out_specs = [pl.BlockSpec((tm, tn), lambda n, r, *_: (r, n))]
    if next_gain is not None:
        in_specs.append(pl.BlockSpec((1, tn), lambda n, r, *_: (0, n)))
        scale_args.append(next_gain.reshape(1, n_total))
        out_shapes += [jax.ShapeDtypeStruct((m, n_total), BF16),
                       jax.ShapeDtypeStruct((m, (n_total // tn) * STAT_LANES), F32)]
        out_specs += [pl.BlockSpec((tm, tn), lambda n, r, *_: (r, n)),
                      pl.BlockSpec((tm, STAT_LANES), lambda n, r, *_: (r, n))]
    kern = functools.partial(_ws_kernel, n_lhs=len(lhs), w_lhs=tuple(w_lhs),
                             n_extra=len(extras), epilogue=epilogue, sub_rows=sub_rows, tn=tn,
                             scaled=col_scale is not None, round_once=round_once,
                             normed_lhs=row_sumsq is not None, emit_prenorm=next_gain is not None)
    if round_once:
        scratch = ([pltpu.VMEM((w.shape[1], tn), F32) for w in ws]
                   + [pltpu.VMEM((w.shape[1], tn), BF16) for w in ws]
                   + [pltpu.SemaphoreType.DMA((len(ws),))])
    else:
        scratch = ([pltpu.VMEM((2, w.shape[1], tn), F32) for w in ws]
                   + [pltpu.SemaphoreType.DMA((len(ws), 2)), pltpu.SMEM((1,), jnp.int32)])
    outs = pl.pallas_call(
        kern,
        out_shape=out_shapes,
        grid_spec=pltpu.PrefetchScalarGridSpec(
            num_scalar_prefetch=3,
            grid=(n_total // tn, n_rows),
            in_specs=in_specs,
            out_specs=out_specs,
            scratch_shapes=scratch,
        ),
        compiler_params=_cparams(("arbitrary", "arbitrary")),
        name=name,
    )(texp, tnext, tvalid, *lhs, *ws, *extras, *scale_args)
    return outs if next_gain is not None else outs[0]


def _ep_plain(prods, extras):
    return prods[0]


def _ep_residual(prods, extras):
    return extras[0] + prods[0]


def _ep_swiglu(prods, extras):
    g, u = prods
    return (g * jax.nn.sigmoid(g)) * u


def _ep_merge(prods, extras):
    ga, gb = extras
    return (jax.nn.sigmoid(ga.astype(F32)) * prods[0]
            + jax.nn.sigmoid(gb.astype(F32)) * prods[1])


def _attn_kernel(q_ref, k_ref, v_ref, o_ref, *, nb, blk, topk):
    seq = k_ref.shape[0]
    neg_inf = jnp.float32(-jnp.inf)
    avg = jnp.where(lax.broadcasted_iota(jnp.int32, (nb, seq), 1) // blk
                    == lax.broadcasted_iota(jnp.int32, (nb, seq), 0), 1.0 / blk, 0.0).astype(BF16)
    k_mean = jnp.dot(avg, k_ref[...], preferred_element_type=F32)
    k_mean_hi = k_mean.astype(BF16).astype(F32)
    k_mean_parts = jnp.concatenate([k_mean_hi, k_mean - k_mean_hi], axis=0).astype(BF16)
    blk_id = lax.broadcasted_iota(jnp.int32, (nb, blk), 0)
    causal = (lax.broadcasted_iota(jnp.int32, (blk, blk), 0)
              <= lax.broadcasted_iota(jnp.int32, (blk, blk), 1))
    nt_dims = (((1,), (1,)), ((), ()))
    v_t = v_ref[...].T

    def scores(n):
        qn = q_ref[n * blk:(n + 1) * blk, :]
        s = lax.dot_general(k_ref[0:(n + 1) * blk, :], qn, nt_dims, preferred_element_type=F32)
        gate_parts = None
        if n > topk:
            gate_parts = lax.dot_general(k_mean_parts, qn, nt_dims, preferred_element_type=F32)
        return s, gate_parts

    def masked_softmax(n, s, gate_parts):
        past = [s[j * blk:(j + 1) * blk, :] for j in range(n)]
        if n > topk:
            gate = gate_parts[:nb] + gate_parts[nb:]
            beaten = jnp.zeros((nb, blk), F32)
            for j in range(n):
                gj = gate[j:j + 1, :]
                wins = jnp.logical_or(gj > gate, jnp.logical_and(gj == gate, j < blk_id))
                beaten = beaten + wins.astype(F32)
            bias = jnp.where(beaten < topk, 0.0, neg_inf)
            past = [past[j] + bias[j:j + 1, :] for j in range(n)]
        own = jnp.where(causal, s[n * blk:(n + 1) * blk, :], neg_inf)
        s = jnp.concatenate(past + [own], axis=0)
        m = jnp.max(s, axis=0, keepdims=True)
        p = jnp.exp2(s - m)
        return p.astype(BF16), jnp.sum(p, axis=0, keepdims=True)

    def weighted_values(n, p, l):
        o_t = jnp.dot(v_t[:, 0:(n + 1) * blk], p, preferred_element_type=F32)
        o_ref[n * blk:(n + 1) * blk, :] = (o_t / l).T.astype(o_ref.dtype)

    order = list(range(nb - 1, -1, -1))
    s_out, p_out = {}, {}
    for t in range(nb + SOFTMAX_LAG + VALUES_LAG):
        if t < nb:
            s_out[order[t]] = scores(order[t])
        if 0 <= t - SOFTMAX_LAG < nb:
            n = order[t - SOFTMAX_LAG]
            p_out[n] = masked_softmax(n, *s_out.pop(n))
        if 0 <= t - SOFTMAX_LAG - VALUES_LAG < nb:
            n = order[t - SOFTMAX_LAG - VALUES_LAG]
            weighted_values(n, *p_out.pop(n))


def moba_attention(z, batch, seq):
    nb = seq // MOBA_BLOCK
    kern = functools.partial(_attn_kernel, nb=nb, blk=MOBA_BLOCK, topk=MOBA_TOPK)
    blockspec = lambda off: pl.BlockSpec((None, seq, HEAD_DIM), lambda b, h: (b, 0, off + h))
    return pl.pallas_call(
        kern,
        out_shape=jax.ShapeDtypeStruct((batch, seq, ATTN_HEADS * HEAD_DIM), BF16),
        grid=(batch, ATTN_HEADS),
        in_specs=[blockspec(0), blockspec(ATTN_HEADS), blockspec(2 * ATTN_HEADS)],
        out_specs=pl.BlockSpec((None, seq, HEAD_DIM), lambda b, h: (b, 0, h)),
        compiler_params=_cparams(("parallel", "parallel")),
        name="moba_attention",
    )(z, z, z)


def _gelu_tanh(x):
    return 0.5 * x * (1.0 + jnp.tanh(0.7978845608028654 * (x + 0.044715 * (x * x * x))))


def _sgu_kernel(u_ref, v_ref, g_ref, w_ref, bt_ref, o_ref, *, chunk, groups, gd):
    tm = u_ref.shape[0]
    u = _gelu_tanh(u_ref[...].astype(F32))
    v = _gelu_tanh(v_ref[...].astype(F32))
    mu = jnp.mean(v, axis=-1, keepdims=True)
    vc = v - mu
    var = jnp.mean(vc * vc, axis=-1, keepdims=True)
    vn = (vc * lax.rsqrt(var + NORM_EPS) * g_ref[...]).astype(BF16)
    t_i = lax.broadcasted_iota(jnp.int32, (chunk, chunk), 0)
    s_i = lax.broadcasted_iota(jnp.int32, (chunk, chunk), 1)
    lower = s_i <= t_i
    for g in range(groups):
        w = jnp.where(lower, w_ref[g], 0.0).astype(BF16)
        bias = bt_ref[:, g:g + 1]
        cols = slice(g * gd, (g + 1) * gd)
        for c in range(tm // chunk):
            rows = slice(c * chunk, (c + 1) * chunk)
            mixed = jnp.dot(w, vn[rows, cols], preferred_element_type=F32) + bias
            o_ref[rows, cols] = (u[rows, cols] * mixed).astype(o_ref.dtype)


def spatial_gating(z, g_sgu, w_s, b_s, col_u, col_v, tm=256):
    n = z.shape[0]
    width = SGU_GROUPS * SGU_GROUP_DIM
    kern = functools.partial(_sgu_kernel, chunk=SGU_CHUNK, groups=SGU_GROUPS, gd=SGU_GROUP_DIM)
    return pl.pallas_call(
        kern,
        out_shape=jax.ShapeDtypeStruct((n, width), BF16),
        grid=(n // tm,),
        in_specs=[pl.BlockSpec((tm, width), lambda i: (i, col_u)),
                  pl.BlockSpec((tm, width), lambda i: (i, col_v)),
                  pl.BlockSpec((1, width), lambda i: (0, 0)),
                  pl.BlockSpec(w_s.shape, lambda i: (0, 0, 0)),
                  pl.BlockSpec((SGU_CHUNK, SGU_GROUPS), lambda i: (0, 0))],
        out_specs=pl.BlockSpec((tm, width), lambda i: (i, 0)),
        compiler_params=_cparams(("parallel",)),
        name="spatial_gating",
    )(z, z, g_sgu.reshape(1, width), w_s, b_s.T)


def _router_kernel(x_ref, g_ref, rwt_ref, rb_ref, eidx_ref, wts_ref, rank_ref, cnt_ref):
    i = pl.program_id(0)
    ne = rwt_ref.shape[0]
    tm = x_ref.shape[0]

    @pl.when(i == 0)
    def _():
        cnt_ref[...] = jnp.zeros_like(cnt_ref)

    h = _rms(x_ref[...], g_ref[...])
    h_hi = h.astype(BF16)
    h_lo = (h - h_hi.astype(F32)).astype(BF16)
    rw = rwt_ref[...]
    rw_hi = rw.astype(BF16)
    rw_hi_f32 = rw_hi.astype(F32)
    rw_parts = jnp.concatenate([rw_hi_f32, rw - rw_hi_f32], axis=0).astype(BF16)
    nt_dims = (((1,), (1,)), ((), ()))
    both = lax.dot_general(rw_parts, h_hi, nt_dims, preferred_element_type=F32)
    logits = (both[:ne] + both[ne:]
              + lax.dot_general(rw_hi, h_lo, nt_dims, preferred_element_type=F32)
              + rb_ref[...])
    row = lax.broadcasted_iota(jnp.int32, (ne, tm), 0).astype(F32)
    neg_inf = jnp.float32(-jnp.inf)
    m0 = jnp.max(logits, axis=0, keepdims=True)
    i0 = jnp.min(jnp.where(logits == m0, row, float(ne)), axis=0, keepdims=True)
    rest = jnp.where(row == i0, neg_inf, logits)
    m1 = jnp.max(rest, axis=0, keepdims=True)
    i1 = jnp.min(jnp.where(jnp.logical_and(rest == m1, row != i0), row, float(ne)),
                 axis=0, keepdims=True)
    e1 = jnp.exp(m1 - m0)
    denom = 1.0 + e1
    eidx_ref[0:1, :] = i0.astype(jnp.int32)
    eidx_ref[1:2, :] = i1.astype(jnp.int32)
    wts_ref[0:1, :] = 1.0 / denom
    wts_ref[1:2, :] = e1 / denom

    hit0 = row == i0
    hit1 = row == i1
    chosen = jnp.logical_or(hit0, hit1).astype(BF16)
    earlier = (lax.broadcasted_iota(jnp.int32, (tm, tm), 0)
               < lax.broadcasted_iota(jnp.int32, (tm, tm), 1)).astype(BF16)
    before = jnp.dot(chosen, earlier, preferred_element_type=F32) + cnt_ref[:, 0:1]
    rank_ref[0:1, :] = jnp.sum(jnp.where(hit0, before, 0.0), axis=0, keepdims=True).astype(jnp.int32)
    rank_ref[1:2, :] = jnp.sum(jnp.where(hit1, before, 0.0), axis=0, keepdims=True).astype(jnp.int32)
    cnt_ref[...] = cnt_ref[...] + jnp.sum(chosen.astype(F32), axis=1, keepdims=True)


def moe_router(x, g, router_w, router_b, tm=ROW_TILE):
    n, d = x.shape
    ne = router_w.shape[1]
    out_shapes = (jax.ShapeDtypeStruct((TOP_K, n), jnp.int32),
                  jax.ShapeDtypeStruct((TOP_K, n), F32),
                  jax.ShapeDtypeStruct((TOP_K, n), jnp.int32),
                  jax.ShapeDtypeStruct((ne, 128), F32))
    tok_spec = pl.BlockSpec((TOP_K, tm), lambda i: (0, i))
    return pl.pallas_call(
        _router_kernel,
        out_shape=out_shapes,
        grid=(n // tm,),
        in_specs=[pl.BlockSpec((tm, d), lambda i: (i, 0)),
                  pl.BlockSpec((1, d), lambda i: (0, 0)),
                  pl.BlockSpec((ne, d), lambda i: (0, 0)),
                  pl.BlockSpec((ne, 1), lambda i: (0, 0))],
        out_specs=(tok_spec, tok_spec, tok_spec, pl.BlockSpec((ne, 128), lambda i: (0, 0))),
        compiler_params=_cparams(("arbitrary",)),
        name="moe_router",
    )(x, g.reshape(1, d), router_w.T, router_b.reshape(ne, 1))


ISSUE_UNROLL = 8


def _start_row_gather(src_hbm, idx_ref, idx_base, n_rows, buf, slot, row_base, sem):
    def body(j, carry):
        row = idx_ref[idx_base + j]
        pltpu.make_async_copy(src_hbm.at[pl.ds(row, 1), :],
                              buf.at[slot, pl.ds(row_base + j, 1), :], sem.at[slot]).start()
        return carry
    lax.fori_loop(0, n_rows, body, 0, unroll=ISSUE_UNROLL)


def _wait_row_gather(src_hbm, buf, slot, sem):
    n_rows = buf.shape[1]
    pltpu.make_async_copy(src_hbm.at[pl.ds(0, n_rows), :], buf.at[slot], sem.at[slot]).wait()


def _dispatch_kernel(src_ref, x_hbm, g_ref, o_ref, buf, sem, *, tg):
    i = pl.program_id(0)
    nsteps = pl.num_programs(0)

    @pl.when(i == 0)
    def _():
        _start_row_gather(x_hbm, src_ref, 0, tg, buf, 0, 0, sem)

    @pl.when(i + 1 < nsteps)
    def _():
        _start_row_gather(x_hbm, src_ref, (i + 1) * tg, tg, buf, jnp.bitwise_and(i + 1, 1), 0, sem)

    slot = jnp.bitwise_and(i, 1)
    _wait_row_gather(x_hbm, buf, slot, sem)
    o_ref[...] = _rms(buf[slot], g_ref[...]).astype(o_ref.dtype)


def moe_dispatch(x, g, src, tg=256):
    n, d = x.shape
    n_slots = src.shape[0]
    kern = functools.partial(_dispatch_kernel, tg=tg)
    return pl.pallas_call(
        kern,
        out_shape=jax.ShapeDtypeStruct((n_slots, d), BF16),
        grid_spec=pltpu.PrefetchScalarGridSpec(
            num_scalar_prefetch=1,
            grid=(n_slots // tg,),
            in_specs=[pl.BlockSpec(memory_space=pl.ANY),
                      pl.BlockSpec((1, d), lambda i, s: (0, 0))],
            out_specs=pl.BlockSpec((tg, d), lambda i, s: (i, 0)),
            scratch_shapes=[pltpu.VMEM((2, tg, d), F32), pltpu.SemaphoreType.DMA((2,))],
        ),
        compiler_params=_cparams(("arbitrary",)),
        name="moe_dispatch",
    )(src, x, g.reshape(1, d))


def _combine_kernel(slot_ref, y_hbm, x_ref, w_ref, g_ref, o_ref, buf, sem, *, tc, final_norm):
    i = pl.program_id(0)
    nsteps = pl.num_programs(0)
    n_tok = nsteps * tc

    def fetch(step, slot):
        for k in range(TOP_K):
            _start_row_gather(y_hbm, slot_ref, k * n_tok + step * tc, tc, buf, slot, k * tc, sem)

    @pl.when(i == 0)
    def _():
        fetch(0, 0)

    @pl.when(i + 1 < nsteps)
    def _():
        fetch(i + 1, jnp.bitwise_and(i + 1, 1))

    slot = jnp.bitwise_and(i, 1)
    _wait_row_gather(y_hbm, buf, slot, sem)
    acc = x_ref[...]
    for k in range(TOP_K):
        acc = acc + w_ref[:, k:k + 1] * buf[slot, k * tc:(k + 1) * tc, :]
    if final_norm:
        acc = _rms(acc, g_ref[...])
    o_ref[...] = acc


def moe_combine(y, x, slot, wts, g, final_norm, tc=128):
    n, d = x.shape
    kern = functools.partial(_combine_kernel, tc=tc, final_norm=final_norm)
    return pl.pallas_call(
        kern,
        out_shape=jax.ShapeDtypeStruct((n, d), F32),
        grid_spec=pltpu.PrefetchScalarGridSpec(
            num_scalar_prefetch=1,
            grid=(n // tc,),
            in_specs=[pl.BlockSpec(memory_space=pl.ANY),
                      pl.BlockSpec((tc, d), lambda i, s: (i, 0)),
                      pl.BlockSpec((tc, TOP_K), lambda i, s: (i, 0)),
                      pl.BlockSpec((1, d), lambda i, s: (0, 0))],
            out_specs=pl.BlockSpec((tc, d), lambda i, s: (i, 0)),
            scratch_shapes=[pltpu.VMEM((2, TOP_K * tc, d), F32), pltpu.SemaphoreType.DMA((2,))],
        ),
        compiler_params=_cparams(("arbitrary",)),
        name="moe_combine",
    )(slot.reshape(-1), y, x, wts.T, g.reshape(1, d))


def token_mixer(x, pre, batch, seq, layer, w_in, g_sgu, w_s, b_s, w_pa, w_pb, w_o, next_gain):
    n, d = x.shape
    aw = ATTN_HEADS * HEAD_DIM
    sw = SGU_GROUPS * SGU_GROUP_DIM
    tn = 1024
    xg, sumsq = pre
    q_scale = jnp.where(jnp.arange(w_in.shape[2]) < aw, HEAD_DIM ** -0.5 * LOG2_E, 1.0).astype(F32)
    in_tm = WIDE_ROW_TILE if sumsq.shape[1] == STAT_LANES else DENSE_ROW_TILE
    z = ws_matmul([xg], [w_in], [0], [], [], _ep_plain, BF16, tn, w_base=layer,
                  col_scale=q_scale, row_sumsq=sumsq, tm=in_tm, name="in_proj")
    attn = moba_attention(z.reshape(batch, seq, z.shape[1]), batch, seq).reshape(n, aw)
    sgu = spatial_gating(z, g_sgu, w_s, b_s, (3 * aw) // sw, (3 * aw + sw) // sw)
    gate_col = (3 * aw + 2 * sw) // tn
    merged = ws_matmul([attn, sgu], [w_pa, w_pb], [0, 1], [z, z],
                       [gate_col, gate_col + d // tn], _ep_merge, BF16, tn, w_base=layer,
                       tm=DENSE_ROW_TILE, name="branch_merge")
    return ws_matmul([merged], [w_o], [0], [x], [0], _ep_residual, F32, tn, w_base=layer,
                     next_gain=next_gain, tm=DENSE_ROW_TILE, name="out_proj")


def dense_ffn(x, pre, j, wg, wu, wd, next_gain):
    xg, sumsq = pre
    act = ws_matmul([xg], [wg, wu], [0, 0], [], [], _ep_swiglu, BF16, 512, w_base=j,
                    round_once=True, row_sumsq=sumsq, tm=DENSE_ROW_TILE, name="ffn_up")
    return ws_matmul([act], [wd], [0], [x], [0], _ep_residual, F32, 512, w_base=j,
                     round_once=True, next_gain=next_gain, tm=DENSE_ROW_TILE, name="ffn_down")


def moe_ffn(x, norm_g, j, router_w, router_b, wg, wu, wd, final_g, final_norm):
    n, d = x.shape
    ne = router_w.shape[1]
    tm = ROW_TILE
    eidx, wts, rank, cnt = moe_router(x, norm_g, router_w, router_b)
    counts = cnt[:, 0].astype(jnp.int32)
    padded = ((counts + tm - 1) // tm) * tm
    ends = jnp.cumsum(padded)
    starts = ends - padded
    group_start = jnp.sum(jnp.where(eidx[..., None] == jnp.arange(ne), starts, 0), axis=-1)
    slot = group_start + rank
    n_tiles = (TOP_K * n) // tm + ne
    tok = jnp.tile(jnp.arange(n, dtype=jnp.int32), TOP_K)
    src = (jnp.arange(n_tiles * tm, dtype=jnp.int32) % n).at[slot.reshape(-1)].set(tok)
    tile_row = jnp.arange(n_tiles, dtype=jnp.int32) * tm
    last_used = jnp.max(jnp.where(counts > 0, jnp.arange(ne, dtype=jnp.int32), 0))
    texp = jnp.minimum(jnp.sum(tile_row[:, None] >= ends[None, :], axis=1), last_used).astype(jnp.int32)
    group_end = jnp.sum(jnp.where(texp[:, None] == jnp.arange(ne), starts + counts, 0), axis=-1)
    tvalid = jnp.clip(group_end - tile_row, 0, tm).astype(jnp.int32)

    hs = moe_dispatch(x, norm_g, src)
    act = ws_matmul([hs], [wg, wu], [0, 0], [], [], _ep_swiglu, BF16, 512, texp=texp,
                    tvalid=tvalid, sub_rows=MOE_SUB_ROWS, w_base=j * ne, name="moe_up")
    y = ws_matmul([act], [wd], [0], [], [], _ep_plain, F32, 1024, texp=texp, tvalid=tvalid,
                  sub_rows=MOE_SUB_ROWS, w_base=j * ne, round_once=True, name="moe_down")
    return moe_combine(y, x, slot, wts, final_g, final_norm)


def kernel(x, mix_norm_g, w_in, sgu_norm_g, w_s, b_s, w_pa, w_pb, w_o, ffn_norm_g,
           dense_w_gate, dense_w_up, dense_w_down, router_w, router_b,
           expert_w_gate, expert_w_up, expert_w_down, final_norm_g):
    batch, seq, d = x.shape
    depth = mix_norm_g.shape[0]
    xf = x.reshape(batch * seq, d)
    merge_experts = lambda w: w.reshape((w.shape[0] * w.shape[1],) + w.shape[2:])
    ewg, ewu, ewd = (merge_experts(w) for w in (expert_w_gate, expert_w_up, expert_w_down))
    pre = prenorm(xf, mix_norm_g[0])
    normed = False
    for i in range(depth):
        j = i // 2
        last = i == depth - 1
        dense = i % 2 == 0
        mixed = token_mixer(xf, pre, batch, seq, i, w_in, sgu_norm_g[i], w_s[i], b_s[i],
                            w_pa, w_pb, w_o, ffn_norm_g[i] if dense else None)
        if dense:
            xf, *pre = mixed
            out = dense_ffn(xf, pre, j, dense_w_gate, dense_w_up, dense_w_down,
                            None if last else mix_norm_g[i + 1])
            if last:
                xf = out
            else:
                xf, *pre = out
        else:
            xf = moe_ffn(mixed, ffn_norm_g[i], j, router_w[j], router_b[j], ewg, ewu, ewd,
                         final_norm_g, last)
            normed = last
            if not last:
                pre = prenorm(xf, mix_norm_g[i + 1])
    if not normed:
        xf = rmsnorm(xf, final_norm_g, F32)
    return xf.reshape(batch, seq, d)
```

```python
import functools

import jax
import jax.numpy as jnp
from jax import lax
from jax.experimental import pallas as pl
from jax.experimental.pallas import tpu as pltpu

F32 = jnp.float32
BF16 = jnp.bfloat16

ATTN_HEADS = 8
HEAD_DIM = 128
MOBA_BLOCK = 256
MOBA_TOPK = 3
SGU_GROUPS = 8
SGU_GROUP_DIM = 128
SGU_CHUNK = 128
TOP_K = 2
NORM_EPS = 1e-6
STAT_LANES = 128
LOG2_E = 1.4426950408889634

V7X_VMEM_BYTES = 64 * 1024 * 1024
VMEM_LIMIT_BYTES = V7X_VMEM_BYTES - 8 * 1024 * 1024

ROW_TILE = 512
DENSE_ROW_TILE = 1024
WIDE_ROW_TILE = 2048
CAST_ROWS = 256
MOE_SUB_ROWS = 128
DISPATCH_ROWS = 256
SOFTMAX_LAG = 2
VALUES_LAG = 1


def _cparams(sem):
    return pltpu.CompilerParams(dimension_semantics=sem, vmem_limit_bytes=VMEM_LIMIT_BYTES)


def _rms(x, g):
    ms = jnp.mean(x * x, axis=-1, keepdims=True)
    return x * lax.rsqrt(ms + NORM_EPS) * g


def _rmsnorm_kernel(x_ref, g_ref, o_ref):
    o_ref[...] = _rms(x_ref[...], g_ref[...]).astype(o_ref.dtype)


def rmsnorm(x, g, out_dtype, tm=ROW_TILE):
    n, d = x.shape
    return pl.pallas_call(
        _rmsnorm_kernel,
        out_shape=jax.ShapeDtypeStruct((n, d), out_dtype),
        grid=(n // tm,),
        in_specs=[pl.BlockSpec((tm, d), lambda i: (i, 0)),
                  pl.BlockSpec((1, d), lambda i: (0, 0))],
        out_specs=pl.BlockSpec((tm, d), lambda i: (i, 0)),
        compiler_params=_cparams(("parallel",)),
        name="rmsnorm",
    )(x, g.reshape(1, d))


def _prenorm_kernel(x_ref, g_ref, o_ref, s_ref):
    x = x_ref[...]
    o_ref[...] = (x * g_ref[...]).astype(o_ref.dtype)
    s_ref[...] = jnp.broadcast_to(jnp.sum(x * x, axis=1, keepdims=True), s_ref.shape)


def prenorm(x, g, tm=ROW_TILE):
    n, d = x.shape
    return pl.pallas_call(
        _prenorm_kernel,
        out_shape=(jax.ShapeDtypeStruct((n, d), BF16), jax.ShapeDtypeStruct((n, STAT_LANES), F32)),
        grid=(n // tm,),
        in_specs=[pl.BlockSpec((tm, d), lambda i: (i, 0)),
                  pl.BlockSpec((1, d), lambda i: (0, 0))],
        out_specs=(pl.BlockSpec((tm, d), lambda i: (i, 0)),
                   pl.BlockSpec((tm, STAT_LANES), lambda i: (i, 0))),
        compiler_params=_cparams(("parallel",)),
        name="prenorm",
    )(x, g.reshape(1, d))


def _ws_kernel(texp_ref, tnext_ref, tvalid_ref, lhs_rows_ref, *refs, n_lhs, w_lhs, n_extra, epilogue,
               sub_rows, tn, scaled, round_once, normed_lhs, emit_prenorm):
    n_w = len(w_lhs)
    lhs_refs = refs[:n_lhs]
    w_hbm = refs[n_lhs:n_lhs + n_w]
    extra_refs = refs[n_lhs + n_w:n_lhs + n_w + n_extra]
    pos = n_lhs + n_w + n_extra
    scale_ref = refs[pos] if scaled else None
    pos += int(scaled)
    stats_ref = refs[pos] if normed_lhs else None
    pos += int(normed_lhs)
    gain_ref = refs[pos] if emit_prenorm else None
    pos += int(emit_prenorm)
    out_ref = refs[pos]
    prenorm_ref, sumsq_ref = refs[pos + 1:pos + 3] if emit_prenorm else (None, None)
    scratch = refs[pos + 1 + 2 * int(emit_prenorm):]
    wf_refs = scratch[:n_w]
    tm = out_ref.shape[0]
    if round_once:
        wb_refs = scratch[n_w:2 * n_w]
        sem = scratch[2 * n_w]
    else:
        sem, groups_seen = scratch[n_w:]

    n = pl.program_id(0)
    r = pl.program_id(1)
    n_col_tiles = pl.num_programs(0)

    def weight_copy(i, index, col_tile, slot):
        cols = pl.ds(pl.multiple_of(col_tile * tn, tn), tn)
        dst, dst_sem = (wf_refs[i], sem.at[i]) if round_once else (wf_refs[i].at[slot], sem.at[i, slot])
        return pltpu.make_async_copy(w_hbm[i].at[index, :, cols], dst, dst_sem)

    def start_weights(index, col_tile, slot):
        for i in range(n_w):
            weight_copy(i, index, col_tile, slot).start()

    def start_following(slot):
        following = tnext_ref[r]

        @pl.when(following >= 0)
        def _():
            start_weights(following, n, slot)

        @pl.when(jnp.logical_and(following < 0, n + 1 < n_col_tiles))
        def _():
            start_weights(texp_ref[0], n + 1, slot)

    def rounded(w):
        if scaled:
            w = w * scale_ref[...]
        return w.astype(BF16)

    @pl.when(jnp.logical_and(n == 0, r == 0))
    def _():
        if not round_once:
            groups_seen[0] = 0
        start_weights(texp_ref[0], 0, 0)

    new_group = jnp.logical_or(r == 0, texp_ref[r] != texp_ref[jnp.maximum(r - 1, 0)])

    @pl.when(new_group)
    def _():
        if round_once:
            for i in range(n_w):
                weight_copy(i, texp_ref[r], n, 0).wait()
            for wf_ref, wb_ref in zip(wf_refs, wb_refs):
                def cast_rows(c, carry, wf_ref=wf_ref, wb_ref=wb_ref):
                    rows = pl.ds(pl.multiple_of(c * CAST_ROWS, CAST_ROWS), CAST_ROWS)
                    wb_ref[rows, :] = rounded(wf_ref[rows, :])
                    return carry

                lax.fori_loop(0, wb_ref.shape[0] // CAST_ROWS, cast_rows, 0)
            start_following(0)
        else:
            seen = groups_seen[0]
            slot = jnp.bitwise_and(seen, 1)
            groups_seen[0] = seen + 1
            for i in range(n_w):
                weight_copy(i, texp_ref[r], n, slot).wait()
            start_following(1 - slot)

    def operand(i):
        if round_once:
            return wb_refs[i][...]
        return rounded(wf_refs[i][jnp.bitwise_and(groups_seen[0] - 1, 1)])

    def compute(n_live):
        rows = slice(0, n_live)
        if n_live > 0:
            prods = [jnp.dot(lhs_refs[w_lhs[i]][rows, :], operand(i),
                             preferred_element_type=F32) for i in range(n_w)]
            if normed_lhs:
                width = lhs_refs[0].shape[1]
                sumsq = sum(stats_ref[rows, p:p + 1] for p in range(0, stats_ref.shape[1], STAT_LANES))
                inv_rms = lax.rsqrt(sumsq / width + NORM_EPS)
                prods = [p * inv_rms for p in prods]
            res = epilogue(prods, [e[rows, :] for e in extra_refs])
            out_ref[rows, :] = res.astype(out_ref.dtype)
            if emit_prenorm:
                prenorm_ref[rows, :] = (res * gain_ref[...]).astype(prenorm_ref.dtype)
                sumsq_ref[rows, :] = jnp.broadcast_to(jnp.sum(res * res, axis=1, keepdims=True),
                                                      (n_live, STAT_LANES))
        if n_live < tm:
            for ref in (out_ref, prenorm_ref, sumsq_ref):
                if ref is not None:
                    ref[n_live:tm, :] = jnp.zeros((tm - n_live, ref.shape[1]), ref.dtype)

    if sub_rows is None:
        compute(tm)
    else:
        live_chunks = (tvalid_ref[r] + (sub_rows - 1)) // sub_rows
        for c in range(tm // sub_rows + 1):
            pl.when(live_chunks == c)(functools.partial(compute, c * sub_rows))


def ws_matmul(lhs, ws, w_lhs, extras, extra_col_off, epilogue, out_dtype, tn, texp=None,
              tvalid=None, sub_rows=None, w_base=0, col_scale=None, round_once=False,
              row_sumsq=None, next_gain=None, lhs_rows=None, tm=ROW_TILE, name="ws_matmul"):
    m = lhs[0].shape[0]
    n_total = ws[0].shape[2]
    n_rows = m // tm
    if texp is None:
        texp = jnp.zeros((n_rows,), jnp.int32)
    if tvalid is None:
        tvalid = jnp.full((n_rows,), tm, jnp.int32)
    no_next = jnp.iinfo(jnp.int32).max
    later = jnp.min(jnp.where(texp[None, :] > texp[:, None], texp[None, :], no_next), axis=1)
    tnext = jnp.where(later == no_next, -1, later + w_base).astype(jnp.int32)
    texp = texp + w_base
    if lhs_rows is None:
        lhs_rows = jnp.arange(n_rows, dtype=jnp.int32)
    in_specs = []
    for a in lhs:
        in_specs.append(pl.BlockSpec((tm, a.shape[1]), lambda n, r, t, tn_, tv, lr: (lr[r], 0)))
    for w in ws:
        in_specs.append(pl.BlockSpec(memory_space=pl.ANY))
    for off in extra_col_off:
        in_specs.append(pl.BlockSpec((tm, tn), lambda n, r, *_, off=off: (r, off + n)))
    scale_args = []
    if col_scale is not None:
        in_specs.append(pl.BlockSpec((1, tn), lambda n, r, *_: (0, n)))
        scale_args.append(col_scale.reshape(1, n_total))
    if row_sumsq is not None:
        in_specs.append(pl.BlockSpec((tm, row_sumsq.shape[1]), lambda n, r, *_: (r, 0)))
        scale_args.append(row_sumsq)
    out_shapes = [jax.ShapeDtypeStruct((m, n_total), out_dtype)]
    out_specs = [pl.BlockSpec((tm, tn), lambda n, r, *_: (r, n))]
    if next_gain is not None:
        in_specs.append(pl.BlockSpec((1, tn), lambda n, r, *_: (0, n)))
        scale_args.append(next_gain.reshape(1, n_total))
        out_shapes += [jax.ShapeDtypeStruct((m, n_total), BF16),
                       jax.ShapeDtypeStruct((m, (n_total // tn) * STAT_LANES), F32)]
        out_specs += [pl.BlockSpec((tm, tn), lambda n, r, *_: (r, n)),
                      pl.BlockSpec((tm, STAT_LANES), lambda n, r, *_: (r, n))]
    kern = functools.partial(_ws_kernel, n_lhs=len(lhs), w_lhs=tuple(w_lhs),
                             n_extra=len(extras), epilogue=epilogue, sub_rows=sub_rows, tn=tn,
                             scaled=col_scale is not None, round_once=round_once,
                             normed_lhs=row_sumsq is not None, emit_prenorm=next_gain is not None)
    if round_once:
        scratch = ([pltpu.VMEM((w.shape[1], tn), F32) for w in ws]
                   + [pltpu.VMEM((w.shape[1], tn), BF16) for w in ws]
                   + [pltpu.SemaphoreType.DMA((len(ws),))])
    else:
        scratch = ([pltpu.VMEM((2, w.shape[1], tn), F32) for w in ws]
                   + [pltpu.SemaphoreType.DMA((len(ws), 2)), pltpu.SMEM((1,), jnp.int32)])
    outs = pl.pallas_call(
        kern,
        out_shape=out_shapes,
        grid_spec=pltpu.PrefetchScalarGridSpec(
            num_scalar_prefetch=4,
            grid=(n_total // tn, n_rows),
            in_specs=in_specs,
            out_specs=out_specs,
            scratch_shapes=scratch,
        ),
        compiler_params=_cparams(("arbitrary", "arbitrary")),
        name=name,
    )(texp, tnext, tvalid, lhs_rows, *lhs, *ws, *extras, *scale_args)
    return outs if next_gain is not None else outs[0]


def _ep_plain(prods, extras):
    return prods[0]


def _ep_residual(prods, extras):
    return extras[0] + prods[0]


def _ep_swiglu(prods, extras):
    g, u = prods
    return (g * jax.nn.sigmoid(g)) * u


def _ep_merge(prods, extras):
    ga, gb = extras
    return (jax.nn.sigmoid(ga.astype(F32)) * prods[0]
            + jax.nn.sigmoid(gb.astype(F32)) * prods[1])


def _attn_kernel(q_ref, k_ref, v_ref, o_ref, *, nb, blk, topk):
    seq = k_ref.shape[0]
    neg_inf = jnp.float32(-jnp.inf)
    avg = jnp.where(lax.broadcasted_iota(jnp.int32, (nb, seq), 1) // blk
                    == lax.broadcasted_iota(jnp.int32, (nb, seq), 0), 1.0 / blk, 0.0).astype(BF16)
    k_mean = jnp.dot(avg, k_ref[...], preferred_element_type=F32)
    k_mean_hi = k_mean.astype(BF16).astype(F32)
    k_mean_parts = jnp.concatenate([k_mean_hi, k_mean - k_mean_hi], axis=0).astype(BF16)
    blk_id = lax.broadcasted_iota(jnp.int32, (nb, blk), 0)
    causal = (lax.broadcasted_iota(jnp.int32, (blk, blk), 0)
              <= lax.broadcasted_iota(jnp.int32, (blk, blk), 1))
    nt_dims = (((1,), (1,)), ((), ()))
    v_t = v_ref[...].T

    def scores(n):
        qn = q_ref[n * blk:(n + 1) * blk, :]
        s = lax.dot_general(k_ref[0:(n + 1) * blk, :], qn, nt_dims, preferred_element_type=F32)
        gate_parts = None
        if n > topk:
            gate_parts = lax.dot_general(k_mean_parts, qn, nt_dims, preferred_element_type=F32)
        return s, gate_parts

    def masked_softmax(n, s, gate_parts):
        past = [s[j * blk:(j + 1) * blk, :] for j in range(n)]
        if n > topk:
            gate = gate_parts[:nb] + gate_parts[nb:]
            beaten = jnp.zeros((nb, blk), F32)
            for j in range(n):
                gj = gate[j:j + 1, :]
                wins = jnp.logical_or(gj > gate, jnp.logical_and(gj == gate, j < blk_id))
                beaten = beaten + wins.astype(F32)
            bias = jnp.where(beaten < topk, 0.0, neg_inf)
            past = [past[j] + bias[j:j + 1, :] for j in range(n)]
        own = jnp.where(causal, s[n * blk:(n + 1) * blk, :], neg_inf)
        s = jnp.concatenate(past + [own], axis=0)
        m = jnp.max(s, axis=0, keepdims=True)
        p = jnp.exp2(s - m)
        return p.astype(BF16), jnp.sum(p, axis=0, keepdims=True)

    def weighted_values(n, p, l):
        o_t = jnp.dot(v_t[:, 0:(n + 1) * blk], p, preferred_element_type=F32)
        o_ref[n * blk:(n + 1) * blk, :] = (o_t / l).T.astype(o_ref.dtype)

    order = list(range(nb - 1, -1, -1))
    s_out, p_out = {}, {}
    for t in range(nb + SOFTMAX_LAG + VALUES_LAG):
        if t < nb:
            s_out[order[t]] = scores(order[t])
        if 0 <= t - SOFTMAX_LAG < nb:
            n = order[t - SOFTMAX_LAG]
            p_out[n] = masked_softmax(n, *s_out.pop(n))
        if 0 <= t - SOFTMAX_LAG - VALUES_LAG < nb:
            n = order[t - SOFTMAX_LAG - VALUES_LAG]
            weighted_values(n, *p_out.pop(n))


def moba_attention(z, batch, seq):
    nb = seq // MOBA_BLOCK
    kern = functools.partial(_attn_kernel, nb=nb, blk=MOBA_BLOCK, topk=MOBA_TOPK)
    blockspec = lambda off: pl.BlockSpec((None, seq, HEAD_DIM), lambda b, h: (b, 0, off + h))
    return pl.pallas_call(
        kern,
        out_shape=jax.ShapeDtypeStruct((batch, seq, ATTN_HEADS * HEAD_DIM), BF16),
        grid=(batch, ATTN_HEADS),
        in_specs=[blockspec(0), blockspec(ATTN_HEADS), blockspec(2 * ATTN_HEADS)],
        out_specs=pl.BlockSpec((None, seq, HEAD_DIM), lambda b, h: (b, 0, h)),
        compiler_params=_cparams(("parallel", "parallel")),
        name="moba_attention",
    )(z, z, z)


def _gelu_tanh(x):
    return 0.5 * x * (1.0 + jnp.tanh(0.7978845608028654 * (x + 0.044715 * (x * x * x))))


def _sgu_kernel(u_ref, v_ref, g_ref, w_ref, bt_ref, o_ref, *, chunk, groups, gd):
    tm = u_ref.shape[0]
    u = _gelu_tanh(u_ref[...].astype(F32))
    v = _gelu_tanh(v_ref[...].astype(F32))
    mu = jnp.mean(v, axis=-1, keepdims=True)
    vc = v - mu
    var = jnp.mean(vc * vc, axis=-1, keepdims=True)
    vn = (vc * lax.rsqrt(var + NORM_EPS) * g_ref[...]).astype(BF16)
    t_i = lax.broadcasted_iota(jnp.int32, (chunk, chunk), 0)
    s_i = lax.broadcasted_iota(jnp.int32, (chunk, chunk), 1)
    lower = s_i <= t_i
    for g in range(groups):
        w = jnp.where(lower, w_ref[g], 0.0).astype(BF16)
        bias = bt_ref[:, g:g + 1]
        cols = slice(g * gd, (g + 1) * gd)
        for c in range(tm // chunk):
            rows = slice(c * chunk, (c + 1) * chunk)
            mixed = jnp.dot(w, vn[rows, cols], preferred_element_type=F32) + bias
            o_ref[rows, cols] = (u[rows, cols] * mixed).astype(o_ref.dtype)


def spatial_gating(z, g_sgu, w_s, b_s, col_u, col_v, tm=256):
    n = z.shape[0]
    width = SGU_GROUPS * SGU_GROUP_DIM
    kern = functools.partial(_sgu_kernel, chunk=SGU_CHUNK, groups=SGU_GROUPS, gd=SGU_GROUP_DIM)
    return pl.pallas_call(
        kern,
        out_shape=jax.ShapeDtypeStruct((n, width), BF16),
        grid=(n // tm,),
        in_specs=[pl.BlockSpec((tm, width), lambda i: (i, col_u)),
                  pl.BlockSpec((tm, width), lambda i: (i, col_v)),
                  pl.BlockSpec((1, width), lambda i: (0, 0)),
                  pl.BlockSpec(w_s.shape, lambda i: (0, 0, 0)),
                  pl.BlockSpec((SGU_CHUNK, SGU_GROUPS), lambda i: (0, 0))],
        out_specs=pl.BlockSpec((tm, width), lambda i: (i, 0)),
        compiler_params=_cparams(("parallel",)),
        name="spatial_gating",
    )(z, z, g_sgu.reshape(1, width), w_s, b_s.T)


def _router_kernel(x_ref, g_ref, rwt_ref, rb_ref, eidx_ref, wts_ref, rank_ref, cnt_ref):
    i = pl.program_id(0)
    ne = rwt_ref.shape[0]
    tm = x_ref.shape[0]

    @pl.when(i == 0)
    def _():
        cnt_ref[...] = jnp.zeros_like(cnt_ref)

    h = _rms(x_ref[...], g_ref[...])
    h_hi = h.astype(BF16)
    h_lo = (h - h_hi.astype(F32)).astype(BF16)
    rw = rwt_ref[...]
    rw_hi = rw.astype(BF16)
    rw_hi_f32 = rw_hi.astype(F32)
    rw_parts = jnp.concatenate([rw_hi_f32, rw - rw_hi_f32], axis=0).astype(BF16)
    nt_dims = (((1,), (1,)), ((), ()))
    both = lax.dot_general(rw_parts, h_hi, nt_dims, preferred_element_type=F32)
    logits = (both[:ne] + both[ne:]
              + lax.dot_general(rw_hi, h_lo, nt_dims, preferred_element_type=F32)
              + rb_ref[...])
    row = lax.broadcasted_iota(jnp.int32, (ne, tm), 0).astype(F32)
    neg_inf = jnp.float32(-jnp.inf)
    m0 = jnp.max(logits, axis=0, keepdims=True)
    i0 = jnp.min(jnp.where(logits == m0, row, float(ne)), axis=0, keepdims=True)
    rest = jnp.where(row == i0, neg_inf, logits)
    m1 = jnp.max(rest, axis=0, keepdims=True)
    i1 = jnp.min(jnp.where(jnp.logical_and(rest == m1, row != i0), row, float(ne)),
                 axis=0, keepdims=True)
    e1 = jnp.exp(m1 - m0)
    denom = 1.0 + e1
    eidx_ref[0:1, :] = i0.astype(jnp.int32)
    eidx_ref[1:2, :] = i1.astype(jnp.int32)
    wts_ref[0:1, :] = 1.0 / denom
    wts_ref[1:2, :] = e1 / denom

    hit0 = row == i0
    hit1 = row == i1
    chosen = jnp.logical_or(hit0, hit1).astype(BF16)
    earlier = (lax.broadcasted_iota(jnp.int32, (tm, tm), 0)
               < lax.broadcasted_iota(jnp.int32, (tm, tm), 1)).astype(BF16)
    before = jnp.dot(chosen, earlier, preferred_element_type=F32) + cnt_ref[:, 0:1]
    rank_ref[0:1, :] = jnp.sum(jnp.where(hit0, before, 0.0), axis=0, keepdims=True).astype(jnp.int32)
    rank_ref[1:2, :] = jnp.sum(jnp.where(hit1, before, 0.0), axis=0, keepdims=True).astype(jnp.int32)
    cnt_ref[...] = cnt_ref[...] + jnp.sum(chosen.astype(F32), axis=1, keepdims=True)


def moe_router(x, g, router_w, router_b, tm=ROW_TILE):
    n, d = x.shape
    ne = router_w.shape[1]
    out_shapes = (jax.ShapeDtypeStruct((TOP_K, n), jnp.int32),
                  jax.ShapeDtypeStruct((TOP_K, n), F32),
                  jax.ShapeDtypeStruct((TOP_K, n), jnp.int32),
                  jax.ShapeDtypeStruct((ne, 128), F32))
    tok_spec = pl.BlockSpec((TOP_K, tm), lambda i: (0, i))
    return pl.pallas_call(
        _router_kernel,
        out_shape=out_shapes,
        grid=(n // tm,),
        in_specs=[pl.BlockSpec((tm, d), lambda i: (i, 0)),
                  pl.BlockSpec((1, d), lambda i: (0, 0)),
                  pl.BlockSpec((ne, d), lambda i: (0, 0)),
                  pl.BlockSpec((ne, 1), lambda i: (0, 0))],
        out_specs=(tok_spec, tok_spec, tok_spec, pl.BlockSpec((ne, 128), lambda i: (0, 0))),
        compiler_params=_cparams(("arbitrary",)),
        name="moe_router",
    )(x, g.reshape(1, d), router_w.T, router_b.reshape(ne, 1))


ISSUE_UNROLL = 8


def _start_row_gather(src_hbm, idx_ref, idx_base, n_rows, buf, slot, row_base, sem):
    def body(j, carry):
        row = idx_ref[idx_base + j]
        pltpu.make_async_copy(src_hbm.at[pl.ds(row, 1), :],
                              buf.at[slot, pl.ds(row_base + j, 1), :], sem.at[slot]).start()
        return carry
    lax.fori_loop(0, n_rows, body, 0, unroll=ISSUE_UNROLL)


def _wait_row_gather(src_hbm, buf, slot, sem):
    n_rows = buf.shape[1]
    pltpu.make_async_copy(src_hbm.at[pl.ds(0, n_rows), :], buf.at[slot], sem.at[slot]).wait()


def _dispatch_kernel(src_ref, used_ref, x_hbm, g_ref, o_ref, buf, sem, *, tg):
    i = pl.program_id(0)
    n_used = used_ref[0]

    @pl.when(i == 0)
    def _():
        _start_row_gather(x_hbm, src_ref, 0, tg, buf, 0, 0, sem)

    @pl.when(i + 1 < n_used)
    def _():
        _start_row_gather(x_hbm, src_ref, (i + 1) * tg, tg, buf, jnp.bitwise_and(i + 1, 1), 0, sem)

    @pl.when(i < n_used)
    def _():
        slot = jnp.bitwise_and(i, 1)
        _wait_row_gather(x_hbm, buf, slot, sem)
        o_ref[...] = _rms(buf[slot], g_ref[...]).astype(o_ref.dtype)

    @pl.when(i >= n_used)
    def _():
        o_ref[...] = jnp.zeros_like(o_ref)


def moe_dispatch(x, g, src, n_used_steps, tg):
    n, d = x.shape
    n_slots = src.shape[0]
    kern = functools.partial(_dispatch_kernel, tg=tg)
    return pl.pallas_call(
        kern,
        out_shape=jax.ShapeDtypeStruct((n_slots, d), BF16),
        grid_spec=pltpu.PrefetchScalarGridSpec(
            num_scalar_prefetch=2,
            grid=(n_slots // tg,),
            in_specs=[pl.BlockSpec(memory_space=pl.ANY),
                      pl.BlockSpec((1, d), lambda i, *_: (0, 0))],
            out_specs=pl.BlockSpec((tg, d), lambda i, *_: (i, 0)),
            scratch_shapes=[pltpu.VMEM((2, tg, d), F32), pltpu.SemaphoreType.DMA((2,))],
        ),
        compiler_params=_cparams(("arbitrary",)),
        name="moe_dispatch",
    )(src, n_used_steps, x, g.reshape(1, d))


def _combine_kernel(slot_ref, y_hbm, x_ref, w_ref, g_ref, o_ref, buf, sem, *, tc, final_norm):
    i = pl.program_id(0)
    nsteps = pl.num_programs(0)
    n_tok = nsteps * tc

    def fetch(step, slot):
        for k in range(TOP_K):
            _start_row_gather(y_hbm, slot_ref, k * n_tok + step * tc, tc, buf, slot, k * tc, sem)

    @pl.when(i == 0)
    def _():
        fetch(0, 0)

    @pl.when(i + 1 < nsteps)
    def _():
        fetch(i + 1, jnp.bitwise_and(i + 1, 1))

    slot = jnp.bitwise_and(i, 1)
    _wait_row_gather(y_hbm, buf, slot, sem)
    acc = x_ref[...]
    for k in range(TOP_K):
        acc = acc + w_ref[:, k:k + 1] * buf[slot, k * tc:(k + 1) * tc, :]
    if final_norm:
        acc = _rms(acc, g_ref[...])
    o_ref[...] = acc


def moe_combine(y, x, slot, wts, g, final_norm, tc=128):
    n, d = x.shape
    kern = functools.partial(_combine_kernel, tc=tc, final_norm=final_norm)
    return pl.pallas_call(
        kern,
        out_shape=jax.ShapeDtypeStruct((n, d), F32),
        grid_spec=pltpu.PrefetchScalarGridSpec(
            num_scalar_prefetch=1,
            grid=(n // tc,),
            in_specs=[pl.BlockSpec(memory_space=pl.ANY),
                      pl.BlockSpec((tc, d), lambda i, s: (i, 0)),
                      pl.BlockSpec((tc, TOP_K), lambda i, s: (i, 0)),
                      pl.BlockSpec((1, d), lambda i, s: (0, 0))],
            out_specs=pl.BlockSpec((tc, d), lambda i, s: (i, 0)),
            scratch_shapes=[pltpu.VMEM((2, TOP_K * tc, d), F32), pltpu.SemaphoreType.DMA((2,))],
        ),
        compiler_params=_cparams(("arbitrary",)),
        name="moe_combine",
    )(slot.reshape(-1), y, x, wts.T, g.reshape(1, d))


def token_mixer(x, pre, batch, seq, layer, w_in, g_sgu, w_s, b_s, w_pa, w_pb, w_o, next_gain):
    n, d = x.shape
    aw = ATTN_HEADS * HEAD_DIM
    sw = SGU_GROUPS * SGU_GROUP_DIM
    tn = 1024
    xg, sumsq = pre
    q_scale = jnp.where(jnp.arange(w_in.shape[2]) < aw, HEAD_DIM ** -0.5 * LOG2_E, 1.0).astype(F32)
    in_tm = WIDE_ROW_TILE if sumsq.shape[1] == STAT_LANES else DENSE_ROW_TILE
    z = ws_matmul([xg], [w_in], [0], [], [], _ep_plain, BF16, tn, w_base=layer,
                  col_scale=q_scale, row_sumsq=sumsq, tm=in_tm, name="in_proj")
    attn = moba_attention(z.reshape(batch, seq, z.shape[1]), batch, seq).reshape(n, aw)
    sgu = spatial_gating(z, g_sgu, w_s, b_s, (3 * aw) // sw, (3 * aw + sw) // sw)
    gate_col = (3 * aw + 2 * sw) // tn
    merged = ws_matmul([attn, sgu], [w_pa, w_pb], [0, 1], [z, z],
                       [gate_col, gate_col + d // tn], _ep_merge, BF16, tn, w_base=layer,
                       tm=DENSE_ROW_TILE, name="branch_merge")
    return ws_matmul([merged], [w_o], [0], [x], [0], _ep_residual, F32, tn, w_base=layer,
                     next_gain=next_gain, tm=DENSE_ROW_TILE, name="out_proj")


def dense_ffn(x, pre, j, wg, wu, wd, next_gain):
    xg, sumsq = pre
    act = ws_matmul([xg], [wg, wu], [0, 0], [], [], _ep_swiglu, BF16, 512, w_base=j,
                    round_once=True, row_sumsq=sumsq, tm=DENSE_ROW_TILE, name="ffn_up")
    return ws_matmul([act], [wd], [0], [x], [0], _ep_residual, F32, 512, w_base=j,
                     round_once=True, next_gain=next_gain, tm=DENSE_ROW_TILE, name="ffn_down")


def moe_ffn(x, norm_g, j, router_w, router_b, wg, wu, wd, final_g, final_norm):
    n, d = x.shape
    ne = router_w.shape[1]
    tm = ROW_TILE
    eidx, wts, rank, cnt = moe_router(x, norm_g, router_w, router_b)
    counts = cnt[:, 0].astype(jnp.int32)
    padded = ((counts + tm - 1) // tm) * tm
    ends = jnp.cumsum(padded)
    starts = ends - padded
    group_start = jnp.sum(jnp.where(eidx[..., None] == jnp.arange(ne), starts, 0), axis=-1)
    slot = group_start + rank
    n_tiles = (TOP_K * n) // tm + ne
    tok = jnp.tile(jnp.arange(n, dtype=jnp.int32), TOP_K)
    src = (jnp.arange(n_tiles * tm, dtype=jnp.int32) % n).at[slot.reshape(-1)].set(tok)
    tile_row = jnp.arange(n_tiles, dtype=jnp.int32) * tm
    last_used = jnp.max(jnp.where(counts > 0, jnp.arange(ne, dtype=jnp.int32), 0))
    texp = jnp.minimum(jnp.sum(tile_row[:, None] >= ends[None, :], axis=1), last_used).astype(jnp.int32)
    group_end = jnp.sum(jnp.where(texp[:, None] == jnp.arange(ne), starts + counts, 0), axis=-1)
    tvalid = jnp.clip(group_end - tile_row, 0, tm).astype(jnp.int32)

    n_used_tiles = ends[-1] // tm
    lhs_rows = jnp.minimum(jnp.arange(n_tiles, dtype=jnp.int32), n_used_tiles - 1)
    n_used_steps = (n_used_tiles * (tm // DISPATCH_ROWS)).reshape(1).astype(jnp.int32)

    hs = moe_dispatch(x, norm_g, src, n_used_steps, DISPATCH_ROWS)
    act = ws_matmul([hs], [wg, wu], [0, 0], [], [], _ep_swiglu, BF16, 512, texp=texp,
                    tvalid=tvalid, sub_rows=MOE_SUB_ROWS, w_base=j * ne, lhs_rows=lhs_rows,
                    name="moe_up")
    y = ws_matmul([act], [wd], [0], [], [], _ep_plain, F32, 1024, texp=texp, tvalid=tvalid,
                  sub_rows=MOE_SUB_ROWS, w_base=j * ne, round_once=True, lhs_rows=lhs_rows,
                  name="moe_down")
    return moe_combine(y, x, slot, wts, final_g, final_norm)


def kernel(x, mix_norm_g, w_in, sgu_norm_g, w_s, b_s, w_pa, w_pb, w_o, ffn_norm_g,
           dense_w_gate, dense_w_up, dense_w_down, router_w, router_b,
           expert_w_gate, expert_w_up, expert_w_down, final_norm_g):
    batch, seq, d = x.shape
    depth = mix_norm_g.shape[0]
    xf = x.reshape(batch * seq, d)
    merge_experts = lambda w: w.reshape((w.shape[0] * w.shape[1],) + w.shape[2:])
    ewg, ewu, ewd = (merge_experts(w) for w in (expert_w_gate, expert_w_up, expert_w_down))
    pre = prenorm(xf, mix_norm_g[0])
    normed = False
    for i in range(depth):
        j = i // 2
        last = i == depth - 1
        dense = i % 2 == 0
        mixed = token_mixer(xf, pre, batch, seq, i, w_in, sgu_norm_g[i], w_s[i], b_s[i],
                            w_pa, w_pb, w_o, ffn_norm_g[i] if dense else None)
        if dense:
            xf, *pre = mixed
            out = dense_ffn(xf, pre, j, dense_w_gate, dense_w_up, dense_w_down,
                            None if last else mix_norm_g[i + 1])
            if last:
                xf = out
            else:
                xf, *pre = out
        else:
            xf = moe_ffn(mixed, ffn_norm_g[i], j, router_w[j], router_b[j], ewg, ewu, ewd,
                         final_norm_g, last)
            normed = last
            if not last:
                pre = prenorm(xf, mix_norm_g[i + 1])
    if not normed:
        xf = rmsnorm(xf, final_norm_g, F32)
    return xf.reshape(batch, seq, d)
```

```python
import functools

import jax
import jax.numpy as jnp
from jax import lax
from jax.experimental import pallas as pl
from jax.experimental.pallas import tpu as pltpu

F32 = jnp.float32
BF16 = jnp.bfloat16

ATTN_HEADS = 8
HEAD_DIM = 128
MOBA_BLOCK = 256
MOBA_TOPK = 3
SGU_GROUPS = 8
SGU_GROUP_DIM = 128
SGU_CHUNK = 128
TOP_K = 2
NORM_EPS = 1e-6
STAT_LANES = 128
LOG2_E = 1.4426950408889634

V7X_VMEM_BYTES = 64 * 1024 * 1024
VMEM_LIMIT_BYTES = V7X_VMEM_BYTES - 8 * 1024 * 1024

ROW_TILE = 512
DENSE_ROW_TILE = 1024
WIDE_ROW_TILE = 2048
CAST_ROWS = 256
MOE_SUB_ROWS = 128
DISPATCH_ROWS = 256
SOFTMAX_LAG = 2
VALUES_LAG = 1
ATTN_HEADS_PER_STEP = 4


def _cparams(sem):
    return pltpu.CompilerParams(dimension_semantics=sem, vmem_limit_bytes=VMEM_LIMIT_BYTES)


def _rms(x, g):
    ms = jnp.mean(x * x, axis=-1, keepdims=True)
    return x * lax.rsqrt(ms + NORM_EPS) * g


def _rmsnorm_kernel(x_ref, g_ref, o_ref):
    o_ref[...] = _rms(x_ref[...], g_ref[...]).astype(o_ref.dtype)


def rmsnorm(x, g, out_dtype, tm=ROW_TILE):
    n, d = x.shape
    return pl.pallas_call(
        _rmsnorm_kernel,
        out_shape=jax.ShapeDtypeStruct((n, d), out_dtype),
        grid=(n // tm,),
        in_specs=[pl.BlockSpec((tm, d), lambda i: (i, 0)),
                  pl.BlockSpec((1, d), lambda i: (0, 0))],
        out_specs=pl.BlockSpec((tm, d), lambda i: (i, 0)),
        compiler_params=_cparams(("parallel",)),
        name="rmsnorm",
    )(x, g.reshape(1, d))


def _prenorm_kernel(x_ref, g_ref, o_ref, s_ref):
    x = x_ref[...]
    o_ref[...] = (x * g_ref[...]).astype(o_ref.dtype)
    s_ref[...] = jnp.broadcast_to(jnp.sum(x * x, axis=1, keepdims=True), s_ref.shape)


def prenorm(x, g, tm=ROW_TILE):
    n, d = x.shape
    return pl.pallas_call(
        _prenorm_kernel,
        out_shape=(jax.ShapeDtypeStruct((n, d), BF16), jax.ShapeDtypeStruct((n, STAT_LANES), F32)),
        grid=(n // tm,),
        in_specs=[pl.BlockSpec((tm, d), lambda i: (i, 0)),
                  pl.BlockSpec((1, d), lambda i: (0, 0))],
        out_specs=(pl.BlockSpec((tm, d), lambda i: (i, 0)),
                   pl.BlockSpec((tm, STAT_LANES), lambda i: (i, 0))),
        compiler_params=_cparams(("parallel",)),
        name="prenorm",
    )(x, g.reshape(1, d))


def _ws_kernel(texp_ref, tnext_ref, tvalid_ref, lhs_rows_ref, *refs, n_lhs, w_lhs, n_extra, epilogue,
               sub_rows, tn, scaled, round_once, normed_lhs, emit_prenorm):
    n_w = len(w_lhs)
    lhs_refs = refs[:n_lhs]
    w_hbm = refs[n_lhs:n_lhs + n_w]
    extra_refs = refs[n_lhs + n_w:n_lhs + n_w + n_extra]
    pos = n_lhs + n_w + n_extra
    scale_ref = refs[pos] if scaled else None
    pos += int(scaled)
    stats_ref = refs[pos] if normed_lhs else None
    pos += int(normed_lhs)
    gain_ref = refs[pos] if emit_prenorm else None
    pos += int(emit_prenorm)
    out_ref = refs[pos]
    prenorm_ref, sumsq_ref = refs[pos + 1:pos + 3] if emit_prenorm else (None, None)
    scratch = refs[pos + 1 + 2 * int(emit_prenorm):]
    wf_refs = scratch[:n_w]
    tm = out_ref.shape[0]
    if round_once:
        wb_refs = scratch[n_w:2 * n_w]
        sem = scratch[2 * n_w]
    else:
        sem, groups_seen = scratch[n_w:]

    n = pl.program_id(0)
    r = pl.program_id(1)
    n_col_tiles = pl.num_programs(0)

    def weight_copy(i, index, col_tile, slot):
        cols = pl.ds(pl.multiple_of(col_tile * tn, tn), tn)
        dst, dst_sem = (wf_refs[i], sem.at[i]) if round_once else (wf_refs[i].at[slot], sem.at[i, slot])
        return pltpu.make_async_copy(w_hbm[i].at[index, :, cols], dst, dst_sem)

    def start_weights(index, col_tile, slot):
        for i in range(n_w):
            weight_copy(i, index, col_tile, slot).start()

    def start_following(slot):
        following = tnext_ref[r]

        @pl.when(following >= 0)
        def _():
            start_weights(following, n, slot)

        @pl.when(jnp.logical_and(following < 0, n + 1 < n_col_tiles))
        def _():
            start_weights(texp_ref[0], n + 1, slot)

    def rounded(w):
        if scaled:
            w = w * scale_ref[...]
        return w.astype(BF16)

    @pl.when(jnp.logical_and(n == 0, r == 0))
    def _():
        if not round_once:
            groups_seen[0] = 0
        start_weights(texp_ref[0], 0, 0)

    new_group = jnp.logical_or(r == 0, texp_ref[r] != texp_ref[jnp.maximum(r - 1, 0)])

    @pl.when(new_group)
    def _():
        if round_once:
            for i in range(n_w):
                weight_copy(i, texp_ref[r], n, 0).wait()
            for wf_ref, wb_ref in zip(wf_refs, wb_refs):
                def cast_rows(c, carry, wf_ref=wf_ref, wb_ref=wb_ref):
                    rows = pl.ds(pl.multiple_of(c * CAST_ROWS, CAST_ROWS), CAST_ROWS)
                    wb_ref[rows, :] = rounded(wf_ref[rows, :])
                    return carry

                lax.fori_loop(0, wb_ref.shape[0] // CAST_ROWS, cast_rows, 0)
            start_following(0)
        else:
            seen = groups_seen[0]
            slot = jnp.bitwise_and(seen, 1)
            groups_seen[0] = seen + 1
            for i in range(n_w):
                weight_copy(i, texp_ref[r], n, slot).wait()
            start_following(1 - slot)

    def operand(i):
        if round_once:
            return wb_refs[i][...]
        return rounded(wf_refs[i][jnp.bitwise_and(groups_seen[0] - 1, 1)])

    def compute(n_live):
        rows = slice(0, n_live)
        if n_live > 0:
            prods = [jnp.dot(lhs_refs[w_lhs[i]][rows, :], operand(i),
                             preferred_element_type=F32) for i in range(n_w)]
            if normed_lhs:
                width = lhs_refs[0].shape[1]
                sumsq = sum(stats_ref[rows, p:p + 1] for p in range(0, stats_ref.shape[1], STAT_LANES))
                inv_rms = lax.rsqrt(sumsq / width + NORM_EPS)
                prods = [p * inv_rms for p in prods]
            res = epilogue(prods, [e[rows, :] for e in extra_refs])
            out_ref[rows, :] = res.astype(out_ref.dtype)
            if emit_prenorm:
                prenorm_ref[rows, :] = (res * gain_ref[...]).astype(prenorm_ref.dtype)
                sumsq_ref[rows, :] = jnp.broadcast_to(jnp.sum(res * res, axis=1, keepdims=True),
                                                      (n_live, STAT_LANES))
        if n_live < tm:
            for ref in (out_ref, prenorm_ref, sumsq_ref):
                if ref is not None:
                    ref[n_live:tm, :] = jnp.zeros((tm - n_live, ref.shape[1]), ref.dtype)

    if sub_rows is None:
        compute(tm)
    else:
        live_chunks = (tvalid_ref[r] + (sub_rows - 1)) // sub_rows
        for c in range(tm // sub_rows + 1):
            pl.when(live_chunks == c)(functools.partial(compute, c * sub_rows))


def ws_matmul(lhs, ws, w_lhs, extras, extra_col_off, epilogue, out_dtype, tn, texp=None,
              tvalid=None, sub_rows=None, w_base=0, col_scale=None, round_once=False,
              row_sumsq=None, next_gain=None, lhs_rows=None, tm=ROW_TILE, name="ws_matmul"):
    m = lhs[0].shape[0]
    n_total = ws[0].shape[2]
    n_rows = m // tm
    if texp is None:
        texp = jnp.zeros((n_rows,), jnp.int32)
    if tvalid is None:
        tvalid = jnp.full((n_rows,), tm, jnp.int32)
    no_next = jnp.iinfo(jnp.int32).max
    later = jnp.min(jnp.where(texp[None, :] > texp[:, None], texp[None, :], no_next), axis=1)
    tnext = jnp.where(later == no_next, -1, later + w_base).astype(jnp.int32)
    texp = texp + w_base
    if lhs_rows is None:
        lhs_rows = jnp.arange(n_rows, dtype=jnp.int32)
    in_specs = []
    for a in lhs:
        in_specs.append(pl.BlockSpec((tm, a.shape[1]), lambda n, r, t, tn_, tv, lr: (lr[r], 0)))
    for w in ws:
        in_specs.append(pl.BlockSpec(memory_space=pl.ANY))
    for off in extra_col_off:
        in_specs.append(pl.BlockSpec((tm, tn), lambda n, r, *_, off=off: (r, off + n)))
    scale_args = []
    if col_scale is not None:
        in_specs.append(pl.BlockSpec((1, tn), lambda n, r, *_: (0, n)))
        scale_args.append(col_scale.reshape(1, n_total))
    if row_sumsq is not None:
        in_specs.append(pl.BlockSpec((tm, row_sumsq.shape[1]), lambda n, r, *_: (r, 0)))
        scale_args.append(row_sumsq)
    out_shapes = [jax.ShapeDtypeStruct((m, n_total), out_dtype)]
    out_specs = [pl.BlockSpec((tm, tn), lambda n, r, *_: (r, n))]
    if next_gain is not None:
        in_specs.append(pl.BlockSpec((1, tn), lambda n, r, *_: (0, n)))
        scale_args.append(next_gain.reshape(1, n_total))
        out_shapes += [jax.ShapeDtypeStruct((m, n_total), BF16),
                       jax.ShapeDtypeStruct((m, (n_total // tn) * STAT_LANES), F32)]
        out_specs += [pl.BlockSpec((tm, tn), lambda n, r, *_: (r, n)),
                      pl.BlockSpec((tm, STAT_LANES), lambda n, r, *_: (r, n))]
    kern = functools.partial(_ws_kernel, n_lhs=len(lhs), w_lhs=tuple(w_lhs),
                             n_extra=len(extras), epilogue=epilogue, sub_rows=sub_rows, tn=tn,
                             scaled=col_scale is not None, round_once=round_once,
                             normed_lhs=row_sumsq is not None, emit_prenorm=next_gain is not None)
    if round_once:
        scratch = ([pltpu.VMEM((w.shape[1], tn), F32) for w in ws]
                   + [pltpu.VMEM((w.shape[1], tn), BF16) for w in ws]
                   + [pltpu.SemaphoreType.DMA((len(ws),))])
    else:
        scratch = ([pltpu.VMEM((2, w.shape[1], tn), F32) for w in ws]
                   + [pltpu.SemaphoreType.DMA((len(ws), 2)), pltpu.SMEM((1,), jnp.int32)])
    outs = pl.pallas_call(
        kern,
        out_shape=out_shapes,
        grid_spec=pltpu.PrefetchScalarGridSpec(
            num_scalar_prefetch=4,
            grid=(n_total // tn, n_rows),
            in_specs=in_specs,
            out_specs=out_specs,
            scratch_shapes=scratch,
        ),
        compiler_params=_cparams(("arbitrary", "arbitrary")),
        name=name,
    )(texp, tnext, tvalid, lhs_rows, *lhs, *ws, *extras, *scale_args)
    return outs if next_gain is not None else outs[0]


def _ep_plain(prods, extras):
    return prods[0]


def _ep_residual(prods, extras):
    return extras[0] + prods[0]


def _ep_swiglu(prods, extras):
    g, u = prods
    return (g * jax.nn.sigmoid(g)) * u


def _ep_merge(prods, extras):
    ga, gb = extras
    return (jax.nn.sigmoid(ga.astype(F32)) * prods[0]
            + jax.nn.sigmoid(gb.astype(F32)) * prods[1])


def _attn_head(q_ref, k_ref, v_ref, o_ref, cols, *, nb, blk, topk):
    seq = k_ref.shape[0]
    neg_inf = jnp.float32(-jnp.inf)
    avg = jnp.where(lax.broadcasted_iota(jnp.int32, (nb, seq), 1) // blk
                    == lax.broadcasted_iota(jnp.int32, (nb, seq), 0), 1.0 / blk, 0.0).astype(BF16)
    k_mean = jnp.dot(avg, k_ref[:, cols], preferred_element_type=F32)
    k_mean_hi = k_mean.astype(BF16).astype(F32)
    k_mean_parts = jnp.concatenate([k_mean_hi, k_mean - k_mean_hi], axis=0).astype(BF16)
    blk_id = lax.broadcasted_iota(jnp.int32, (nb, blk), 0)
    causal = (lax.broadcasted_iota(jnp.int32, (blk, blk), 0)
              <= lax.broadcasted_iota(jnp.int32, (blk, blk), 1))
    nt_dims = (((1,), (1,)), ((), ()))
    v_t = v_ref[:, cols].T

    def scores(n):
        qn = q_ref[n * blk:(n + 1) * blk, cols]
        s = lax.dot_general(k_ref[0:(n + 1) * blk, cols], qn, nt_dims, preferred_element_type=F32)
        gate_parts = None
        if n > topk:
            gate_parts = lax.dot_general(k_mean_parts, qn, nt_dims, preferred_element_type=F32)
        return s, gate_parts

    def masked_softmax(n, s, gate_parts):
        past = [s[j * blk:(j + 1) * blk, :] for j in range(n)]
        if n > topk:
            gate = gate_parts[:nb] + gate_parts[nb:]
            beaten = jnp.zeros((nb, blk), F32)
            for j in range(n):
                gj = gate[j:j + 1, :]
                wins = jnp.logical_or(gj > gate, jnp.logical_and(gj == gate, j < blk_id))
                beaten = beaten + wins.astype(F32)
            bias = jnp.where(beaten < topk, 0.0, neg_inf)
            past = [past[j] + bias[j:j + 1, :] for j in range(n)]
        own = jnp.where(causal, s[n * blk:(n + 1) * blk, :], neg_inf)
        s = jnp.concatenate(past + [own], axis=0)
        m = jnp.max(s, axis=0, keepdims=True)
        p = jnp.exp2(s - m)
        return p.astype(BF16), jnp.sum(p, axis=0, keepdims=True)

    def weighted_values(n, p, l):
        o_t = jnp.dot(v_t[:, 0:(n + 1) * blk], p, preferred_element_type=F32)
        o_ref[n * blk:(n + 1) * blk, cols] = (o_t / l).T.astype(o_ref.dtype)

    order = list(range(nb - 1, -1, -1))
    s_out, p_out = {}, {}
    for t in range(nb + SOFTMAX_LAG + VALUES_LAG):
        if t < nb:
            s_out[order[t]] = scores(order[t])
        if 0 <= t - SOFTMAX_LAG < nb:
            n = order[t - SOFTMAX_LAG]
            p_out[n] = masked_softmax(n, *s_out.pop(n))
        if 0 <= t - SOFTMAX_LAG - VALUES_LAG < nb:
            n = order[t - SOFTMAX_LAG - VALUES_LAG]
            weighted_values(n, *p_out.pop(n))
        yield


def _attn_kernel(q_ref, k_ref, v_ref, o_ref, *, heads, dh, nb, blk, topk):
    emitters = [_attn_head(q_ref, k_ref, v_ref, o_ref, slice(h * dh, (h + 1) * dh),
                           nb=nb, blk=blk, topk=topk) for h in range(heads)]
    while emitters:
        emitters = [e for e in emitters if next(e, StopIteration) is not StopIteration]


def moba_attention(z, batch, seq):
    nb = seq // MOBA_BLOCK
    kern = functools.partial(_attn_kernel, heads=ATTN_HEADS_PER_STEP, dh=HEAD_DIM, nb=nb,
                             blk=MOBA_BLOCK, topk=MOBA_TOPK)
    width = ATTN_HEADS_PER_STEP * HEAD_DIM
    steps = ATTN_HEADS // ATTN_HEADS_PER_STEP
    blockspec = lambda off: pl.BlockSpec((None, seq, width), lambda b, h: (b, 0, off + h))
    return pl.pallas_call(
        kern,
        out_shape=jax.ShapeDtypeStruct((batch, seq, ATTN_HEADS * HEAD_DIM), BF16),
        grid=(batch, steps),
        in_specs=[blockspec(0), blockspec(steps), blockspec(2 * steps)],
        out_specs=pl.BlockSpec((None, seq, width), lambda b, h: (b, 0, h)),
        compiler_params=_cparams(("parallel", "parallel")),
        name="moba_attention",
    )(z, z, z)


def _gelu_tanh(x):
    return 0.5 * x * (1.0 + jnp.tanh(0.7978845608028654 * (x + 0.044715 * (x * x * x))))


def _sgu_kernel(u_ref, v_ref, g_ref, w_ref, bt_ref, o_ref, *, chunk, groups, gd):
    tm = u_ref.shape[0]
    u = _gelu_tanh(u_ref[...].astype(F32))
    v = _gelu_tanh(v_ref[...].astype(F32))
    mu = jnp.mean(v, axis=-1, keepdims=True)
    vc = v - mu
    var = jnp.mean(vc * vc, axis=-1, keepdims=True)
    vn = (vc * lax.rsqrt(var + NORM_EPS) * g_ref[...]).astype(BF16)
    t_i = lax.broadcasted_iota(jnp.int32, (chunk, chunk), 0)
    s_i = lax.broadcasted_iota(jnp.int32, (chunk, chunk), 1)
    lower = s_i <= t_i
    for g in range(groups):
        w = jnp.where(lower, w_ref[g], 0.0).astype(BF16)
        bias = bt_ref[:, g:g + 1]
        cols = slice(g * gd, (g + 1) * gd)
        for c in range(tm // chunk):
            rows = slice(c * chunk, (c + 1) * chunk)
            mixed = jnp.dot(w, vn[rows, cols], preferred_element_type=F32) + bias
            o_ref[rows, cols] = (u[rows, cols] * mixed).astype(o_ref.dtype)


def spatial_gating(z, g_sgu, w_s, b_s, col_u, col_v, tm=256):
    n = z.shape[0]
    width = SGU_GROUPS * SGU_GROUP_DIM
    kern = functools.partial(_sgu_kernel, chunk=SGU_CHUNK, groups=SGU_GROUPS, gd=SGU_GROUP_DIM)
    return pl.pallas_call(
        kern,
        out_shape=jax.ShapeDtypeStruct((n, width), BF16),
        grid=(n // tm,),
        in_specs=[pl.BlockSpec((tm, width), lambda i: (i, col_u)),
                  pl.BlockSpec((tm, width), lambda i: (i, col_v)),
                  pl.BlockSpec((1, width), lambda i: (0, 0)),
                  pl.BlockSpec(w_s.shape, lambda i: (0, 0, 0)),
                  pl.BlockSpec((SGU_CHUNK, SGU_GROUPS), lambda i: (0, 0))],
        out_specs=pl.BlockSpec((tm, width), lambda i: (i, 0)),
        compiler_params=_cparams(("parallel",)),
        name="spatial_gating",
    )(z, z, g_sgu.reshape(1, width), w_s, b_s.T)


def _router_kernel(x_ref, g_ref, rwt_ref, rb_ref, eidx_ref, wts_ref, rank_ref, cnt_ref):
    i = pl.program_id(0)
    ne = rwt_ref.shape[0]
    tm = x_ref.shape[0]

    @pl.when(i == 0)
    def _():
        cnt_ref[...] = jnp.zeros_like(cnt_ref)

    h = _rms(x_ref[...], g_ref[...])
    h_hi = h.astype(BF16)
    h_lo = (h - h_hi.astype(F32)).astype(BF16)
    rw = rwt_ref[...]
    rw_hi = rw.astype(BF16)
    rw_hi_f32 = rw_hi.astype(F32)
    rw_parts = jnp.concatenate([rw_hi_f32, rw - rw_hi_f32], axis=0).astype(BF16)
    nt_dims = (((1,), (1,)), ((), ()))
    both = lax.dot_general(rw_parts, h_hi, nt_dims, preferred_element_type=F32)
    logits = (both[:ne] + both[ne:]
              + lax.dot_general(rw_hi, h_lo, nt_dims, preferred_element_type=F32)
              + rb_ref[...])
    row = lax.broadcasted_iota(jnp.int32, (ne, tm), 0).astype(F32)
    neg_inf = jnp.float32(-jnp.inf)
    m0 = jnp.max(logits, axis=0, keepdims=True)
    i0 = jnp.min(jnp.where(logits == m0, row, float(ne)), axis=0, keepdims=True)
    rest = jnp.where(row == i0, neg_inf, logits)
    m1 = jnp.max(rest, axis=0, keepdims=True)
    i1 = jnp.min(jnp.where(jnp.logical_and(rest == m1, row != i0), row, float(ne)),
                 axis=0, keepdims=True)
    e1 = jnp.exp(m1 - m0)
    denom = 1.0 + e1
    eidx_ref[0:1, :] = i0.astype(jnp.int32)
    eidx_ref[1:2, :] = i1.astype(jnp.int32)
    wts_ref[0:1, :] = 1.0 / denom
    wts_ref[1:2, :] = e1 / denom

    hit0 = row == i0
    hit1 = row == i1
    chosen = jnp.logical_or(hit0, hit1).astype(BF16)
    earlier = (lax.broadcasted_iota(jnp.int32, (tm, tm), 0)
               < lax.broadcasted_iota(jnp.int32, (tm, tm), 1)).astype(BF16)
    before = jnp.dot(chosen, earlier, preferred_element_type=F32) + cnt_ref[:, 0:1]
    rank_ref[0:1, :] = jnp.sum(jnp.where(hit0, before, 0.0), axis=0, keepdims=True).astype(jnp.int32)
    rank_ref[1:2, :] = jnp.sum(jnp.where(hit1, before, 0.0), axis=0, keepdims=True).astype(jnp.int32)
    cnt_ref[...] = cnt_ref[...] + jnp.sum(chosen.astype(F32), axis=1, keepdims=True)


def moe_router(x, g, router_w, router_b, tm=ROW_TILE):
    n, d = x.shape
    ne = router_w.shape[1]
    out_shapes = (jax.ShapeDtypeStruct((TOP_K, n), jnp.int32),
                  jax.ShapeDtypeStruct((TOP_K, n), F32),
                  jax.ShapeDtypeStruct((TOP_K, n), jnp.int32),
                  jax.ShapeDtypeStruct((ne, 128), F32))
    tok_spec = pl.BlockSpec((TOP_K, tm), lambda i: (0, i))
    return pl.pallas_call(
        _router_kernel,
        out_shape=out_shapes,
        grid=(n // tm,),
        in_specs=[pl.BlockSpec((tm, d), lambda i: (i, 0)),
                  pl.BlockSpec((1, d), lambda i: (0, 0)),
                  pl.BlockSpec((ne, d), lambda i: (0, 0)),
                  pl.BlockSpec((ne, 1), lambda i: (0, 0))],
        out_specs=(tok_spec, tok_spec, tok_spec, pl.BlockSpec((ne, 128), lambda i: (0, 0))),
        compiler_params=_cparams(("arbitrary",)),
        name="moe_router",
    )(x, g.reshape(1, d), router_w.T, router_b.reshape(ne, 1))


ISSUE_UNROLL = 8


def _start_row_gather(src_hbm, idx_ref, idx_base, n_rows, buf, slot, row_base, sem):
    def body(j, carry):
        row = idx_ref[idx_base + j]
        pltpu.make_async_copy(src_hbm.at[pl.ds(row, 1), :],
                              buf.at[slot, pl.ds(row_base + j, 1), :], sem.at[slot]).start()
        return carry
    lax.fori_loop(0, n_rows, body, 0, unroll=ISSUE_UNROLL)


def _wait_row_gather(src_hbm, buf, slot, sem):
    n_rows = buf.shape[1]
    pltpu.make_async_copy(src_hbm.at[pl.ds(0, n_rows), :], buf.at[slot], sem.at[slot]).wait()


def _dispatch_kernel(src_ref, used_ref, x_hbm, g_ref, o_ref, buf, sem, *, tg):
    i = pl.program_id(0)
    n_used = used_ref[0]

    @pl.when(i == 0)
    def _():
        _start_row_gather(x_hbm, src_ref, 0, tg, buf, 0, 0, sem)

    @pl.when(i + 1 < n_used)
    def _():
        _start_row_gather(x_hbm, src_ref, (i + 1) * tg, tg, buf, jnp.bitwise_and(i + 1, 1), 0, sem)

    @pl.when(i < n_used)
    def _():
        slot = jnp.bitwise_and(i, 1)
        _wait_row_gather(x_hbm, buf, slot, sem)
        o_ref[...] = _rms(buf[slot], g_ref[...]).astype(o_ref.dtype)

    @pl.when(i >= n_used)
    def _():
        o_ref[...] = jnp.zeros_like(o_ref)


def moe_dispatch(x, g, src, n_used_steps, tg):
    n, d = x.shape
    n_slots = src.shape[0]
    kern = functools.partial(_dispatch_kernel, tg=tg)
    return pl.pallas_call(
        kern,
        out_shape=jax.ShapeDtypeStruct((n_slots, d), BF16),
        grid_spec=pltpu.PrefetchScalarGridSpec(
            num_scalar_prefetch=2,
            grid=(n_slots // tg,),
            in_specs=[pl.BlockSpec(memory_space=pl.ANY),
                      pl.BlockSpec((1, d), lambda i, *_: (0, 0))],
            out_specs=pl.BlockSpec((tg, d), lambda i, *_: (i, 0)),
            scratch_shapes=[pltpu.VMEM((2, tg, d), F32), pltpu.SemaphoreType.DMA((2,))],
        ),
        compiler_params=_cparams(("arbitrary",)),
        name="moe_dispatch",
    )(src, n_used_steps, x, g.reshape(1, d))


def _combine_kernel(slot_ref, y_hbm, x_ref, w_ref, g_ref, o_ref, buf, sem, *, tc, final_norm):
    i = pl.program_id(0)
    nsteps = pl.num_programs(0)
    n_tok = nsteps * tc

    def fetch(step, slot):
        for k in range(TOP_K):
            _start_row_gather(y_hbm, slot_ref, k * n_tok + step * tc, tc, buf, slot, k * tc, sem)

    @pl.when(i == 0)
    def _():
        fetch(0, 0)

    @pl.when(i + 1 < nsteps)
    def _():
        fetch(i + 1, jnp.bitwise_and(i + 1, 1))

    slot = jnp.bitwise_and(i, 1)
    _wait_row_gather(y_hbm, buf, slot, sem)
    acc = x_ref[...]
    for k in range(TOP_K):
        acc = acc + w_ref[:, k:k + 1] * buf[slot, k * tc:(k + 1) * tc, :]
    if final_norm:
        acc = _rms(acc, g_ref[...])
    o_ref[...] = acc


def moe_combine(y, x, slot, wts, g, final_norm, tc=128):
    n, d = x.shape
    kern = functools.partial(_combine_kernel, tc=tc, final_norm=final_norm)
    return pl.pallas_call(
        kern,
        out_shape=jax.ShapeDtypeStruct((n, d), F32),
        grid_spec=pltpu.PrefetchScalarGridSpec(
            num_scalar_prefetch=1,
            grid=(n // tc,),
            in_specs=[pl.BlockSpec(memory_space=pl.ANY),
                      pl.BlockSpec((tc, d), lambda i, s: (i, 0)),
                      pl.BlockSpec((tc, TOP_K), lambda i, s: (i, 0)),
                      pl.BlockSpec((1, d), lambda i, s: (0, 0))],
            out_specs=pl.BlockSpec((tc, d), lambda i, s: (i, 0)),
            scratch_shapes=[pltpu.VMEM((2, TOP_K * tc, d), F32), pltpu.SemaphoreType.DMA((2,))],
        ),
        compiler_params=_cparams(("arbitrary",)),
        name="moe_combine",
    )(slot.reshape(-1), y, x, wts.T, g.reshape(1, d))


def token_mixer(x, pre, batch, seq, layer, w_in, g_sgu, w_s, b_s, w_pa, w_pb, w_o, next_gain):
    n, d = x.shape
    aw = ATTN_HEADS * HEAD_DIM
    sw = SGU_GROUPS * SGU_GROUP_DIM
    tn = 1024
    xg, sumsq = pre
    q_scale = jnp.where(jnp.arange(w_in.shape[2]) < aw, HEAD_DIM ** -0.5 * LOG2_E, 1.0).astype(F32)
    in_tm = WIDE_ROW_TILE if sumsq.shape[1] == STAT_LANES else DENSE_ROW_TILE
    z = ws_matmul([xg], [w_in], [0], [], [], _ep_plain, BF16, tn, w_base=layer,
                  col_scale=q_scale, row_sumsq=sumsq, tm=in_tm, name="in_proj")
    attn = moba_attention(z.reshape(batch, seq, z.shape[1]), batch, seq).reshape(n, aw)
    sgu = spatial_gating(z, g_sgu, w_s, b_s, (3 * aw) // sw, (3 * aw + sw) // sw)
    gate_col = (3 * aw + 2 * sw) // tn
    merged = ws_matmul([attn, sgu], [w_pa, w_pb], [0, 1], [z, z],
                       [gate_col, gate_col + d // tn], _ep_merge, BF16, tn, w_base=layer,
                       tm=DENSE_ROW_TILE, name="branch_merge")
    return ws_matmul([merged], [w_o], [0], [x], [0], _ep_residual, F32, tn, w_base=layer,
                     next_gain=next_gain, tm=DENSE_ROW_TILE, name="out_proj")


def dense_ffn(x, pre, j, wg, wu, wd, next_gain):
    xg, sumsq = pre
    act = ws_matmul([xg], [wg, wu], [0, 0], [], [], _ep_swiglu, BF16, 512, w_base=j,
                    round_once=True, row_sumsq=sumsq, tm=DENSE_ROW_TILE, name="ffn_up")
    return ws_matmul([act], [wd], [0], [x], [0], _ep_residual, F32, 512, w_base=j,
                     round_once=True, next_gain=next_gain, tm=DENSE_ROW_TILE, name="ffn_down")


def moe_ffn(x, norm_g, j, router_w, router_b, wg, wu, wd, final_g, final_norm):
    n, d = x.shape
    ne = router_w.shape[1]
    tm = ROW_TILE
    eidx, wts, rank, cnt = moe_router(x, norm_g, router_w, router_b)
    counts = cnt[:, 0].astype(jnp.int32)
    padded = ((counts + tm - 1) // tm) * tm
    ends = jnp.cumsum(padded)
    starts = ends - padded
    group_start = jnp.sum(jnp.where(eidx[..., None] == jnp.arange(ne), starts, 0), axis=-1)
    slot = group_start + rank
    n_tiles = (TOP_K * n) // tm + ne
    tok = jnp.tile(jnp.arange(n, dtype=jnp.int32), TOP_K)
    src = (jnp.arange(n_tiles * tm, dtype=jnp.int32) % n).at[slot.reshape(-1)].set(tok)
    tile_row = jnp.arange(n_tiles, dtype=jnp.int32) * tm
    last_used = jnp.max(jnp.where(counts > 0, jnp.arange(ne, dtype=jnp.int32), 0))
    texp = jnp.minimum(jnp.sum(tile_row[:, None] >= ends[None, :], axis=1), last_used).astype(jnp.int32)
    group_end = jnp.sum(jnp.where(texp[:, None] == jnp.arange(ne), starts + counts, 0), axis=-1)
    tvalid = jnp.clip(group_end - tile_row, 0, tm).astype(jnp.int32)

    n_used_tiles = ends[-1] // tm
    lhs_rows = jnp.minimum(jnp.arange(n_tiles, dtype=jnp.int32), n_used_tiles - 1)
    n_used_steps = (n_used_tiles * (tm // DISPATCH_ROWS)).reshape(1).astype(jnp.int32)

    hs = moe_dispatch(x, norm_g, src, n_used_steps, DISPATCH_ROWS)
    act = ws_matmul([hs], [wg, wu], [0, 0], [], [], _ep_swiglu, BF16, 512, texp=texp,
                    tvalid=tvalid, sub_rows=MOE_SUB_ROWS, w_base=j * ne, lhs_rows=lhs_rows,
                    name="moe_up")
    y = ws_matmul([act], [wd], [0], [], [], _ep_plain, F32, 1024, texp=texp, tvalid=tvalid,
                  sub_rows=MOE_SUB_ROWS, w_base=j * ne, round_once=True, lhs_rows=lhs_rows,
                  name="moe_down")
    return moe_combine(y, x, slot, wts, final_g, final_norm)


def kernel(x, mix_norm_g, w_in, sgu_norm_g, w_s, b_s, w_pa, w_pb, w_o, ffn_norm_g,
           dense_w_gate, dense_w_up, dense_w_down, router_w, router_b,
           expert_w_gate, expert_w_up, expert_w_down, final_norm_g):
    batch, seq, d = x.shape
    depth = mix_norm_g.shape[0]
    xf = x.reshape(batch * seq, d)
    merge_experts = lambda w: w.reshape((w.shape[0] * w.shape[1],) + w.shape[2:])
    ewg, ewu, ewd = (merge_experts(w) for w in (expert_w_gate, expert_w_up, expert_w_down))
    pre = prenorm(xf, mix_norm_g[0])
    normed = False
    for i in range(depth):
        j = i // 2
        last = i == depth - 1
        dense = i % 2 == 0
        mixed = token_mixer(xf, pre, batch, seq, i, w_in, sgu_norm_g[i], w_s[i], b_s[i],
                            w_pa, w_pb, w_o, ffn_norm_g[i] if dense else None)
        if dense:
            xf, *pre = mixed
            out = dense_ffn(xf, pre, j, dense_w_gate, dense_w_up, dense_w_down,
                            None if last else mix_norm_g[i + 1])
            if last:
                xf = out
            else:
                xf, *pre = out
        else:
            xf = moe_ffn(mixed, ffn_norm_g[i], j, router_w[j], router_b[j], ewg, ewu, ewd,
                         final_norm_g, last)
            normed = last
            if not last:
                pre = prenorm(xf, mix_norm_g[i + 1])
    if not normed:
        xf = rmsnorm(xf, final_norm_g, F32)
    return xf.reshape(batch, seq, d)
```

```python
import functools

import jax
import jax.numpy as jnp
from jax import lax
from jax.experimental import pallas as pl
from jax.experimental.pallas import tpu as pltpu

F32 = jnp.float32
BF16 = jnp.bfloat16

ATTN_HEADS = 8
HEAD_DIM = 128
MOBA_BLOCK = 256
MOBA_TOPK = 3
SGU_GROUPS = 8
SGU_GROUP_DIM = 128
SGU_CHUNK = 128
TOP_K = 2
NORM_EPS = 1e-6
STAT_LANES = 128
LOG2_E = 1.4426950408889634

V7X_VMEM_BYTES = 64 * 1024 * 1024
VMEM_LIMIT_BYTES = V7X_VMEM_BYTES - 8 * 1024 * 1024

ROW_TILE = 512
DENSE_ROW_TILE = 1024
WIDE_ROW_TILE = 2048
CAST_ROWS = 256
MOE_SUB_ROWS = 128
DISPATCH_ROWS = 256
SOFTMAX_LAG = 2
VALUES_LAG = 1
ATTN_HEADS_PER_STEP = 4


def _cparams(sem):
    return pltpu.CompilerParams(dimension_semantics=sem, vmem_limit_bytes=VMEM_LIMIT_BYTES)


def _rms(x, g):
    ms = jnp.mean(x * x, axis=-1, keepdims=True)
    return x * lax.rsqrt(ms + NORM_EPS) * g


def _rmsnorm_kernel(x_ref, g_ref, o_ref):
    o_ref[...] = _rms(x_ref[...], g_ref[...]).astype(o_ref.dtype)


def rmsnorm(x, g, out_dtype, tm=ROW_TILE):
    n, d = x.shape
    return pl.pallas_call(
        _rmsnorm_kernel,
        out_shape=jax.ShapeDtypeStruct((n, d), out_dtype),
        grid=(n // tm,),
        in_specs=[pl.BlockSpec((tm, d), lambda i: (i, 0)),
                  pl.BlockSpec((1, d), lambda i: (0, 0))],
        out_specs=pl.BlockSpec((tm, d), lambda i: (i, 0)),
        compiler_params=_cparams(("parallel",)),
        name="rmsnorm",
    )(x, g.reshape(1, d))


def _prenorm_kernel(x_ref, g_ref, o_ref, s_ref):
    x = x_ref[...]
    o_ref[...] = (x * g_ref[...]).astype(o_ref.dtype)
    s_ref[...] = jnp.broadcast_to(jnp.sum(x * x, axis=1, keepdims=True), s_ref.shape)


def prenorm(x, g, tm=ROW_TILE):
    n, d = x.shape
    return pl.pallas_call(
        _prenorm_kernel,
        out_shape=(jax.ShapeDtypeStruct((n, d), BF16), jax.ShapeDtypeStruct((n, STAT_LANES), F32)),
        grid=(n // tm,),
        in_specs=[pl.BlockSpec((tm, d), lambda i: (i, 0)),
                  pl.BlockSpec((1, d), lambda i: (0, 0))],
        out_specs=(pl.BlockSpec((tm, d), lambda i: (i, 0)),
                   pl.BlockSpec((tm, STAT_LANES), lambda i: (i, 0))),
        compiler_params=_cparams(("parallel",)),
        name="prenorm",
    )(x, g.reshape(1, d))


def _ws_kernel(texp_ref, tnext_ref, tvalid_ref, lhs_rows_ref, *refs, n_lhs, w_lhs, n_extra, epilogue,
               sub_rows, tn, scaled, round_once, normed_lhs, emit_prenorm):
    n_w = len(w_lhs)
    lhs_refs = refs[:n_lhs]
    w_hbm = refs[n_lhs:n_lhs + n_w]
    extra_refs = refs[n_lhs + n_w:n_lhs + n_w + n_extra]
    pos = n_lhs + n_w + n_extra
    scale_ref = refs[pos] if scaled else None
    pos += int(scaled)
    stats_ref = refs[pos] if normed_lhs else None
    pos += int(normed_lhs)
    gain_ref = refs[pos] if emit_prenorm else None
    pos += int(emit_prenorm)
    out_ref = refs[pos]
    prenorm_ref, sumsq_ref = refs[pos + 1:pos + 3] if emit_prenorm else (None, None)
    scratch = refs[pos + 1 + 2 * int(emit_prenorm):]
    wf_refs = scratch[:n_w]
    tm = out_ref.shape[0]
    if round_once:
        wb_refs = scratch[n_w:2 * n_w]
        sem = scratch[2 * n_w]
    else:
        sem, groups_seen = scratch[n_w:]

    n = pl.program_id(0)
    r = pl.program_id(1)
    n_col_tiles = pl.num_programs(0)

    def weight_copy(i, index, col_tile, slot):
        cols = pl.ds(pl.multiple_of(col_tile * tn, tn), tn)
        dst, dst_sem = (wf_refs[i], sem.at[i]) if round_once else (wf_refs[i].at[slot], sem.at[i, slot])
        return pltpu.make_async_copy(w_hbm[i].at[index, :, cols], dst, dst_sem)

    def start_weights(index, col_tile, slot):
        for i in range(n_w):
            weight_copy(i, index, col_tile, slot).start()

    def start_following(slot):
        following = tnext_ref[r]

        @pl.when(following >= 0)
        def _():
            start_weights(following, n, slot)

        @pl.when(jnp.logical_and(following < 0, n + 1 < n_col_tiles))
        def _():
            start_weights(texp_ref[0], n + 1, slot)

    def rounded(w):
        if scaled:
            w = w * scale_ref[...]
        return w.astype(BF16)

    @pl.when(jnp.logical_and(n == 0, r == 0))
    def _():
        if not round_once:
            groups_seen[0] = 0
        start_weights(texp_ref[0], 0, 0)

    new_group = jnp.logical_or(r == 0, texp_ref[r] != texp_ref[jnp.maximum(r - 1, 0)])

    @pl.when(new_group)
    def _():
        if round_once:
            for i in range(n_w):
                weight_copy(i, texp_ref[r], n, 0).wait()
            for wf_ref, wb_ref in zip(wf_refs, wb_refs):
                def cast_rows(c, carry, wf_ref=wf_ref, wb_ref=wb_ref):
                    rows = pl.ds(pl.multiple_of(c * CAST_ROWS, CAST_ROWS), CAST_ROWS)
                    wb_ref[rows, :] = rounded(wf_ref[rows, :])
                    return carry

                lax.fori_loop(0, wb_ref.shape[0] // CAST_ROWS, cast_rows, 0)
            start_following(0)
        else:
            seen = groups_seen[0]
            slot = jnp.bitwise_and(seen, 1)
            groups_seen[0] = seen + 1
            for i in range(n_w):
                weight_copy(i, texp_ref[r], n, slot).wait()
            start_following(1 - slot)

    def operand(i):
        if round_once:
            return wb_refs[i][...]
        return rounded(wf_refs[i][jnp.bitwise_and(groups_seen[0] - 1, 1)])

    def compute(n_live):
        rows = slice(0, n_live)
        if n_live > 0:
            prods = [jnp.dot(lhs_refs[w_lhs[i]][rows, :], operand(i),
                             preferred_element_type=F32) for i in range(n_w)]
            if normed_lhs:
                width = lhs_refs[0].shape[1]
                sumsq = sum(stats_ref[rows, p:p + 1] for p in range(0, stats_ref.shape[1], STAT_LANES))
                inv_rms = lax.rsqrt(sumsq / width + NORM_EPS)
                prods = [p * inv_rms for p in prods]
            res = epilogue(prods, [e[rows, :] for e in extra_refs])
            out_ref[rows, :] = res.astype(out_ref.dtype)
            if emit_prenorm:
                prenorm_ref[rows, :] = (res * gain_ref[...]).astype(prenorm_ref.dtype)
                sumsq_ref[rows, :] = jnp.broadcast_to(jnp.sum(res * res, axis=1, keepdims=True),
                                                      (n_live, STAT_LANES))
        if n_live < tm:
            for ref in (out_ref, prenorm_ref, sumsq_ref):
                if ref is not None:
                    ref[n_live:tm, :] = jnp.zeros((tm - n_live, ref.shape[1]), ref.dtype)

    if sub_rows is None:
        compute(tm)
    else:
        live_chunks = (tvalid_ref[r] + (sub_rows - 1)) // sub_rows
        for c in range(tm // sub_rows + 1):
            pl.when(live_chunks == c)(functools.partial(compute, c * sub_rows))


def ws_matmul(lhs, ws, w_lhs, extras, extra_col_off, epilogue, out_dtype, tn, texp=None,
              tvalid=None, sub_rows=None, w_base=0, col_scale=None, round_once=False,
              row_sumsq=None, next_gain=None, lhs_rows=None, tm=ROW_TILE, name="ws_matmul"):
    m = lhs[0].shape[0]
    n_total = ws[0].shape[2]
    n_rows = m // tm
    if texp is None:
        texp = jnp.zeros((n_rows,), jnp.int32)
    if tvalid is None:
        tvalid = jnp.full((n_rows,), tm, jnp.int32)
    no_next = jnp.iinfo(jnp.int32).max
    later = jnp.min(jnp.where(texp[None, :] > texp[:, None], texp[None, :], no_next), axis=1)
    tnext = jnp.where(later == no_next, -1, later + w_base).astype(jnp.int32)
    texp = texp + w_base
    if lhs_rows is None:
        lhs_rows = jnp.arange(n_rows, dtype=jnp.int32)
    in_specs = []
    for a in lhs:
        in_specs.append(pl.BlockSpec((tm, a.shape[1]), lambda n, r, t, tn_, tv, lr: (lr[r], 0)))
    for w in ws:
        in_specs.append(pl.BlockSpec(memory_space=pl.ANY))
    for off in extra_col_off:
        in_specs.append(pl.BlockSpec((tm, tn), lambda n, r, *_, off=off: (r, off + n)))
    scale_args = []
    if col_scale is not None:
        in_specs.append(pl.BlockSpec((1, tn), lambda n, r, *_: (0, n)))
        scale_args.append(col_scale.reshape(1, n_total))
    if row_sumsq is not None:
        in_specs.append(pl.BlockSpec((tm, row_sumsq.shape[1]), lambda n, r, *_: (r, 0)))
        scale_args.append(row_sumsq)
    out_shapes = [jax.ShapeDtypeStruct((m, n_total), out_dtype)]
    out_specs = [pl.BlockSpec((tm, tn), lambda n, r, *_: (r, n))]
    if next_gain is not None:
        in_specs.append(pl.BlockSpec((1, tn), lambda n, r, *_: (0, n)))
        scale_args.append(next_gain.reshape(1, n_total))
        out_shapes += [jax.ShapeDtypeStruct((m, n_total), BF16),
                       jax.ShapeDtypeStruct((m, (n_total // tn) * STAT_LANES), F32)]
        out_specs += [pl.BlockSpec((tm, tn), lambda n, r, *_: (r, n)),
                      pl.BlockSpec((tm, STAT_LANES), lambda n, r, *_: (r, n))]
    kern = functools.partial(_ws_kernel, n_lhs=len(lhs), w_lhs=tuple(w_lhs),
                             n_extra=len(extras), epilogue=epilogue, sub_rows=sub_rows, tn=tn,
                             scaled=col_scale is not None, round_once=round_once,
                             normed_lhs=row_sumsq is not None, emit_prenorm=next_gain is not None)
    if round_once:
        scratch = ([pltpu.VMEM((w.shape[1], tn), F32) for w in ws]
                   + [pltpu.VMEM((w.shape[1], tn), BF16) for w in ws]
                   + [pltpu.SemaphoreType.DMA((len(ws),))])
    else:
        scratch = ([pltpu.VMEM((2, w.shape[1], tn), F32) for w in ws]
                   + [pltpu.SemaphoreType.DMA((len(ws), 2)), pltpu.SMEM((1,), jnp.int32)])
    outs = pl.pallas_call(
        kern,
        out_shape=out_shapes,
        grid_spec=pltpu.PrefetchScalarGridSpec(
            num_scalar_prefetch=4,
            grid=(n_total // tn, n_rows),
            in_specs=in_specs,
            out_specs=out_specs,
            scratch_shapes=scratch,
        ),
        compiler_params=_cparams(("arbitrary", "arbitrary")),
        name=name,
    )(texp, tnext, tvalid, lhs_rows, *lhs, *ws, *extras, *scale_args)
    return outs if next_gain is not None else outs[0]


def _ep_plain(prods, extras):
    return prods[0]


def _ep_residual(prods, extras):
    return extras[0] + prods[0]


def _ep_swiglu(prods, extras):
    g, u = prods
    return (g * jax.nn.sigmoid(g)) * u


def _ep_merge(prods, extras):
    ga, gb = extras
    return (jax.nn.sigmoid(ga.astype(F32)) * prods[0]
            + jax.nn.sigmoid(gb.astype(F32)) * prods[1])


def _attn_head(q_ref, k_ref, v_ref, o_ref, cols, *, nb, blk, topk):
    seq = k_ref.shape[0]
    neg_inf = jnp.float32(-jnp.inf)
    avg = jnp.where(lax.broadcasted_iota(jnp.int32, (nb, seq), 1) // blk
                    == lax.broadcasted_iota(jnp.int32, (nb, seq), 0), 1.0 / blk, 0.0).astype(BF16)
    k_mean = jnp.dot(avg, k_ref[:, cols], preferred_element_type=F32)
    k_mean_hi = k_mean.astype(BF16).astype(F32)
    k_mean_parts = jnp.concatenate([k_mean_hi, k_mean - k_mean_hi], axis=0).astype(BF16)
    blk_id = lax.broadcasted_iota(jnp.int32, (nb, blk), 0)
    causal = (lax.broadcasted_iota(jnp.int32, (blk, blk), 0)
              <= lax.broadcasted_iota(jnp.int32, (blk, blk), 1))
    nt_dims = (((1,), (1,)), ((), ()))
    v_t = v_ref[:, cols].T

    def scores(n):
        qn = q_ref[n * blk:(n + 1) * blk, cols]
        s = lax.dot_general(k_ref[0:(n + 1) * blk, cols], qn, nt_dims, preferred_element_type=F32)
        gate_parts = None
        if n > topk:
            gate_parts = lax.dot_general(k_mean_parts, qn, nt_dims, preferred_element_type=F32)
        return s, gate_parts

    def masked_softmax(n, s, gate_parts):
        past = [s[j * blk:(j + 1) * blk, :] for j in range(n)]
        if n > topk:
            gate = gate_parts[:nb] + gate_parts[nb:]
            beaten = jnp.zeros((nb, blk), F32)
            for j in range(n):
                gj = gate[j:j + 1, :]
                wins = jnp.logical_or(gj > gate, jnp.logical_and(gj == gate, j < blk_id))
                beaten = beaten + wins.astype(F32)
            bias = jnp.where(beaten < topk, 0.0, neg_inf)
            past = [past[j] + bias[j:j + 1, :] for j in range(n)]
        own = jnp.where(causal, s[n * blk:(n + 1) * blk, :], neg_inf)
        s = jnp.concatenate(past + [own], axis=0)
        m = jnp.max(s, axis=0, keepdims=True)
        p = jnp.exp2(s - m)
        return p.astype(BF16), jnp.sum(p, axis=0, keepdims=True)

    def weighted_values(n, p, l):
        o_t = jnp.dot(v_t[:, 0:(n + 1) * blk], p, preferred_element_type=F32)
        o_ref[n * blk:(n + 1) * blk, cols] = (o_t / l).T.astype(o_ref.dtype)

    order = list(range(nb - 1, -1, -1))
    s_out, p_out = {}, {}
    for t in range(nb + SOFTMAX_LAG + VALUES_LAG):
        if t < nb:
            s_out[order[t]] = scores(order[t])
        if 0 <= t - SOFTMAX_LAG < nb:
            n = order[t - SOFTMAX_LAG]
            p_out[n] = masked_softmax(n, *s_out.pop(n))
        if 0 <= t - SOFTMAX_LAG - VALUES_LAG < nb:
            n = order[t - SOFTMAX_LAG - VALUES_LAG]
            weighted_values(n, *p_out.pop(n))
        yield


def _attn_kernel(q_ref, k_ref, v_ref, o_ref, *, heads, dh, nb, blk, topk):
    emitters = [_attn_head(q_ref, k_ref, v_ref, o_ref, slice(h * dh, (h + 1) * dh),
                           nb=nb, blk=blk, topk=topk) for h in range(heads)]
    while emitters:
        emitters = [e for e in emitters if next(e, StopIteration) is not StopIteration]


def moba_attention(z, batch, seq):
    nb = seq // MOBA_BLOCK
    kern = functools.partial(_attn_kernel, heads=ATTN_HEADS_PER_STEP, dh=HEAD_DIM, nb=nb,
                             blk=MOBA_BLOCK, topk=MOBA_TOPK)
    width = ATTN_HEADS_PER_STEP * HEAD_DIM
    steps = ATTN_HEADS // ATTN_HEADS_PER_STEP
    blockspec = lambda off: pl.BlockSpec((None, seq, width), lambda b, h: (b, 0, off + h))
    return pl.pallas_call(
        kern,
        out_shape=jax.ShapeDtypeStruct((batch, seq, ATTN_HEADS * HEAD_DIM), BF16),
        grid=(batch, steps),
        in_specs=[blockspec(0), blockspec(steps), blockspec(2 * steps)],
        out_specs=pl.BlockSpec((None, seq, width), lambda b, h: (b, 0, h)),
        compiler_params=_cparams(("parallel", "parallel")),
        name="moba_attention",
    )(z, z, z)


GELU_C = 0.7978845608028654
GELU_A = 0.044715


def _gelu_tanh(x):
    half = 0.5 * x
    inner = x * (GELU_C + (GELU_C * GELU_A) * (x * x))
    return half + half * jnp.tanh(inner)


def _sgu_kernel(u_ref, v_ref, g_ref, w_ref, bt_ref, o_ref, w_lower, *, chunk, groups, gd):
    tm = u_ref.shape[0]

    @pl.when(pl.program_id(0) == 0)
    def _():
        lower = (lax.broadcasted_iota(jnp.int32, (chunk, chunk), 1)
                 <= lax.broadcasted_iota(jnp.int32, (chunk, chunk), 0))
        for g in range(groups):
            w_lower[g] = jnp.where(lower, w_ref[g], 0.0).astype(BF16)

    u = _gelu_tanh(u_ref[...].astype(F32))
    v = _gelu_tanh(v_ref[...].astype(F32))
    mu = jnp.mean(v, axis=-1, keepdims=True)
    vc = v - mu
    var = jnp.mean(vc * vc, axis=-1, keepdims=True)
    vn = (vc * lax.rsqrt(var + NORM_EPS) * g_ref[...]).astype(BF16)
    for g in range(groups):
        w = w_lower[g]
        bias = bt_ref[:, g:g + 1]
        cols = slice(g * gd, (g + 1) * gd)
        for c in range(tm // chunk):
            rows = slice(c * chunk, (c + 1) * chunk)
            mixed = jnp.dot(w, vn[rows, cols], preferred_element_type=F32) + bias
            o_ref[rows, cols] = (u[rows, cols] * mixed).astype(o_ref.dtype)


def spatial_gating(z, g_sgu, w_s, b_s, col_u, col_v, tm=256):
    n = z.shape[0]
    width = SGU_GROUPS * SGU_GROUP_DIM
    kern = functools.partial(_sgu_kernel, chunk=SGU_CHUNK, groups=SGU_GROUPS, gd=SGU_GROUP_DIM)
    return pl.pallas_call(
        kern,
        out_shape=jax.ShapeDtypeStruct((n, width), BF16),
        grid=(n // tm,),
        in_specs=[pl.BlockSpec((tm, width), lambda i: (i, col_u)),
                  pl.BlockSpec((tm, width), lambda i: (i, col_v)),
                  pl.BlockSpec((1, width), lambda i: (0, 0)),
                  pl.BlockSpec(w_s.shape, lambda i: (0, 0, 0)),
                  pl.BlockSpec((SGU_CHUNK, SGU_GROUPS), lambda i: (0, 0))],
        out_specs=pl.BlockSpec((tm, width), lambda i: (i, 0)),
        scratch_shapes=[pltpu.VMEM(w_s.shape, BF16)],
        compiler_params=_cparams(("arbitrary",)),
        name="spatial_gating",
    )(z, z, g_sgu.reshape(1, width), w_s, b_s.T)


def _router_kernel(x_ref, g_ref, rwt_ref, rb_ref, eidx_ref, wts_ref, rank_ref, cnt_ref):
    i = pl.program_id(0)
    ne = rwt_ref.shape[0]
    tm = x_ref.shape[0]

    @pl.when(i == 0)
    def _():
        cnt_ref[...] = jnp.zeros_like(cnt_ref)

    h = _rms(x_ref[...], g_ref[...])
    h_hi = h.astype(BF16)
    h_lo = (h - h_hi.astype(F32)).astype(BF16)
    rw = rwt_ref[...]
    rw_hi = rw.astype(BF16)
    rw_hi_f32 = rw_hi.astype(F32)
    rw_parts = jnp.concatenate([rw_hi_f32, rw - rw_hi_f32], axis=0).astype(BF16)
    nt_dims = (((1,), (1,)), ((), ()))
    both = lax.dot_general(rw_parts, h_hi, nt_dims, preferred_element_type=F32)
    logits = (both[:ne] + both[ne:]
              + lax.dot_general(rw_hi, h_lo, nt_dims, preferred_element_type=F32)
              + rb_ref[...])
    row = lax.broadcasted_iota(jnp.int32, (ne, tm), 0).astype(F32)
    neg_inf = jnp.float32(-jnp.inf)
    m0 = jnp.max(logits, axis=0, keepdims=True)
    i0 = jnp.min(jnp.where(logits == m0, row, float(ne)), axis=0, keepdims=True)
    rest = jnp.where(row == i0, neg_inf, logits)
    m1 = jnp.max(rest, axis=0, keepdims=True)
    i1 = jnp.min(jnp.where(jnp.logical_and(rest == m1, row != i0), row, float(ne)),
                 axis=0, keepdims=True)
    e1 = jnp.exp(m1 - m0)
    denom = 1.0 + e1
    eidx_ref[0:1, :] = i0.astype(jnp.int32)
    eidx_ref[1:2, :] = i1.astype(jnp.int32)
    wts_ref[0:1, :] = 1.0 / denom
    wts_ref[1:2, :] = e1 / denom

    hit0 = row == i0
    hit1 = row == i1
    chosen = jnp.logical_or(hit0, hit1).astype(BF16)
    earlier = (lax.broadcasted_iota(jnp.int32, (tm, tm), 0)
               < lax.broadcasted_iota(jnp.int32, (tm, tm), 1)).astype(BF16)
    before = jnp.dot(chosen, earlier, preferred_element_type=F32) + cnt_ref[:, 0:1]
    rank_ref[0:1, :] = jnp.sum(jnp.where(hit0, before, 0.0), axis=0, keepdims=True).astype(jnp.int32)
    rank_ref[1:2, :] = jnp.sum(jnp.where(hit1, before, 0.0), axis=0, keepdims=True).astype(jnp.int32)
    cnt_ref[...] = cnt_ref[...] + jnp.sum(chosen.astype(F32), axis=1, keepdims=True)


def moe_router(x, g, router_w, router_b, tm=ROW_TILE):
    n, d = x.shape
    ne = router_w.shape[1]
    out_shapes = (jax.ShapeDtypeStruct((TOP_K, n), jnp.int32),
                  jax.ShapeDtypeStruct((TOP_K, n), F32),
                  jax.ShapeDtypeStruct((TOP_K, n), jnp.int32),
                  jax.ShapeDtypeStruct((ne, 128), F32))
    tok_spec = pl.BlockSpec((TOP_K, tm), lambda i: (0, i))
    return pl.pallas_call(
        _router_kernel,
        out_shape=out_shapes,
        grid=(n // tm,),
        in_specs=[pl.BlockSpec((tm, d), lambda i: (i, 0)),
                  pl.BlockSpec((1, d), lambda i: (0, 0)),
                  pl.BlockSpec((ne, d), lambda i: (0, 0)),
                  pl.BlockSpec((ne, 1), lambda i: (0, 0))],
        out_specs=(tok_spec, tok_spec, tok_spec, pl.BlockSpec((ne, 128), lambda i: (0, 0))),
        compiler_params=_cparams(("arbitrary",)),
        name="moe_router",
    )(x, g.reshape(1, d), router_w.T, router_b.reshape(ne, 1))


ISSUE_UNROLL = 8


def _start_row_gather(src_hbm, idx_ref, idx_base, n_rows, buf, slot, row_base, sem):
    def body(j, carry):
        row = idx_ref[idx_base + j]
        pltpu.make_async_copy(src_hbm.at[pl.ds(row, 1), :],
                              buf.at[slot, pl.ds(row_base + j, 1), :], sem.at[slot]).start()
        return carry
    lax.fori_loop(0, n_rows, body, 0, unroll=ISSUE_UNROLL)


def _wait_row_gather(src_hbm, buf, slot, sem):
    n_rows = buf.shape[1]
    pltpu.make_async_copy(src_hbm.at[pl.ds(0, n_rows), :], buf.at[slot], sem.at[slot]).wait()


def _dispatch_kernel(src_ref, used_ref, x_hbm, g_ref, o_ref, buf, sem, *, tg):
    i = pl.program_id(0)
    n_used = used_ref[0]

    @pl.when(i == 0)
    def _():
        _start_row_gather(x_hbm, src_ref, 0, tg, buf, 0, 0, sem)

    @pl.when(i + 1 < n_used)
    def _():
        _start_row_gather(x_hbm, src_ref, (i + 1) * tg, tg, buf, jnp.bitwise_and(i + 1, 1), 0, sem)

    @pl.when(i < n_used)
    def _():
        slot = jnp.bitwise_and(i, 1)
        _wait_row_gather(x_hbm, buf, slot, sem)
        o_ref[...] = _rms(buf[slot], g_ref[...]).astype(o_ref.dtype)

    @pl.when(i >= n_used)
    def _():
        o_ref[...] = jnp.zeros_like(o_ref)


def moe_dispatch(x, g, src, n_used_steps, tg):
    n, d = x.shape
    n_slots = src.shape[0]
    kern = functools.partial(_dispatch_kernel, tg=tg)
    return pl.pallas_call(
        kern,
        out_shape=jax.ShapeDtypeStruct((n_slots, d), BF16),
        grid_spec=pltpu.PrefetchScalarGridSpec(
            num_scalar_prefetch=2,
            grid=(n_slots // tg,),
            in_specs=[pl.BlockSpec(memory_space=pl.ANY),
                      pl.BlockSpec((1, d), lambda i, *_: (0, 0))],
            out_specs=pl.BlockSpec((tg, d), lambda i, *_: (i, 0)),
            scratch_shapes=[pltpu.VMEM((2, tg, d), F32), pltpu.SemaphoreType.DMA((2,))],
        ),
        compiler_params=_cparams(("arbitrary",)),
        name="moe_dispatch",
    )(src, n_used_steps, x, g.reshape(1, d))


def _combine_kernel(slot_ref, y_hbm, x_ref, w_ref, g_ref, o_ref, buf, sem, *, tc, final_norm):
    i = pl.program_id(0)
    nsteps = pl.num_programs(0)
    n_tok = nsteps * tc

    def fetch(step, slot):
        for k in range(TOP_K):
            _start_row_gather(y_hbm, slot_ref, k * n_tok + step * tc, tc, buf, slot, k * tc, sem)

    @pl.when(i == 0)
    def _():
        fetch(0, 0)

    @pl.when(i + 1 < nsteps)
    def _():
        fetch(i + 1, jnp.bitwise_and(i + 1, 1))

    slot = jnp.bitwise_and(i, 1)
    _wait_row_gather(y_hbm, buf, slot, sem)
    acc = x_ref[...]
    for k in range(TOP_K):
        acc = acc + w_ref[:, k:k + 1] * buf[slot, k * tc:(k + 1) * tc, :]
    if final_norm:
        acc = _rms(acc, g_ref[...])
    o_ref[...] = acc


def moe_combine(y, x, slot, wts, g, final_norm, tc=128):
    n, d = x.shape
    kern = functools.partial(_combine_kernel, tc=tc, final_norm=final_norm)
    return pl.pallas_call(
        kern,
        out_shape=jax.ShapeDtypeStruct((n, d), F32),
        grid_spec=pltpu.PrefetchScalarGridSpec(
            num_scalar_prefetch=1,
            grid=(n // tc,),
            in_specs=[pl.BlockSpec(memory_space=pl.ANY),
                      pl.BlockSpec((tc, d), lambda i, s: (i, 0)),
                      pl.BlockSpec((tc, TOP_K), lambda i, s: (i, 0)),
                      pl.BlockSpec((1, d), lambda i, s: (0, 0))],
            out_specs=pl.BlockSpec((tc, d), lambda i, s: (i, 0)),
            scratch_shapes=[pltpu.VMEM((2, TOP_K * tc, d), F32), pltpu.SemaphoreType.DMA((2,))],
        ),
        compiler_params=_cparams(("arbitrary",)),
        name="moe_combine",
    )(slot.reshape(-1), y, x, wts.T, g.reshape(1, d))


def token_mixer(x, pre, batch, seq, layer, w_in, g_sgu, w_s, b_s, w_pa, w_pb, w_o, next_gain):
    n, d = x.shape
    aw = ATTN_HEADS * HEAD_DIM
    sw = SGU_GROUPS * SGU_GROUP_DIM
    tn = 1024
    xg, sumsq = pre
    q_scale = jnp.where(jnp.arange(w_in.shape[2]) < aw, HEAD_DIM ** -0.5 * LOG2_E, 1.0).astype(F32)
    in_tm = WIDE_ROW_TILE if sumsq.shape[1] == STAT_LANES else DENSE_ROW_TILE
    z = ws_matmul([xg], [w_in], [0], [], [], _ep_plain, BF16, tn, w_base=layer,
                  col_scale=q_scale, row_sumsq=sumsq, tm=in_tm, name="in_proj")
    attn = moba_attention(z.reshape(batch, seq, z.shape[1]), batch, seq).reshape(n, aw)
    sgu = spatial_gating(z, g_sgu, w_s, b_s, (3 * aw) // sw, (3 * aw + sw) // sw)
    gate_col = (3 * aw + 2 * sw) // tn
    merged = ws_matmul([attn, sgu], [w_pa, w_pb], [0, 1], [z, z],
                       [gate_col, gate_col + d // tn], _ep_merge, BF16, tn, w_base=layer,
                       tm=DENSE_ROW_TILE, name="branch_merge")
    return ws_matmul([merged], [w_o], [0], [x], [0], _ep_residual, F32, tn, w_base=layer,
                     next_gain=next_gain, tm=DENSE_ROW_TILE, name="out_proj")


def dense_ffn(x, pre, j, wg, wu, wd, next_gain):
    xg, sumsq = pre
    act = ws_matmul([xg], [wg, wu], [0, 0], [], [], _ep_swiglu, BF16, 512, w_base=j,
                    round_once=True, row_sumsq=sumsq, tm=DENSE_ROW_TILE, name="ffn_up")
    return ws_matmul([act], [wd], [0], [x], [0], _ep_residual, F32, 512, w_base=j,
                     round_once=True, next_gain=next_gain, tm=DENSE_ROW_TILE, name="ffn_down")


def moe_ffn(x, norm_g, j, router_w, router_b, wg, wu, wd, final_g, final_norm):
    n, d = x.shape
    ne = router_w.shape[1]
    tm = ROW_TILE
    eidx, wts, rank, cnt = moe_router(x, norm_g, router_w, router_b)
    counts = cnt[:, 0].astype(jnp.int32)
    padded = ((counts + tm - 1) // tm) * tm
    ends = jnp.cumsum(padded)
    starts = ends - padded
    group_start = jnp.sum(jnp.where(eidx[..., None] == jnp.arange(ne), starts, 0), axis=-1)
    slot = group_start + rank
    n_tiles = (TOP_K * n) // tm + ne
    tok = jnp.tile(jnp.arange(n, dtype=jnp.int32), TOP_K)
    src = (jnp.arange(n_tiles * tm, dtype=jnp.int32) % n).at[slot.reshape(-1)].set(tok)
    tile_row = jnp.arange(n_tiles, dtype=jnp.int32) * tm
    last_used = jnp.max(jnp.where(counts > 0, jnp.arange(ne, dtype=jnp.int32), 0))
    texp = jnp.minimum(jnp.sum(tile_row[:, None] >= ends[None, :], axis=1), last_used).astype(jnp.int32)
    group_end = jnp.sum(jnp.where(texp[:, None] == jnp.arange(ne), starts + counts, 0), axis=-1)
    tvalid = jnp.clip(group_end - tile_row, 0, tm).astype(jnp.int32)

    n_used_tiles = ends[-1] // tm
    lhs_rows = jnp.minimum(jnp.arange(n_tiles, dtype=jnp.int32), n_used_tiles - 1)
    n_used_steps = (n_used_tiles * (tm // DISPATCH_ROWS)).reshape(1).astype(jnp.int32)

    hs = moe_dispatch(x, norm_g, src, n_used_steps, DISPATCH_ROWS)
    act = ws_matmul([hs], [wg, wu], [0, 0], [], [], _ep_swiglu, BF16, 512, texp=texp,
                    tvalid=tvalid, sub_rows=MOE_SUB_ROWS, w_base=j * ne, lhs_rows=lhs_rows,
                    name="moe_up")
    y = ws_matmul([act], [wd], [0], [], [], _ep_plain, F32, 1024, texp=texp, tvalid=tvalid,
                  sub_rows=MOE_SUB_ROWS, w_base=j * ne, round_once=True, lhs_rows=lhs_rows,
                  name="moe_down")
    return moe_combine(y, x, slot, wts, final_g, final_norm)


def kernel(x, mix_norm_g, w_in, sgu_norm_g, w_s, b_s, w_pa, w_pb, w_o, ffn_norm_g,
           dense_w_gate, dense_w_up, dense_w_down, router_w, router_b,
           expert_w_gate, expert_w_up, expert_w_down, final_norm_g):
    batch, seq, d = x.shape
    depth = mix_norm_g.shape[0]
    xf = x.reshape(batch * seq, d)
    merge_experts = lambda w: w.reshape((w.shape[0] * w.shape[1],) + w.shape[2:])
    ewg, ewu, ewd = (merge_experts(w) for w in (expert_w_gate, expert_w_up, expert_w_down))
    pre = prenorm(xf, mix_norm_g[0])
    normed = False
    for i in range(depth):
        j = i // 2
        last = i == depth - 1
        dense = i % 2 == 0
        mixed = token_mixer(xf, pre, batch, seq, i, w_in, sgu_norm_g[i], w_s[i], b_s[i],
                            w_pa, w_pb, w_o, ffn_norm_g[i] if dense else None)
        if dense:
            xf, *pre = mixed
            out = dense_ffn(xf, pre, j, dense_w_gate, dense_w_up, dense_w_down,
                            None if last else mix_norm_g[i + 1])
            if last:
                xf = out
            else:
                xf, *pre = out
        else:
            xf = moe_ffn(mixed, ffn_norm_g[i], j, router_w[j], router_b[j], ewg, ewu, ewd,
                         final_norm_g, last)
            normed = last
            if not last:
                pre = prenorm(xf, mix_norm_g[i + 1])
    if not normed:
        xf = rmsnorm(xf, final_norm_g, F32)
    return xf.reshape(batch, seq, d)
```

```python
import functools

import jax
import jax.numpy as jnp
from jax import lax
from jax.experimental import pallas as pl
from jax.experimental.pallas import tpu as pltpu

F32 = jnp.float32
BF16 = jnp.bfloat16

ATTN_HEADS = 8
HEAD_DIM = 128
MOBA_BLOCK = 256
MOBA_TOPK = 3
SGU_GROUPS = 8
SGU_GROUP_DIM = 128
SGU_CHUNK = 128
TOP_K = 2
NORM_EPS = 1e-6
STAT_LANES = 128
LOG2_E = 1.4426950408889634

V7X_VMEM_BYTES = 64 * 1024 * 1024
VMEM_LIMIT_BYTES = V7X_VMEM_BYTES - 8 * 1024 * 1024

ROW_TILE = 512
DENSE_ROW_TILE = 1024
WIDE_ROW_TILE = 2048
CAST_ROWS = 256
MOE_SUB_ROWS = 128
DISPATCH_ROWS = 512
SOFTMAX_LAG = 2
VALUES_LAG = 1
ATTN_HEADS_PER_STEP = 4


def _cparams(sem):
    return pltpu.CompilerParams(dimension_semantics=sem, vmem_limit_bytes=VMEM_LIMIT_BYTES)


def _rms(x, g):
    ms = jnp.mean(x * x, axis=-1, keepdims=True)
    return x * lax.rsqrt(ms + NORM_EPS) * g


def _rmsnorm_kernel(x_ref, g_ref, o_ref):
    o_ref[...] = _rms(x_ref[...], g_ref[...]).astype(o_ref.dtype)


def rmsnorm(x, g, out_dtype, tm=ROW_TILE):
    n, d = x.shape
    return pl.pallas_call(
        _rmsnorm_kernel,
        out_shape=jax.ShapeDtypeStruct((n, d), out_dtype),
        grid=(n // tm,),
        in_specs=[pl.BlockSpec((tm, d), lambda i: (i, 0)),
                  pl.BlockSpec((1, d), lambda i: (0, 0))],
        out_specs=pl.BlockSpec((tm, d), lambda i: (i, 0)),
        compiler_params=_cparams(("parallel",)),
        name="rmsnorm",
    )(x, g.reshape(1, d))


def _prenorm_kernel(x_ref, g_ref, o_ref, s_ref):
    x = x_ref[...]
    o_ref[...] = (x * g_ref[...]).astype(o_ref.dtype)
    s_ref[...] = jnp.broadcast_to(jnp.sum(x * x, axis=1, keepdims=True), s_ref.shape)


def prenorm(x, g, tm=ROW_TILE):
    n, d = x.shape
    return pl.pallas_call(
        _prenorm_kernel,
        out_shape=(jax.ShapeDtypeStruct((n, d), BF16), jax.ShapeDtypeStruct((n, STAT_LANES), F32)),
        grid=(n // tm,),
        in_specs=[pl.BlockSpec((tm, d), lambda i: (i, 0)),
                  pl.BlockSpec((1, d), lambda i: (0, 0))],
        out_specs=(pl.BlockSpec((tm, d), lambda i: (i, 0)),
                   pl.BlockSpec((tm, STAT_LANES), lambda i: (i, 0))),
        compiler_params=_cparams(("parallel",)),
        name="prenorm",
    )(x, g.reshape(1, d))


def _ws_kernel(texp_ref, tnext_ref, tvalid_ref, lhs_rows_ref, *refs, n_lhs, w_lhs, n_extra, epilogue,
               sub_rows, tn, scaled, round_once, normed_lhs, emit_prenorm):
    n_w = len(w_lhs)
    lhs_refs = refs[:n_lhs]
    w_hbm = refs[n_lhs:n_lhs + n_w]
    extra_refs = refs[n_lhs + n_w:n_lhs + n_w + n_extra]
    pos = n_lhs + n_w + n_extra
    scale_ref = refs[pos] if scaled else None
    pos += int(scaled)
    stats_ref = refs[pos] if normed_lhs else None
    pos += int(normed_lhs)
    gain_ref = refs[pos] if emit_prenorm else None
    pos += int(emit_prenorm)
    out_ref = refs[pos]
    prenorm_ref, sumsq_ref = refs[pos + 1:pos + 3] if emit_prenorm else (None, None)
    scratch = refs[pos + 1 + 2 * int(emit_prenorm):]
    wf_refs = scratch[:n_w]
    tm = out_ref.shape[0]
    if round_once:
        wb_refs = scratch[n_w:2 * n_w]
        sem = scratch[2 * n_w]
    else:
        sem, groups_seen = scratch[n_w:]

    n = pl.program_id(0)
    r = pl.program_id(1)
    n_col_tiles = pl.num_programs(0)

    def weight_copy(i, index, col_tile, slot):
        cols = pl.ds(pl.multiple_of(col_tile * tn, tn), tn)
        dst, dst_sem = (wf_refs[i], sem.at[i]) if round_once else (wf_refs[i].at[slot], sem.at[i, slot])
        return pltpu.make_async_copy(w_hbm[i].at[index, :, cols], dst, dst_sem)

    def start_weights(index, col_tile, slot):
        for i in range(n_w):
            weight_copy(i, index, col_tile, slot).start()

    def start_following(slot):
        following = tnext_ref[r]

        @pl.when(following >= 0)
        def _():
            start_weights(following, n, slot)

        @pl.when(jnp.logical_and(following < 0, n + 1 < n_col_tiles))
        def _():
            start_weights(texp_ref[0], n + 1, slot)

    def rounded(w):
        if scaled:
            w = w * scale_ref[...]
        return w.astype(BF16)

    @pl.when(jnp.logical_and(n == 0, r == 0))
    def _():
        if not round_once:
            groups_seen[0] = 0
        start_weights(texp_ref[0], 0, 0)

    new_group = jnp.logical_or(r == 0, texp_ref[r] != texp_ref[jnp.maximum(r - 1, 0)])

    @pl.when(new_group)
    def _():
        if round_once:
            for i in range(n_w):
                weight_copy(i, texp_ref[r], n, 0).wait()
            for wf_ref, wb_ref in zip(wf_refs, wb_refs):
                def cast_rows(c, carry, wf_ref=wf_ref, wb_ref=wb_ref):
                    rows = pl.ds(pl.multiple_of(c * CAST_ROWS, CAST_ROWS), CAST_ROWS)
                    wb_ref[rows, :] = rounded(wf_ref[rows, :])
                    return carry

                lax.fori_loop(0, wb_ref.shape[0] // CAST_ROWS, cast_rows, 0)
            start_following(0)
        else:
            seen = groups_seen[0]
            slot = jnp.bitwise_and(seen, 1)
            groups_seen[0] = seen + 1
            for i in range(n_w):
                weight_copy(i, texp_ref[r], n, slot).wait()
            start_following(1 - slot)

    def operand(i):
        if round_once:
            return wb_refs[i][...]
        return rounded(wf_refs[i][jnp.bitwise_and(groups_seen[0] - 1, 1)])

    def compute(n_live):
        rows = slice(0, n_live)
        if n_live > 0:
            prods = [jnp.dot(lhs_refs[w_lhs[i]][rows, :], operand(i),
                             preferred_element_type=F32) for i in range(n_w)]
            if normed_lhs:
                width = lhs_refs[0].shape[1]
                sumsq = sum(stats_ref[rows, p:p + 1] for p in range(0, stats_ref.shape[1], STAT_LANES))
                inv_rms = lax.rsqrt(sumsq / width + NORM_EPS)
                prods = [p * inv_rms for p in prods]
            res = epilogue(prods, [e[rows, :] for e in extra_refs])
            out_ref[rows, :] = res.astype(out_ref.dtype)
            if emit_prenorm:
                prenorm_ref[rows, :] = (res * gain_ref[...]).astype(prenorm_ref.dtype)
                sumsq_ref[rows, :] = jnp.broadcast_to(jnp.sum(res * res, axis=1, keepdims=True),
                                                      (n_live, STAT_LANES))
        if n_live < tm:
            for ref in (out_ref, prenorm_ref, sumsq_ref):
                if ref is not None:
                    ref[n_live:tm, :] = jnp.zeros((tm - n_live, ref.shape[1]), ref.dtype)

    if sub_rows is None:
        compute(tm)
    else:
        live_chunks = (tvalid_ref[r] + (sub_rows - 1)) // sub_rows
        for c in range(tm // sub_rows + 1):
            pl.when(live_chunks == c)(functools.partial(compute, c * sub_rows))


def ws_matmul(lhs, ws, w_lhs, extras, extra_col_off, epilogue, out_dtype, tn, texp=None,
              tvalid=None, sub_rows=None, w_base=0, col_scale=None, round_once=False,
              row_sumsq=None, next_gain=None, lhs_rows=None, tm=ROW_TILE, name="ws_matmul"):
    m = lhs[0].shape[0]
    n_total = ws[0].shape[2]
    n_rows = m // tm
    if texp is None:
        texp = jnp.zeros((n_rows,), jnp.int32)
    if tvalid is None:
        tvalid = jnp.full((n_rows,), tm, jnp.int32)
    no_next = jnp.iinfo(jnp.int32).max
    later = jnp.min(jnp.where(texp[None, :] > texp[:, None], texp[None, :], no_next), axis=1)
    tnext = jnp.where(later == no_next, -1, later + w_base).astype(jnp.int32)
    texp = texp + w_base
    if lhs_rows is None:
        lhs_rows = jnp.arange(n_rows, dtype=jnp.int32)
    in_specs = []
    for a in lhs:
        in_specs.append(pl.BlockSpec((tm, a.shape[1]), lambda n, r, t, tn_, tv, lr: (lr[r], 0)))
    for w in ws:
        in_specs.append(pl.BlockSpec(memory_space=pl.ANY))
    for off in extra_col_off:
        in_specs.append(pl.BlockSpec((tm, tn), lambda n, r, *_, off=off: (r, off + n)))
    scale_args = []
    if col_scale is not None:
        in_specs.append(pl.BlockSpec((1, tn), lambda n, r, *_: (0, n)))
        scale_args.append(col_scale.reshape(1, n_total))
    if row_sumsq is not None:
        in_specs.append(pl.BlockSpec((tm, row_sumsq.shape[1]), lambda n, r, *_: (r, 0)))
        scale_args.append(row_sumsq)
    out_shapes = [jax.ShapeDtypeStruct((m, n_total), out_dtype)]
    out_specs = [pl.BlockSpec((tm, tn), lambda n, r, *_: (r, n))]
    if next_gain is not None:
        in_specs.append(pl.BlockSpec((1, tn), lambda n, r, *_: (0, n)))
        scale_args.append(next_gain.reshape(1, n_total))
        out_shapes += [jax.ShapeDtypeStruct((m, n_total), BF16),
                       jax.ShapeDtypeStruct((m, (n_total // tn) * STAT_LANES), F32)]
        out_specs += [pl.BlockSpec((tm, tn), lambda n, r, *_: (r, n)),
                      pl.BlockSpec((tm, STAT_LANES), lambda n, r, *_: (r, n))]
    kern = functools.partial(_ws_kernel, n_lhs=len(lhs), w_lhs=tuple(w_lhs),
                             n_extra=len(extras), epilogue=epilogue, sub_rows=sub_rows, tn=tn,
                             scaled=col_scale is not None, round_once=round_once,
                             normed_lhs=row_sumsq is not None, emit_prenorm=next_gain is not None)
    if round_once:
        scratch = ([pltpu.VMEM((w.shape[1], tn), F32) for w in ws]
                   + [pltpu.VMEM((w.shape[1], tn), BF16) for w in ws]
                   + [pltpu.SemaphoreType.DMA((len(ws),))])
    else:
        scratch = ([pltpu.VMEM((2, w.shape[1], tn), F32) for w in ws]
                   + [pltpu.SemaphoreType.DMA((len(ws), 2)), pltpu.SMEM((1,), jnp.int32)])
    outs = pl.pallas_call(
        kern,
        out_shape=out_shapes,
        grid_spec=pltpu.PrefetchScalarGridSpec(
            num_scalar_prefetch=4,
            grid=(n_total // tn, n_rows),
            in_specs=in_specs,
            out_specs=out_specs,
            scratch_shapes=scratch,
        ),
        compiler_params=_cparams(("arbitrary", "arbitrary")),
        name=name,
    )(texp, tnext, tvalid, lhs_rows, *lhs, *ws, *extras, *scale_args)
    return outs if next_gain is not None else outs[0]


def _ep_plain(prods, extras):
    return prods[0]


def _ep_residual(prods, extras):
    return extras[0] + prods[0]


def _ep_swiglu(prods, extras):
    g, u = prods
    return (g * jax.nn.sigmoid(g)) * u


def _ep_merge(prods, extras):
    ga, gb = extras
    return (jax.nn.sigmoid(ga.astype(F32)) * prods[0]
            + jax.nn.sigmoid(gb.astype(F32)) * prods[1])


def _attn_head(q_ref, k_ref, v_ref, o_ref, cols, *, nb, blk, topk):
    seq = k_ref.shape[0]
    neg_inf = jnp.float32(-jnp.inf)
    avg = jnp.where(lax.broadcasted_iota(jnp.int32, (nb, seq), 1) // blk
                    == lax.broadcasted_iota(jnp.int32, (nb, seq), 0), 1.0 / blk, 0.0).astype(BF16)
    k_mean = jnp.dot(avg, k_ref[:, cols], preferred_element_type=F32)
    k_mean_hi = k_mean.astype(BF16).astype(F32)
    k_mean_parts = jnp.concatenate([k_mean_hi, k_mean - k_mean_hi], axis=0).astype(BF16)
    blk_id = lax.broadcasted_iota(jnp.int32, (nb, blk), 0)
    causal = (lax.broadcasted_iota(jnp.int32, (blk, blk), 0)
              <= lax.broadcasted_iota(jnp.int32, (blk, blk), 1))
    nt_dims = (((1,), (1,)), ((), ()))
    v_t = v_ref[:, cols].T

    def scores(n):
        qn = q_ref[n * blk:(n + 1) * blk, cols]
        s = lax.dot_general(k_ref[0:(n + 1) * blk, cols], qn, nt_dims, preferred_element_type=F32)
        gate_parts = None
        if n > topk:
            gate_parts = lax.dot_general(k_mean_parts, qn, nt_dims, preferred_element_type=F32)
        return s, gate_parts

    def masked_softmax(n, s, gate_parts):
        past = [s[j * blk:(j + 1) * blk, :] for j in range(n)]
        if n > topk:
            gate = gate_parts[:nb] + gate_parts[nb:]
            beaten = jnp.zeros((nb, blk), F32)
            for j in range(n):
                gj = gate[j:j + 1, :]
                wins = jnp.logical_or(gj > gate, jnp.logical_and(gj == gate, j < blk_id))
                beaten = beaten + wins.astype(F32)
            bias = jnp.where(beaten < topk, 0.0, neg_inf)
            past = [past[j] + bias[j:j + 1, :] for j in range(n)]
        own = jnp.where(causal, s[n * blk:(n + 1) * blk, :], neg_inf)
        s = jnp.concatenate(past + [own], axis=0)
        m = jnp.max(s, axis=0, keepdims=True)
        p = jnp.exp2(s - m)
        return p.astype(BF16), jnp.sum(p, axis=0, keepdims=True)

    def weighted_values(n, p, l):
        o_t = jnp.dot(v_t[:, 0:(n + 1) * blk], p, preferred_element_type=F32)
        o_ref[n * blk:(n + 1) * blk, cols] = (o_t / l).T.astype(o_ref.dtype)

    order = list(range(nb - 1, -1, -1))
    s_out, p_out = {}, {}
    for t in range(nb + SOFTMAX_LAG + VALUES_LAG):
        if t < nb:
            s_out[order[t]] = scores(order[t])
        if 0 <= t - SOFTMAX_LAG < nb:
            n = order[t - SOFTMAX_LAG]
            p_out[n] = masked_softmax(n, *s_out.pop(n))
        if 0 <= t - SOFTMAX_LAG - VALUES_LAG < nb:
            n = order[t - SOFTMAX_LAG - VALUES_LAG]
            weighted_values(n, *p_out.pop(n))
        yield


def _attn_kernel(q_ref, k_ref, v_ref, o_ref, *, heads, dh, nb, blk, topk):
    emitters = [_attn_head(q_ref, k_ref, v_ref, o_ref, slice(h * dh, (h + 1) * dh),
                           nb=nb, blk=blk, topk=topk) for h in range(heads)]
    while emitters:
        emitters = [e for e in emitters if next(e, StopIteration) is not StopIteration]


def moba_attention(z, batch, seq):
    nb = seq // MOBA_BLOCK
    kern = functools.partial(_attn_kernel, heads=ATTN_HEADS_PER_STEP, dh=HEAD_DIM, nb=nb,
                             blk=MOBA_BLOCK, topk=MOBA_TOPK)
    width = ATTN_HEADS_PER_STEP * HEAD_DIM
    steps = ATTN_HEADS // ATTN_HEADS_PER_STEP
    blockspec = lambda off: pl.BlockSpec((None, seq, width), lambda b, h: (b, 0, off + h))
    return pl.pallas_call(
        kern,
        out_shape=jax.ShapeDtypeStruct((batch, seq, ATTN_HEADS * HEAD_DIM), BF16),
        grid=(batch, steps),
        in_specs=[blockspec(0), blockspec(steps), blockspec(2 * steps)],
        out_specs=pl.BlockSpec((None, seq, width), lambda b, h: (b, 0, h)),
        compiler_params=_cparams(("parallel", "parallel")),
        name="moba_attention",
    )(z, z, z)


GELU_C = 0.7978845608028654
GELU_A = 0.044715


def _gelu_tanh(x):
    half = 0.5 * x
    inner = x * (GELU_C + (GELU_C * GELU_A) * (x * x))
    return half + half * jnp.tanh(inner)


def _sgu_kernel(u_ref, v_ref, g_ref, w_ref, bt_ref, o_ref, w_lower, *, chunk, groups, gd):
    tm = u_ref.shape[0]

    @pl.when(pl.program_id(0) == 0)
    def _():
        lower = (lax.broadcasted_iota(jnp.int32, (chunk, chunk), 1)
                 <= lax.broadcasted_iota(jnp.int32, (chunk, chunk), 0))
        for g in range(groups):
            w_lower[g] = jnp.where(lower, w_ref[g], 0.0).astype(BF16)

    u = _gelu_tanh(u_ref[...].astype(F32))
    v = _gelu_tanh(v_ref[...].astype(F32))
    mu = jnp.mean(v, axis=-1, keepdims=True)
    vc = v - mu
    var = jnp.mean(vc * vc, axis=-1, keepdims=True)
    vn = (vc * lax.rsqrt(var + NORM_EPS) * g_ref[...]).astype(BF16)
    for g in range(groups):
        w = w_lower[g]
        bias = bt_ref[:, g:g + 1]
        cols = slice(g * gd, (g + 1) * gd)
        for c in range(tm // chunk):
            rows = slice(c * chunk, (c + 1) * chunk)
            mixed = jnp.dot(w, vn[rows, cols], preferred_element_type=F32) + bias
            o_ref[rows, cols] = (u[rows, cols] * mixed).astype(o_ref.dtype)


def spatial_gating(z, g_sgu, w_s, b_s, col_u, col_v, tm=256):
    n = z.shape[0]
    width = SGU_GROUPS * SGU_GROUP_DIM
    kern = functools.partial(_sgu_kernel, chunk=SGU_CHUNK, groups=SGU_GROUPS, gd=SGU_GROUP_DIM)
    return pl.pallas_call(
        kern,
        out_shape=jax.ShapeDtypeStruct((n, width), BF16),
        grid=(n // tm,),
        in_specs=[pl.BlockSpec((tm, width), lambda i: (i, col_u)),
                  pl.BlockSpec((tm, width), lambda i: (i, col_v)),
                  pl.BlockSpec((1, width), lambda i: (0, 0)),
                  pl.BlockSpec(w_s.shape, lambda i: (0, 0, 0)),
                  pl.BlockSpec((SGU_CHUNK, SGU_GROUPS), lambda i: (0, 0))],
        out_specs=pl.BlockSpec((tm, width), lambda i: (i, 0)),
        scratch_shapes=[pltpu.VMEM(w_s.shape, BF16)],
        compiler_params=_cparams(("arbitrary",)),
        name="spatial_gating",
    )(z, z, g_sgu.reshape(1, width), w_s, b_s.T)


def _router_kernel(x_ref, g_ref, rwt_ref, rb_ref, eidx_ref, wts_ref, rank_ref, cnt_ref):
    i = pl.program_id(0)
    ne = rwt_ref.shape[0]
    tm = x_ref.shape[0]

    @pl.when(i == 0)
    def _():
        cnt_ref[...] = jnp.zeros_like(cnt_ref)

    h = _rms(x_ref[...], g_ref[...])
    h_hi = h.astype(BF16)
    h_lo = (h - h_hi.astype(F32)).astype(BF16)
    rw = rwt_ref[...]
    rw_hi = rw.astype(BF16)
    rw_hi_f32 = rw_hi.astype(F32)
    rw_parts = jnp.concatenate([rw_hi_f32, rw - rw_hi_f32], axis=0).astype(BF16)
    nt_dims = (((1,), (1,)), ((), ()))
    both = lax.dot_general(rw_parts, h_hi, nt_dims, preferred_element_type=F32)
    logits = (both[:ne] + both[ne:]
              + lax.dot_general(rw_hi, h_lo, nt_dims, preferred_element_type=F32)
              + rb_ref[...])
    row = lax.broadcasted_iota(jnp.int32, (ne, tm), 0).astype(F32)
    neg_inf = jnp.float32(-jnp.inf)
    m0 = jnp.max(logits, axis=0, keepdims=True)
    i0 = jnp.min(jnp.where(logits == m0, row, float(ne)), axis=0, keepdims=True)
    rest = jnp.where(row == i0, neg_inf, logits)
    m1 = jnp.max(rest, axis=0, keepdims=True)
    i1 = jnp.min(jnp.where(jnp.logical_and(rest == m1, row != i0), row, float(ne)),
                 axis=0, keepdims=True)
    e1 = jnp.exp(m1 - m0)
    denom = 1.0 + e1
    eidx_ref[0:1, :] = i0.astype(jnp.int32)
    eidx_ref[1:2, :] = i1.astype(jnp.int32)
    wts_ref[0:1, :] = 1.0 / denom
    wts_ref[1:2, :] = e1 / denom

    hit0 = row == i0
    hit1 = row == i1
    chosen = jnp.logical_or(hit0, hit1).astype(BF16)
    earlier = (lax.broadcasted_iota(jnp.int32, (tm, tm), 0)
               < lax.broadcasted_iota(jnp.int32, (tm, tm), 1)).astype(BF16)
    before = jnp.dot(chosen, earlier, preferred_element_type=F32) + cnt_ref[:, 0:1]
    rank_ref[0:1, :] = jnp.sum(jnp.where(hit0, before, 0.0), axis=0, keepdims=True).astype(jnp.int32)
    rank_ref[1:2, :] = jnp.sum(jnp.where(hit1, before, 0.0), axis=0, keepdims=True).astype(jnp.int32)
    cnt_ref[...] = cnt_ref[...] + jnp.sum(chosen.astype(F32), axis=1, keepdims=True)


def moe_router(x, g, router_w, router_b, tm=ROW_TILE):
    n, d = x.shape
    ne = router_w.shape[1]
    out_shapes = (jax.ShapeDtypeStruct((TOP_K, n), jnp.int32),
                  jax.ShapeDtypeStruct((TOP_K, n), F32),
                  jax.ShapeDtypeStruct((TOP_K, n), jnp.int32),
                  jax.ShapeDtypeStruct((ne, 128), F32))
    tok_spec = pl.BlockSpec((TOP_K, tm), lambda i: (0, i))
    return pl.pallas_call(
        _router_kernel,
        out_shape=out_shapes,
        grid=(n // tm,),
        in_specs=[pl.BlockSpec((tm, d), lambda i: (i, 0)),
                  pl.BlockSpec((1, d), lambda i: (0, 0)),
                  pl.BlockSpec((ne, d), lambda i: (0, 0)),
                  pl.BlockSpec((ne, 1), lambda i: (0, 0))],
        out_specs=(tok_spec, tok_spec, tok_spec, pl.BlockSpec((ne, 128), lambda i: (0, 0))),
        compiler_params=_cparams(("arbitrary",)),
        name="moe_router",
    )(x, g.reshape(1, d), router_w.T, router_b.reshape(ne, 1))


ISSUE_UNROLL = 8


def _start_row_gather(src_hbm, idx_ref, idx_base, n_rows, buf, slot, row_base, sem):
    def body(j, carry):
        row = idx_ref[idx_base + j]
        pltpu.make_async_copy(src_hbm.at[pl.ds(row, 1), :],
                              buf.at[slot, pl.ds(row_base + j, 1), :], sem.at[slot]).start()
        return carry
    lax.fori_loop(0, n_rows, body, 0, unroll=ISSUE_UNROLL)


def _wait_row_gather(src_hbm, buf, slot, sem):
    n_rows = buf.shape[1]
    pltpu.make_async_copy(src_hbm.at[pl.ds(0, n_rows), :], buf.at[slot], sem.at[slot]).wait()


def _dispatch_kernel(src_ref, used_ref, x_hbm, g_ref, o_ref, buf, sem, *, tg):
    i = pl.program_id(0)
    n_used = used_ref[0]

    @pl.when(i == 0)
    def _():
        _start_row_gather(x_hbm, src_ref, 0, tg, buf, 0, 0, sem)

    @pl.when(i + 1 < n_used)
    def _():
        _start_row_gather(x_hbm, src_ref, (i + 1) * tg, tg, buf, jnp.bitwise_and(i + 1, 1), 0, sem)

    @pl.when(i < n_used)
    def _():
        slot = jnp.bitwise_and(i, 1)
        _wait_row_gather(x_hbm, buf, slot, sem)
        o_ref[...] = _rms(buf[slot], g_ref[...]).astype(o_ref.dtype)

    @pl.when(i >= n_used)
    def _():
        o_ref[...] = jnp.zeros_like(o_ref)


def moe_dispatch(x, g, src, n_used_steps, tg):
    n, d = x.shape
    n_slots = src.shape[0]
    kern = functools.partial(_dispatch_kernel, tg=tg)
    return pl.pallas_call(
        kern,
        out_shape=jax.ShapeDtypeStruct((n_slots, d), BF16),
        grid_spec=pltpu.PrefetchScalarGridSpec(
            num_scalar_prefetch=2,
            grid=(n_slots // tg,),
            in_specs=[pl.BlockSpec(memory_space=pl.ANY),
                      pl.BlockSpec((1, d), lambda i, *_: (0, 0))],
            out_specs=pl.BlockSpec((tg, d), lambda i, *_: (i, 0)),
            scratch_shapes=[pltpu.VMEM((2, tg, d), F32), pltpu.SemaphoreType.DMA((2,))],
        ),
        compiler_params=_cparams(("arbitrary",)),
        name="moe_dispatch",
    )(src, n_used_steps, x, g.reshape(1, d))


def _combine_kernel(slot_ref, y_hbm, x_ref, w_ref, g_ref, o_ref, buf, sem, *, tc, final_norm):
    i = pl.program_id(0)
    nsteps = pl.num_programs(0)
    n_tok = nsteps * tc

    def fetch(step, slot):
        for k in range(TOP_K):
            _start_row_gather(y_hbm, slot_ref, k * n_tok + step * tc, tc, buf, slot, k * tc, sem)

    @pl.when(i == 0)
    def _():
        fetch(0, 0)

    @pl.when(i + 1 < nsteps)
    def _():
        fetch(i + 1, jnp.bitwise_and(i + 1, 1))

    slot = jnp.bitwise_and(i, 1)
    _wait_row_gather(y_hbm, buf, slot, sem)
    acc = x_ref[...]
    for k in range(TOP_K):
        acc = acc + w_ref[:, k:k + 1] * buf[slot, k * tc:(k + 1) * tc, :]
    if final_norm:
        acc = _rms(acc, g_ref[...])
    o_ref[...] = acc


def moe_combine(y, x, slot, wts, g, final_norm, tc=256):
    n, d = x.shape
    kern = functools.partial(_combine_kernel, tc=tc, final_norm=final_norm)
    return pl.pallas_call(
        kern,
        out_shape=jax.ShapeDtypeStruct((n, d), F32),
        grid_spec=pltpu.PrefetchScalarGridSpec(
            num_scalar_prefetch=1,
            grid=(n // tc,),
            in_specs=[pl.BlockSpec(memory_space=pl.ANY),
                      pl.BlockSpec((tc, d), lambda i, s: (i, 0)),
                      pl.BlockSpec((tc, TOP_K), lambda i, s: (i, 0)),
                      pl.BlockSpec((1, d), lambda i, s: (0, 0))],
            out_specs=pl.BlockSpec((tc, d), lambda i, s: (i, 0)),
            scratch_shapes=[pltpu.VMEM((2, TOP_K * tc, d), F32), pltpu.SemaphoreType.DMA((2,))],
        ),
        compiler_params=_cparams(("arbitrary",)),
        name="moe_combine",
    )(slot.reshape(-1), y, x, wts.T, g.reshape(1, d))


def token_mixer(x, pre, batch, seq, layer, w_in, g_sgu, w_s, b_s, w_pa, w_pb, w_o, next_gain):
    n, d = x.shape
    aw = ATTN_HEADS * HEAD_DIM
    sw = SGU_GROUPS * SGU_GROUP_DIM
    tn = 1024
    xg, sumsq = pre
    q_scale = jnp.where(jnp.arange(w_in.shape[2]) < aw, HEAD_DIM ** -0.5 * LOG2_E, 1.0).astype(F32)
    in_tm = WIDE_ROW_TILE if sumsq.shape[1] == STAT_LANES else DENSE_ROW_TILE
    z = ws_matmul([xg], [w_in], [0], [], [], _ep_plain, BF16, tn, w_base=layer,
                  col_scale=q_scale, row_sumsq=sumsq, tm=in_tm, name="in_proj")
    attn = moba_attention(z.reshape(batch, seq, z.shape[1]), batch, seq).reshape(n, aw)
    sgu = spatial_gating(z, g_sgu, w_s, b_s, (3 * aw) // sw, (3 * aw + sw) // sw)
    gate_col = (3 * aw + 2 * sw) // tn
    merged = ws_matmul([attn, sgu], [w_pa, w_pb], [0, 1], [z, z],
                       [gate_col, gate_col + d // tn], _ep_merge, BF16, tn, w_base=layer,
                       tm=DENSE_ROW_TILE, name="branch_merge")
    return ws_matmul([merged], [w_o], [0], [x], [0], _ep_residual, F32, tn, w_base=layer,
                     next_gain=next_gain, tm=DENSE_ROW_TILE, name="out_proj")


def dense_ffn(x, pre, j, wg, wu, wd, next_gain):
    xg, sumsq = pre
    act = ws_matmul([xg], [wg, wu], [0, 0], [], [], _ep_swiglu, BF16, 512, w_base=j,
                    round_once=True, row_sumsq=sumsq, tm=DENSE_ROW_TILE, name="ffn_up")
    return ws_matmul([act], [wd], [0], [x], [0], _ep_residual, F32, 512, w_base=j,
                     round_once=True, next_gain=next_gain, tm=DENSE_ROW_TILE, name="ffn_down")


def moe_ffn(x, norm_g, j, router_w, router_b, wg, wu, wd, final_g, final_norm):
    n, d = x.shape
    ne = router_w.shape[1]
    tm = ROW_TILE
    eidx, wts, rank, cnt = moe_router(x, norm_g, router_w, router_b)
    counts = cnt[:, 0].astype(jnp.int32)
    padded = ((counts + tm - 1) // tm) * tm
    ends = jnp.cumsum(padded)
    starts = ends - padded
    group_start = jnp.sum(jnp.where(eidx[..., None] == jnp.arange(ne), starts, 0), axis=-1)
    slot = group_start + rank
    n_tiles = (TOP_K * n) // tm + ne
    tok = jnp.tile(jnp.arange(n, dtype=jnp.int32), TOP_K)
    src = (jnp.arange(n_tiles * tm, dtype=jnp.int32) % n).at[slot.reshape(-1)].set(tok)
    tile_row = jnp.arange(n_tiles, dtype=jnp.int32) * tm
    last_used = jnp.max(jnp.where(counts > 0, jnp.arange(ne, dtype=jnp.int32), 0))
    texp = jnp.minimum(jnp.sum(tile_row[:, None] >= ends[None, :], axis=1), last_used).astype(jnp.int32)
    group_end = jnp.sum(jnp.where(texp[:, None] == jnp.arange(ne), starts + counts, 0), axis=-1)
    tvalid = jnp.clip(group_end - tile_row, 0, tm).astype(jnp.int32)

    n_used_tiles = ends[-1] // tm
    lhs_rows = jnp.minimum(jnp.arange(n_tiles, dtype=jnp.int32), n_used_tiles - 1)
    n_used_steps = (n_used_tiles * (tm // DISPATCH_ROWS)).reshape(1).astype(jnp.int32)

    hs = moe_dispatch(x, norm_g, src, n_used_steps, DISPATCH_ROWS)
    act = ws_matmul([hs], [wg, wu], [0, 0], [], [], _ep_swiglu, BF16, 512, texp=texp,
                    tvalid=tvalid, sub_rows=MOE_SUB_ROWS, w_base=j * ne, lhs_rows=lhs_rows,
                    name="moe_up")
    y = ws_matmul([act], [wd], [0], [], [], _ep_plain, F32, 1024, texp=texp, tvalid=tvalid,
                  sub_rows=MOE_SUB_ROWS, w_base=j * ne, round_once=True, lhs_rows=lhs_rows,
                  name="moe_down")
    return moe_combine(y, x, slot, wts, final_g, final_norm)


def kernel(x, mix_norm_g, w_in, sgu_norm_g, w_s, b_s, w_pa, w_pb, w_o, ffn_norm_g,
           dense_w_gate, dense_w_up, dense_w_down, router_w, router_b,
           expert_w_gate, expert_w_up, expert_w_down, final_norm_g):
    batch, seq, d = x.shape
    depth = mix_norm_g.shape[0]
    xf = x.reshape(batch * seq, d)
    merge_experts = lambda w: w.reshape((w.shape[0] * w.shape[1],) + w.shape[2:])
    ewg, ewu, ewd = (merge_experts(w) for w in (expert_w_gate, expert_w_up, expert_w_down))
    pre = prenorm(xf, mix_norm_g[0])
    normed = False
    for i in range(depth):
        j = i // 2
        last = i == depth - 1
        dense = i % 2 == 0
        mixed = token_mixer(xf, pre, batch, seq, i, w_in, sgu_norm_g[i], w_s[i], b_s[i],
                            w_pa, w_pb, w_o, ffn_norm_g[i] if dense else None)
        if dense:
            xf, *pre = mixed
            out = dense_ffn(xf, pre, j, dense_w_gate, dense_w_up, dense_w_down,
                            None if last else mix_norm_g[i + 1])
            if last:
                xf = out
            else:
                xf, *pre = out
        else:
            xf = moe_ffn(mixed, ffn_norm_g[i], j, router_w[j], router_b[j], ewg, ewu, ewd,
                         final_norm_g, last)
            normed = last
            if not last:
                pre = prenorm(xf, mix_norm_g[i + 1])
    if not normed:
        xf = rmsnorm(xf, final_norm_g, F32)
    return xf.reshape(batch, seq, d)
```
